```python
import math
import jax, jax.numpy as jnp
from jax import lax
import numpy as np

D_MODEL = 1024
BATCH = 2
SEQ = 8192
DEPTH = 4

GRID_W = 64
CTX_LEN = 256
EPS = 1e-6
NEG_INF = -1e30
ROPE_BASE = 10000.0

HY_WIDTH = D_MODEL // 2
HY_SHORT = 3
HY_BANDS = 16
HY_EMB = 1 + 2 * HY_BANDS
HY_FILTER_HIDDEN = 64
HY_SHIFT = 0.05
HY_FAST_DECAY = math.log(1e-2) / 0.3
HY_SLOW_DECAY = math.log(1e-2) / 1.5

S5_WIDTH = D_MODEL - HY_WIDTH
S5_GROUP = 16
S5_GROUPS = S5_WIDTH // S5_GROUP
S5_STATE = 64

RET_HEADS = 4
RET_DK = D_MODEL // 16
RET_DV = 2 * RET_DK
RET_CHUNK = 128
RET_WIDTH = RET_HEADS * (2 * RET_DK + 2 * RET_DV)

SWA_Q_HEADS = 8
SWA_KV_HEADS = 2
SWA_HD = D_MODEL // 16
SWA_WINDOW = 128
SWA_BLOCK = 128

AB_IN = 3 * HY_WIDTH + S5_WIDTH
CD_IN = RET_WIDTH + (SWA_Q_HEADS + 2 * SWA_KV_HEADS) * SWA_HD
MIX_OUT = HY_WIDTH + S5_WIDTH

N_EXPERTS = 32
TOP_K = 4
D_FF = D_MODEL
SWIGLU_LIMIT = 7.0
SWIGLU_ALPHA = 1.702
MOE_BLOCK = 128

kernel_name = 'hyena_s5_retention_swa_moe_hybrid'

F32 = jnp.float32


def rmsnorm(x, g):
    xf = x.astype(F32)
    y = xf * lax.rsqrt(jnp.mean(xf * xf, axis=-1, keepdims=True) + EPS)
    return (y * g.astype(F32)).astype(x.dtype)


def rope_1d(x, pos, base):
    half = x.shape[-1] // 2
    inv = base ** (-jnp.arange(half, dtype=F32) / half)
    ang = pos[:, None] * inv[None]
    cos = jnp.cos(ang)[:, None, :].astype(x.dtype)
    sin = jnp.sin(ang)[:, None, :].astype(x.dtype)
    x1, x2 = x[..., :half], x[..., half:]
    return jnp.concatenate([x1 * cos - x2 * sin, x1 * sin + x2 * cos], axis=-1)


def grid_positions(L):
    n_rows = L // GRID_W
    row = jnp.repeat(jnp.arange(n_rows, dtype=F32), GRID_W)
    col = jnp.tile(jnp.arange(GRID_W, dtype=F32), n_rows)
    return row, col


def rope_2d(x, row, col):
    h = x.shape[-1] // 2
    return jnp.concatenate([rope_1d(x[..., :h], row, ROPE_BASE), rope_1d(x[..., h:], col, ROPE_BASE)], axis=-1)


def hyena_filter(L, w1, b1, w2, b2, w3):
    pos = jnp.arange(L, dtype=F32)
    t = (pos / L)[:, None]
    bands = jnp.linspace(1e-4, HY_BANDS - 1, HY_BANDS, dtype=F32)[None]
    w = (2.0 * math.pi * pos / L)[:, None]
    feats = jnp.concatenate([t, jnp.cos(bands * w), -jnp.sin(bands * w)], axis=-1)
    h = jnp.sin(feats @ w1.astype(F32) + b1.astype(F32))
    h = jnp.sin(h @ w2.astype(F32) + b2.astype(F32))
    h = h @ w3.astype(F32)
    lag = jnp.abs(pos - L // 2) / (L / 2)
    deltas = jnp.abs(jnp.linspace(HY_FAST_DECAY, HY_SLOW_DECAY, HY_WIDTH, dtype=F32))
    h = h * (jnp.exp(-lag[:, None] * deltas[None]) + HY_SHIFT)
    return h / jnp.sum(jnp.abs(h), axis=0, keepdims=True)


def short_conv(u, w, b):
    ch = u.shape[-1]
    y = lax.conv_general_dilated(u, w[:, None, :].astype(u.dtype), window_strides=(1,),
                                 padding=[(HY_SHORT // 2, HY_SHORT // 2)],
                                 dimension_numbers=('NWC', 'WIO', 'NWC'), feature_group_count=ch)
    return y + b


def long_conv_centred(z, h):
    L = z.shape[1]
    n = 2 * L
    zf = jnp.fft.rfft(z.astype(F32), n=n, axis=1)
    hf = jnp.fft.rfft(h, n=n, axis=0)
    y = jnp.fft.irfft(zf * hf[None], n=n, axis=1)
    return y[:, L // 2: L // 2 + L]


def hyena(p, short_w, short_b, w1, b1, w2, b2, w3, hy_bias):
    L = p.shape[1]
    u = short_conv(p, short_w, short_b)
    x0, x1, v = jnp.split(u, 3, axis=-1)
    z = v * x1
    h = hyena_filter(L, w1, b1, w2, b2, w3)
    y = long_conv_centred(z, h).astype(z.dtype) + hy_bias * z
    return x0 * y


def s5_discretize(lam_re, lam_im, log_dt, b_re, b_im, c_re, c_im):
    lam = lax.complex(jnp.minimum(lam_re.astype(F32), -1e-4), lam_im.astype(F32))
    dt = jnp.exp(log_dt.astype(F32))[:, None]
    lam_bar = jnp.exp(lam * dt)
    b = lax.complex(b_re.astype(F32), b_im.astype(F32))
    b_bar = ((lam_bar - 1.0) / lam)[..., None] * b
    cm = lax.complex(c_re.astype(F32), c_im.astype(F32))
    return lam_bar, b_bar, cm


def diag_scan(bu, lam_bar, s0):
    bu = bu.at[:, 0].add(lam_bar[None] * s0)
    a = jnp.broadcast_to(lam_bar, bu.shape)

    def combine(e1, e2):
        a1, b1 = e1
        a2, b2 = e2
        return a1 * a2, a2 * b1 + b2

    return lax.associative_scan(combine, (a, bu), axis=1)[1]


def s5_mixer(u_lat, u_ctx, lam_re, lam_im, log_dt, b_re, b_im, c_re, c_im, d_skip, glu_w, glu_b, with_ctx):
    bsz, L, W = u_lat.shape
    Lc = u_ctx.shape[1]
    ul = u_lat.astype(F32).reshape(bsz, L, S5_GROUPS, S5_GROUP).astype(jnp.complex64)
    uc = u_ctx.astype(F32).reshape(bsz, Lc, S5_GROUPS, S5_GROUP).astype(jnp.complex64)
    y_lat = d_skip.astype(F32) * u_lat.astype(F32)
    y_ctx = d_skip.astype(F32) * u_ctx.astype(F32)
    s_zero = jnp.zeros((bsz, S5_GROUPS, S5_STATE), jnp.complex64)
    for d in range(2):
        lam_bar, b_bar, cm = s5_discretize(lam_re[d], lam_im[d], log_dt[d], b_re[d], b_im[d], c_re[d], c_im[d])
        uc_d = uc[:, ::-1] if d == 1 else uc
        ul_d = ul[:, ::-1] if d == 1 else ul
        s_ctx = diag_scan(jnp.einsum('blgh,gph->blgp', uc_d, b_bar), lam_bar, s_zero)
        s_lat = diag_scan(jnp.einsum('blgh,gph->blgp', ul_d, b_bar), lam_bar, s_ctx[:, -1])
        r_lat = jnp.real(jnp.einsum('blgp,ghp->blgh', s_lat, cm)).reshape(bsz, L, W)
        y_lat = y_lat + (r_lat[:, ::-1] if d == 1 else r_lat)
        if with_ctx:
            r_ctx = jnp.real(jnp.einsum('blgp,ghp->blgh', s_ctx, cm)).reshape(bsz, Lc, W)
            y_ctx = y_ctx + (r_ctx[:, ::-1] if d == 1 else r_ctx)

    def glu(y):
        y = jax.nn.gelu(y)
        return y * jax.nn.sigmoid(y @ glu_w.astype(F32) + glu_b.astype(F32))

    out_lat = glu(y_lat).astype(u_lat.dtype)
    out_ctx = glu(y_ctx).astype(u_ctx.dtype) if with_ctx else None
    return out_lat, out_ctx


def mixer_ab(hx, hc, w_in, w_out, short_w, short_b, f_w1, f_b1, f_w2, f_b2, f_w3, hy_bias,
             lam_re, lam_im, log_dt, b_re, b_im, c_re, c_im, s5_d, glu_w, glu_b, with_ctx):
    px = hx @ w_in
    pc = hc @ w_in
    hw3 = 3 * HY_WIDTH
    hy_args = (short_w, short_b, f_w1, f_b1, f_w2, f_b2, f_w3, hy_bias)
    ya_lat = hyena(px[..., :hw3], *hy_args)
    yb_lat, yb_ctx = s5_mixer(px[..., hw3:], pc[..., hw3:], lam_re, lam_im, log_dt, b_re, b_im,
                              c_re, c_im, s5_d, glu_w, glu_b, with_ctx)
    out_lat = jnp.concatenate([ya_lat, yb_lat], axis=-1) @ w_out
    out_ctx = None
    if with_ctx:
        ya_ctx = hyena(pc[..., :hw3], *hy_args)
        out_ctx = jnp.concatenate([ya_ctx, yb_ctx], axis=-1) @ w_out
    return out_lat, out_ctx


def ret_chunkwise(q, k, v, log_g, s0):
    bsz, L, H, dk = q.shape
    dv = v.shape[-1]
    C = RET_CHUNK
    n = L // C
    qc = q.reshape(bsz, n, C, H, dk)
    kc = k.reshape(bsz, n, C, H, dk)
    vc = v.reshape(bsz, n, C, H, dv)
    idx = jnp.arange(C, dtype=F32)
    diff = idx[:, None] - idx[None, :]
    mask = jnp.where(diff[None] >= 0, jnp.exp(jnp.maximum(diff, 0.0)[None] * log_g[:, None, None]), 0.0)
    scores = jnp.einsum('bnihd,bnjhd->bnhij', qc, kc) * mask
    inner = jnp.einsum('bnhij,bnjhe->bnihe', scores, vc)
    k_dec = kc * jnp.exp((C - 1 - idx)[:, None] * log_g[None])[None, None, :, :, None]
    chunk_kv = jnp.einsum('bnjhd,bnjhe->nbhde', k_dec, vc)
    g_chunk = jnp.exp(C * log_g)[None, :, None, None]

    def step(s, kv):
        return g_chunk * s + kv, s

    s_final, s_prev = lax.scan(step, s0, chunk_kv)
    q_dec = qc * jnp.exp((idx + 1.0)[:, None] * log_g[None])[None, None, :, :, None]
    cross = jnp.einsum('bnihd,nbhde->bnihe', q_dec, s_prev)
    return (inner + cross).reshape(bsz, L, H, dv), s_final


def ret_final_state(k, v, log_g):
    L = k.shape[1]
    w = jnp.exp((L - 1 - jnp.arange(L, dtype=F32))[:, None] * log_g[None])
    return jnp.einsum('blhd,blhe->bhde', k * w[None, :, :, None], v)


def retention(p_lat, p_ctx, decay_logit, with_ctx):
    qk = RET_HEADS * RET_DK
    vw = RET_HEADS * RET_DV

    def heads(p):
        bsz, L, _ = p.shape
        q = p[..., :qk].reshape(bsz, L, RET_HEADS, RET_DK)
        k = p[..., qk:2 * qk].reshape(bsz, L, RET_HEADS, RET_DK)
        v = p[..., 2 * qk:2 * qk + vw].reshape(bsz, L, RET_HEADS, RET_DV)
        g = p[..., 2 * qk + vw:]
        return q, k, v, g

    ql, kl, vl, gl = heads(p_lat)
    qc, kc, vc, gc = heads(p_ctx)
    bsz, L, _ = p_lat.shape
    pos = jnp.arange(L, dtype=F32)
    scale = RET_DK ** -0.5
    ql = rope_1d(ql, pos, ROPE_BASE).astype(F32) * scale
    kl = rope_1d(kl, pos, ROPE_BASE).astype(F32)
    vl = vl.astype(F32)
    qc = qc.astype(F32) * scale
    kc = kc.astype(F32)
    vc = vc.astype(F32)
    log_g = jax.nn.log_sigmoid(decay_logit.astype(F32))
    s_zero = jnp.zeros((bsz, RET_HEADS, RET_DK, RET_DV), F32)
    o_lat = jnp.zeros(vl.shape, F32)
    o_ctx = jnp.zeros(vc.shape, F32)
    for d in range(2):
        flip = (lambda a: a[:, ::-1]) if d == 1 else (lambda a: a)
        if with_ctx:
            oc, s_c = ret_chunkwise(flip(qc), flip(kc), flip(vc), log_g[d], s_zero)
            o_ctx = o_ctx + flip(oc)
        else:
            s_c = ret_final_state(flip(kc), flip(vc), log_g[d])
        ol, _ = ret_chunkwise(flip(ql), flip(kl), flip(vl), log_g[d], s_c)
        o_lat = o_lat + flip(ol)

    def readout(o, g):
        o = o * lax.rsqrt(jnp.mean(o * o, axis=-1, keepdims=True) + EPS)
        o = o.reshape(o.shape[0], o.shape[1], vw)
        return (o * jax.nn.silu(g.astype(F32))).astype(g.dtype)

    return readout(o_lat, gl), (readout(o_ctx, gc) if with_ctx else None)


def swa(p_lat, p_ctx, sink, with_ctx):
    qw = SWA_Q_HEADS * SWA_HD
    kw = SWA_KV_HEADS * SWA_HD
    G = SWA_Q_HEADS // SWA_KV_HEADS

    def heads(p):
        bsz, L, _ = p.shape
        q = p[..., :qw].reshape(bsz, L, SWA_Q_HEADS, SWA_HD)
        k = p[..., qw:qw + kw].reshape(bsz, L, SWA_KV_HEADS, SWA_HD)
        v = p[..., qw + kw:].reshape(bsz, L, SWA_KV_HEADS, SWA_HD)
        return q, k, v

    ql, kl, vl = heads(p_lat)
    qc, kc, vc = heads(p_ctx)
    bsz, L, _ = p_lat.shape
    Lc = p_ctx.shape[1]
    row, col = grid_positions(L)
    ql = rope_2d(ql, row, col).reshape(bsz, L, SWA_KV_HEADS, G, SWA_HD)
    kl = rope_2d(kl, row, col)
    qc = qc.reshape(bsz, Lc, SWA_KV_HEADS, G, SWA_HD)
    scale = SWA_HD ** -0.5
    sink_b = sink.astype(F32).reshape(SWA_KV_HEADS, G)

    bk = SWA_BLOCK
    nb = L // bk
    qb = ql.reshape(bsz, nb, bk, SWA_KV_HEADS, G, SWA_HD)

    def band(t):
        tp = jnp.pad(t, ((0, 0), (bk, bk), (0, 0), (0, 0))).reshape(bsz, nb + 2, bk, SWA_KV_HEADS, SWA_HD)
        return jnp.concatenate([tp[:, :-2], tp[:, 1:-1], tp[:, 2:]], axis=2)

    kb, vb = band(kl), band(vl)
    s_loc = jnp.einsum('bnikgd,bnjkd->bnkgij', qb, kb, preferred_element_type=F32) * scale
    s_ctx = jnp.einsum('bnikgd,bjkd->bnkgij', qb, kc, preferred_element_type=F32) * scale
    qi = jnp.arange(bk)[:, None] + bk
    kj = jnp.arange(3 * bk)[None, :]
    in_window = jnp.abs(kj - qi) <= SWA_WINDOW
    key_pos = (jnp.arange(nb)[:, None] - 1) * bk + jnp.arange(3 * bk)[None, :]
    in_range = (key_pos >= 0) & (key_pos < L)
    mask = in_window[None] & in_range[:, None, :]
    s_loc = jnp.where(mask[None, :, None, None], s_loc, NEG_INF)
    sink_col = jnp.broadcast_to(sink_b[None, None, :, :, None, None], s_loc.shape[:-1] + (1,))
    probs = jax.nn.softmax(jnp.concatenate([s_loc, s_ctx, sink_col], axis=-1), axis=-1)
    p_loc = probs[..., :3 * bk]
    p_cx = probs[..., 3 * bk:3 * bk + Lc]
    o = (jnp.einsum('bnkgij,bnjkd->bnikgd', p_loc, vb.astype(F32))
         + jnp.einsum('bnkgij,bjkd->bnikgd', p_cx, vc.astype(F32)))
    o_lat = o.reshape(bsz, L, qw).astype(p_lat.dtype)

    o_ctx = None
    if with_ctx:
        s_cc = jnp.einsum('bikgd,bjkd->bkgij', qc, kc, preferred_element_type=F32) * scale
        sink_c = jnp.broadcast_to(sink_b[None, :, :, None, None], s_cc.shape[:-1] + (1,))
        pc = jax.nn.softmax(jnp.concatenate([s_cc, sink_c], axis=-1), axis=-1)[..., :Lc]
        o_ctx = jnp.einsum('bkgij,bjkd->bikgd', pc, vc.astype(F32)).reshape(bsz, Lc, qw).astype(p_ctx.dtype)
    return o_lat, o_ctx


def mixer_cd(hx, hc, w_in, w_out, decay_logit, sink, with_ctx):
    px = hx @ w_in
    pc = hc @ w_in
    rl, rc = retention(px[..., :RET_WIDTH], pc[..., :RET_WIDTH], decay_logit, with_ctx)
    sl, sc = swa(px[..., RET_WIDTH:], pc[..., RET_WIDTH:], sink, with_ctx)
    out_lat = jnp.concatenate([rl, sl], axis=-1) @ w_out
    out_ctx = jnp.concatenate([rc, sc], axis=-1) @ w_out if with_ctx else None
    return out_lat, out_ctx


def moe(h2, router_w, router_b, w_gu, b_gu, w_dn, b_dn):
    N, dm = h2.shape
    logits = (h2 @ router_w).astype(F32) + router_b.astype(F32)
    top_v, top_i = lax.top_k(logits, TOP_K)
    gates = jax.nn.softmax(top_v, axis=-1).astype(h2.dtype)
    M = N * TOP_K
    flat_e = top_i.reshape(M)
    flat_t = jnp.repeat(jnp.arange(N, dtype=jnp.int32), TOP_K)
    flat_g = gates.reshape(M)
    order = jnp.argsort(flat_e)
    se, st, sg = flat_e[order], flat_t[order], flat_g[order]
    counts = jnp.bincount(flat_e, length=N_EXPERTS)
    padded = (counts + MOE_BLOCK - 1) // MOE_BLOCK * MOE_BLOCK
    pad_end = jnp.cumsum(padded)
    pad_start = pad_end - padded
    start = jnp.cumsum(counts) - counts
    dest = pad_start[se] + jnp.arange(M, dtype=jnp.int32) - start[se]
    n_blocks = -(-M // MOE_BLOCK) + N_EXPERTS
    n_slots = n_blocks * MOE_BLOCK
    slot_tok = jnp.full((n_slots,), N, jnp.int32).at[dest].set(st)
    slot_gate = jnp.zeros((n_slots,), h2.dtype).at[dest].set(sg)
    block_exp = jnp.minimum(jnp.searchsorted(pad_end, jnp.arange(n_blocks, dtype=jnp.int32) * MOE_BLOCK, side='right'),
                            N_EXPERTS - 1)
    h_pad = jnp.concatenate([h2, jnp.zeros((1, dm), h2.dtype)], axis=0)
    xb = h_pad[slot_tok].reshape(n_blocks, MOE_BLOCK, dm)

    def expert_block(args):
        xblk, e = args
        gu = xblk @ w_gu[e] + b_gu[e]
        gate = jnp.minimum(gu[:, :D_FF], SWIGLU_LIMIT)
        up = jnp.clip(gu[:, D_FF:], -SWIGLU_LIMIT, SWIGLU_LIMIT)
        act = gate * jax.nn.sigmoid(SWIGLU_ALPHA * gate) * (up + 1.0)
        return act @ w_dn[e] + b_dn[e]

    yb = lax.map(expert_block, (xb, block_exp))
    y = jnp.zeros((N + 1, dm), h2.dtype).at[slot_tok].add(yb.reshape(n_slots, dm) * slot_gate[:, None])
    return y[:N]


def setup_inputs(seed: int = 0) -> dict:
    key = jax.random.key(seed)
    keys = iter(jax.random.split(key, 48))
    D = D_MODEL
    ne = (DEPTH + 1) // 2
    no = DEPTH // 2
    G, P, H = S5_GROUPS, S5_STATE, S5_GROUP

    def normal(shape, scale):
        return jax.random.normal(next(keys), shape, F32) * scale

    inp = {}
    inp['x'] = normal((BATCH, SEQ, D), 1.0)
    inp['c'] = normal((BATCH, D), 1.0)
    inp['ctx'] = normal((BATCH, CTX_LEN, D), 1.0)
    inp['c_ctx'] = normal((D,), 1.0)
    inp['mod_w'] = normal((DEPTH, D, 6 * D), 0.5 * D ** -0.5)
    inp['mod_b'] = normal((DEPTH, 6 * D), 0.01)
    inp['norm1_g'] = 1.0 + normal((DEPTH, D), 0.01)
    inp['norm2_g'] = 1.0 + normal((DEPTH, D), 0.01)
    inp['ab_w_in'] = normal((ne, D, AB_IN), D ** -0.5)
    inp['ab_w_out'] = normal((ne, MIX_OUT, D), MIX_OUT ** -0.5)
    inp['hy_short_w'] = normal((ne, HY_SHORT, 3 * HY_WIDTH), HY_SHORT ** -0.5)
    inp['hy_short_b'] = normal((ne, 3 * HY_WIDTH), 0.01)
    inp['hy_f_w1'] = normal((ne, HY_EMB, HY_FILTER_HIDDEN), HY_EMB ** -0.5)
    inp['hy_f_b1'] = normal((ne, HY_FILTER_HIDDEN), 0.01)
    inp['hy_f_w2'] = normal((ne, HY_FILTER_HIDDEN, HY_FILTER_HIDDEN), HY_FILTER_HIDDEN ** -0.5)
    inp['hy_f_b2'] = normal((ne, HY_FILTER_HIDDEN), 0.01)
    inp['hy_f_w3'] = normal((ne, HY_FILTER_HIDDEN, HY_WIDTH), HY_FILTER_HIDDEN ** -0.5)
    inp['hy_bias'] = normal((ne, HY_WIDTH), 0.5)
    inp['s5_lambda_re'] = -0.5 + normal((ne, 2, G, P), 0.01)
    inp['s5_lambda_im'] = math.pi * jnp.arange(P, dtype=F32) + normal((ne, 2, G, P), 0.01)
    inp['s5_log_dt'] = jax.random.uniform(next(keys), (ne, 2, G), F32, math.log(1e-3), math.log(1e-1))
    inp['s5_b_re'] = normal((ne, 2, G, P, H), (2 * H) ** -0.5)
    inp['s5_b_im'] = normal((ne, 2, G, P, H), (2 * H) ** -0.5)
    inp['s5_c_re'] = normal((ne, 2, G, H, P), (2 * P) ** -0.5)
    inp['s5_c_im'] = normal((ne, 2, G, H, P), (2 * P) ** -0.5)
    inp['s5_d'] = normal((ne, S5_WIDTH), 1.0)
    inp['s5_glu_w'] = normal((ne, S5_WIDTH, S5_WIDTH), S5_WIDTH ** -0.5)
    inp['s5_glu_b'] = normal((ne, S5_WIDTH), 0.01)
    inp['cd_w_in'] = normal((no, D, CD_IN), D ** -0.5)
    inp['cd_w_out'] = normal((no, MIX_OUT, D), MIX_OUT ** -0.5)
    ret_logit = jnp.log(2.0 ** (5.0 + jnp.arange(RET_HEADS, dtype=F32)) - 1.0)
    inp['ret_decay_logit'] = ret_logit + normal((no, 2, RET_HEADS), 0.05)
    inp['swa_sink'] = normal((no, SWA_Q_HEADS), 1.0)
    inp['router_w'] = normal((DEPTH, D, N_EXPERTS), D ** -0.5)
    inp['router_b'] = normal((DEPTH, N_EXPERTS), 0.01)
    inp['exp_w_gu'] = normal((DEPTH, N_EXPERTS, D, 2 * D_FF), D ** -0.5)
    inp['exp_b_gu'] = normal((DEPTH, N_EXPERTS, 2 * D_FF), 0.01)
    inp['exp_w_down'] = normal((DEPTH, N_EXPERTS, D_FF, D), D_FF ** -0.5)
    inp['exp_b_down'] = normal((DEPTH, N_EXPERTS, D), 0.01)
    inp['final_g'] = 1.0 + normal((D,), 0.01)
    return inp


def reference(x, c, ctx, c_ctx, mod_w, mod_b, norm1_g, norm2_g, ab_w_in, ab_w_out, hy_short_w, hy_short_b,
              hy_f_w1, hy_f_b1, hy_f_w2, hy_f_b2, hy_f_w3, hy_bias, s5_lambda_re, s5_lambda_im, s5_log_dt,
              s5_b_re, s5_b_im, s5_c_re, s5_c_im, s5_d, s5_glu_w, s5_glu_b, cd_w_in, cd_w_out,
              ret_decay_logit, swa_sink, router_w, router_b, exp_w_gu, exp_b_gu, exp_w_down, exp_b_down, final_g):
    bsz, L, dm = x.shape
    Lc = ctx.shape[1]
    h_ctx = ctx
    for layer in range(DEPTH):
        with_ctx = layer < DEPTH - 1
        m_lat = (jax.nn.silu(c) @ mod_w[layer] + mod_b[layer])[:, None, :]
        m_ctx = (jax.nn.silu(c_ctx) @ mod_w[layer] + mod_b[layer])[None, None, :]
        sh1, sc1, g1, sh2, sc2, g2 = jnp.split(m_lat, 6, axis=-1)
        csh1, csc1, cg1, csh2, csc2, cg2 = jnp.split(m_ctx, 6, axis=-1)
        hx = rmsnorm(x, norm1_g[layer]) * (1.0 + sc1) + sh1
        hc = rmsnorm(h_ctx, norm1_g[layer]) * (1.0 + csc1) + csh1
        if layer % 2 == 0:
            i = layer // 2
            mx, mc = mixer_ab(hx, hc, ab_w_in[i], ab_w_out[i], hy_short_w[i], hy_short_b[i],
                              hy_f_w1[i], hy_f_b1[i], hy_f_w2[i], hy_f_b2[i], hy_f_w3[i], hy_bias[i],
                              s5_lambda_re[i], s5_lambda_im[i], s5_log_dt[i], s5_b_re[i], s5_b_im[i],
                              s5_c_re[i], s5_c_im[i], s5_d[i], s5_glu_w[i], s5_glu_b[i], with_ctx)
        else:
            i = layer // 2
            mx, mc = mixer_cd(hx, hc, cd_w_in[i], cd_w_out[i], ret_decay_logit[i], swa_sink[i], with_ctx)
        moe_args = (router_w[layer], router_b[layer], exp_w_gu[layer], exp_b_gu[layer],
                    exp_w_down[layer], exp_b_down[layer])
        x = x + g1 * mx
        hx = rmsnorm(x, norm2_g[layer]) * (1.0 + sc2) + sh2
        x = x + g2 * moe(hx.reshape(bsz * L, dm), *moe_args).reshape(bsz, L, dm)
        if with_ctx:
            h_ctx = h_ctx + cg1 * mc
            hc = rmsnorm(h_ctx, norm2_g[layer]) * (1.0 + csc2) + csh2
            h_ctx = h_ctx + cg2 * moe(hc.reshape(bsz * Lc, dm), *moe_args).reshape(bsz, Lc, dm)
    return rmsnorm(x, final_g)
```

```python
import functools
import math

import numpy as np
import jax
import jax.numpy as jnp
from jax import lax
from jax.experimental import pallas as pl
from jax.experimental.pallas import tpu as pltpu

F32 = jnp.float32
BF16 = jnp.bfloat16
HI = lax.Precision.HIGHEST

EPS = 1e-6
NEG_INF = -1e30
ROPE_BASE = 10000.0
GRID_W = 64

HY_SHORT = 3
HY_BANDS = 16
HY_SHIFT = 0.05
HY_FAST_DECAY = math.log(1e-2) / 0.3
HY_SLOW_DECAY = math.log(1e-2) / 1.5
S5_GROUP = 16
S5_STATE = 64
S5_CHUNK = 8
RET_HEADS = 4
RET_CHUNK = 128
SWA_Q_HEADS = 8
SWA_KV_HEADS = 2
SWA_WINDOW = 128
SWA_BLOCK = 128
N_EXPERTS = 32
TOP_K = 4
SWIGLU_LIMIT = 7.0
SWIGLU_ALPHA = 1.702

LANES = 128
SUBLANES = 8
VMEM_LIMIT = 52 * 2**20
FFT_B = 128
MOE_TM = 256


def _cparams(sem):
    return pltpu.CompilerParams(dimension_semantics=sem, vmem_limit_bytes=VMEM_LIMIT)


def _rows_call(body, rows, consts, batched, outs, tm, rows_per_batch, name, periodic=()):
    m = rows[0].shape[-2]
    assert m % tm == 0 and rows_per_batch % tm == 0
    tiles_per_batch = rows_per_batch // tm

    def row_spec(shape):
        if len(shape) == 2:
            return pl.BlockSpec((tm, shape[1]), lambda i: (i, 0))
        return pl.BlockSpec((shape[0], tm, shape[2]), lambda i: (0, i, 0))

    in_specs = [row_spec(a.shape) for a in rows]
    for a in periodic:
        in_specs.append(pl.BlockSpec((tm, a.shape[1]), lambda i: (i % tiles_per_batch, 0)))
    for a in consts:
        in_specs.append(pl.BlockSpec(a.shape, lambda i, n=a.ndim: (0,) * n))
    for a in batched:
        in_specs.append(pl.BlockSpec((None, 1, a.shape[2]), lambda i: (i // tiles_per_batch, 0, 0)))
    out_specs = [row_spec(s) for s, _ in outs]
    out_shape = [jax.ShapeDtypeStruct(s, d) for s, d in outs]
    res = pl.pallas_call(
        body,
        grid=(m // tm,),
        in_specs=in_specs,
        out_specs=out_specs,
        out_shape=out_shape,
        compiler_params=_cparams(("parallel",)),
        name=name,
    )(*rows, *periodic, *consts, *batched)
    return res


def _norm_mod(x, g, sc, sh):
    y = x * lax.rsqrt(jnp.mean(x * x, axis=-1, keepdims=True) + EPS) * g
    return y * (1.0 + sc) + sh


def _norm_mod_matmul_body(x_ref, g_ref, w_ref, sc_ref, sh_ref, *o_refs, splits):
    h = _norm_mod(x_ref[...], g_ref[...], sc_ref[...], sh_ref[...])
    r = jnp.dot(h.astype(BF16), w_ref[...], preferred_element_type=F32)
    off = 0
    for o_ref, n in zip(o_refs, splits):
        if o_ref.ndim == 3:
            hd = o_ref.shape[2]
            for h_i in range(o_ref.shape[0]):
                o_ref[h_i] = r[:, off + h_i * hd: off + (h_i + 1) * hd]
        else:
            o_ref[...] = r[:, off:off + n]
        off += n


def _norm_mod_matmul(x2, g, sc, sh, w, splits, tm, rows_per_batch, name):
    m = x2.shape[0]
    outs = []
    for n, heads in splits:
        outs.append(((heads, m, n // heads), F32) if heads else ((m, n), F32))
    splits = [n for n, _ in splits]
    body = functools.partial(_norm_mod_matmul_body, splits=splits)
    return _rows_call(body, [x2], [g.reshape(1, -1), w.astype(BF16)], [sc, sh], outs, tm, rows_per_batch, name)


def _out_proj_body(*refs, n_in):
    a_refs = refs[:n_in]
    x_ref = refs[n_in]
    w_refs = refs[n_in + 1: 2 * n_in + 1]
    g_ref = refs[2 * n_in + 1]
    o_ref = refs[2 * n_in + 2]
    acc = None
    for a_ref, w_ref in zip(a_refs, w_refs):
        if a_ref.ndim == 3:
            for h_i in range(a_ref.shape[0]):
                t = jnp.dot(a_ref[h_i].astype(BF16), w_ref[h_i], preferred_element_type=F32)
                acc = t if acc is None else acc + t
        else:
            t = jnp.dot(a_ref[...].astype(BF16), w_ref[...], preferred_element_type=F32)
            acc = t if acc is None else acc + t
    o_ref[...] = x_ref[...] + g_ref[...] * acc


def _out_proj(parts, ws, x2, gate, tm, rows_per_batch, name):
    m, d = x2.shape
    body = functools.partial(_out_proj_body, n_in=len(parts))
    ws = [w.astype(BF16) for w in ws]
    return _rows_call(body, list(parts) + [x2], ws, [gate], [((m, d), F32)], tm, rows_per_batch, name)[0]


def _final_norm_body(x_ref, g_ref, o_ref):
    x = x_ref[...]
    o_ref[...] = x * lax.rsqrt(jnp.mean(x * x, axis=-1, keepdims=True) + EPS) * g_ref[...]


def _hy_prep_body(u_ref, p_ref, n_ref, w_ref, b_ref, x0_ref, z_ref, *, width):
    i = pl.program_id(1)
    last = pl.num_programs(1) - 1
    u = u_ref[...]
    tt = u.shape[0]
    prev_row = jnp.where(i == 0, 0.0, p_ref[SUBLANES - 1:SUBLANES, :])
    next_row = jnp.where(i == last, 0.0, n_ref[0:1, :])
    rows = lax.broadcasted_iota(jnp.int32, (tt, 1), 0)
    up = jnp.where(rows == 0, prev_row, pltpu.roll(u, 1, axis=0))
    dn = jnp.where(rows == tt - 1, next_row, pltpu.roll(u, tt - 1, axis=0))
    y = w_ref[0:1, :] * up + w_ref[1:2, :] * u + w_ref[2:3, :] * dn + b_ref[...]
    x0_ref[...] = y[:, :width]
    z_ref[...] = y[:, 2 * width:] * y[:, width:2 * width]


def _hyena_prep(p, short_w, short_b, bsz, seq, tt):
    w3 = p.shape[1]
    width = w3 // 3
    p3 = p.reshape(bsz, seq, w3)
    nt = seq // tt
    sub = tt // SUBLANES
    nsub = seq // SUBLANES
    body = functools.partial(_hy_prep_body, width=width)
    x0, z = pl.pallas_call(
        body,
        grid=(bsz, nt),
        in_specs=[
            pl.BlockSpec((None, tt, w3), lambda b, i: (b, i, 0)),
            pl.BlockSpec((None, SUBLANES, w3), lambda b, i: (b, jnp.maximum(i * sub - 1, 0), 0)),
            pl.BlockSpec((None, SUBLANES, w3), lambda b, i: (b, jnp.minimum((i + 1) * sub, nsub - 1), 0)),
            pl.BlockSpec((HY_SHORT, w3), lambda b, i: (0, 0)),
            pl.BlockSpec((1, w3), lambda b, i: (0, 0)),
        ],
        out_specs=[pl.BlockSpec((None, tt, width), lambda b, i: (b, i, 0))] * 2,
        out_shape=[jax.ShapeDtypeStruct((bsz, seq, width), F32)] * 2,
        compiler_params=_cparams(("parallel", "parallel")),
        name="hyena_prep",
    )(p3, p3, p3, short_w, short_b.reshape(1, -1))
    return x0, z


def _hy_filter_body(t_ref, w_ref, lag_ref, bands_ref, deltas_ref, w1t_ref, w1c_ref, w1s_ref, b1_ref,
                    w2_ref, b2_ref, w3_ref, h_ref, s_ref):
    i = pl.program_id(0)
    arg = w_ref[...] * bands_ref[...]
    pre = (t_ref[...] * w1t_ref[...]
           + jnp.dot(jnp.cos(arg), w1c_ref[...], precision=HI, preferred_element_type=F32)
           + jnp.dot(-jnp.sin(arg), w1s_ref[...], precision=HI, preferred_element_type=F32)
           + b1_ref[...])
    h1 = jnp.sin(pre)
    h2 = jnp.sin(jnp.dot(h1, w2_ref[...], precision=HI, preferred_element_type=F32) + b2_ref[...])
    h3 = jnp.dot(h2, w3_ref[...], precision=HI, preferred_element_type=F32)
    h = h3 * (jnp.exp(-lag_ref[...] * deltas_ref[...]) + HY_SHIFT)
    h_ref[...] = h

    @pl.when(i == 0)
    def _():
        s_ref[...] = jnp.zeros_like(s_ref)

    s_ref[...] += jnp.sum(jnp.abs(h), axis=0, keepdims=True)


def _hyena_filter(seq, w1, b1, w2, b2, w3):
    width = w3.shape[1]
    pos = jnp.arange(seq, dtype=F32)
    t = (pos / seq)[:, None]
    w = (2.0 * math.pi * pos / seq)[:, None]
    lag = (jnp.abs(pos - seq // 2) / (seq / 2))[:, None]
    bands = jnp.linspace(1e-4, HY_BANDS - 1, HY_BANDS, dtype=F32)[None]
    deltas = jnp.abs(jnp.linspace(HY_FAST_DECAY, HY_SLOW_DECAY, width, dtype=F32))[None]
    tl = min(seq, 1024)
    col = pl.BlockSpec((tl, 1), lambda i: (i, 0))

    def whole(a):
        return pl.BlockSpec(a.shape, lambda i, n=a.ndim: (0,) * n)

    consts = [bands, deltas, w1[0:1], w1[1:1 + HY_BANDS], w1[1 + HY_BANDS:], b1.reshape(1, -1),
              w2, b2.reshape(1, -1), w3]
    h, s = pl.pallas_call(
        _hy_filter_body,
        grid=(seq // tl,),
        in_specs=[col, col, col] + [whole(a) for a in consts],
        out_specs=[pl.BlockSpec((tl, width), lambda i: (i, 0)), pl.BlockSpec((1, width), lambda i: (0, 0))],
        out_shape=[jax.ShapeDtypeStruct((seq, width), F32), jax.ShapeDtypeStruct((1, width), F32)],
        compiler_params=_cparams(("arbitrary",)),
        name="hyena_filter",
    )(t, w, lag, *consts)
    return h, s


def _dft_tables(na):
    n = na * FFT_B
    a = np.arange(na)
    b = np.arange(FFT_B)
    ang1 = 2.0 * np.pi * np.outer(a, a) / na
    ang2 = 2.0 * np.pi * np.outer(b, b) / FFT_B
    angt = 2.0 * np.pi * np.outer(b, a) / n
    c2, s2 = np.cos(ang2), np.sin(ang2)
    tabs = dict(
        f1=np.concatenate([np.cos(ang1), -np.sin(ang1)], axis=0),
        f2=np.block([[c2, s2], [-s2, c2]]),
        f2i=np.block([[c2, -s2], [s2, c2]]),
        f3=np.concatenate([np.cos(ang1), -np.sin(ang1)], axis=1) / n,
        twc_b=np.cos(angt)[:, :, None], tws_b=np.sin(angt)[:, :, None],
        twc_c=np.cos(angt).T[:, :, None], tws_c=np.sin(angt).T[:, :, None],
    )
    return {k: jnp.asarray(v, F32) for k, v in tabs.items()}


FFT_G = SUBLANES


def _fft1_body(x_ref, f_ref, tc_ref, ts_ref, sc_ref, o_ref, *, na):
    f = f_ref[...].astype(BF16)
    for j in range(FFT_G):
        x = (x_ref[:, j, :] * sc_ref[...]).astype(BF16)
        g = jnp.dot(f, x, preferred_element_type=F32)
        gr, gi = g[:na], g[na:]
        tc, ts = tc_ref[j], ts_ref[j]
        o_ref[:, j, :] = jnp.concatenate([gr * tc + gi * ts, gi * tc - gr * ts], axis=0)


def _fft_stage1(x4, scale, tabs, na):
    bz, ka, _, ch = x4.shape
    return pl.pallas_call(
        functools.partial(_fft1_body, na=na),
        grid=(bz, FFT_B // FFT_G),
        in_specs=[
            pl.BlockSpec((None, ka, FFT_G, ch), lambda z, b: (z, 0, b, 0)),
            pl.BlockSpec((2 * na, ka), lambda z, b: (0, 0)),
            pl.BlockSpec((FFT_G, na, 1), lambda z, b: (b, 0, 0)),
            pl.BlockSpec((FFT_G, na, 1), lambda z, b: (b, 0, 0)),
            pl.BlockSpec((1, ch), lambda z, b: (0, 0)),
        ],
        out_specs=pl.BlockSpec((None, 2 * na, FFT_G, ch), lambda z, b: (z, 0, b, 0)),
        out_shape=jax.ShapeDtypeStruct((bz, 2 * na, FFT_B, ch), F32),
        compiler_params=_cparams(("parallel", "parallel")),
        name="fft_stage1",
    )(x4, tabs["f1"][:, :ka], tabs["twc_b"], tabs["tws_b"], scale)


def _fft2_spec_body(ar_ref, ai_ref, f_ref, o_ref):
    f = f_ref[...].astype(BF16)
    for j in range(FFT_G):
        v = jnp.concatenate([ar_ref[j], ai_ref[j]], axis=0).astype(BF16)
        o_ref[j] = jnp.dot(f, v, preferred_element_type=F32)


def _fft2_conv_body(ar_ref, ai_ref, h_ref, f_ref, fi_ref, tc_ref, ts_ref, o_ref):
    f = f_ref[...].astype(BF16)
    fi = fi_ref[...].astype(BF16)
    for j in range(FFT_G):
        v = jnp.concatenate([ar_ref[j], ai_ref[j]], axis=0).astype(BF16)
        x = jnp.dot(f, v, preferred_element_type=F32)
        xr, xi = x[:FFT_B], x[FFT_B:]
        hr, hi = h_ref[j, :FFT_B, :], h_ref[j, FFT_B:, :]
        p = jnp.concatenate([xr * hr - xi * hi, xr * hi + xi * hr], axis=0).astype(BF16)
        q = jnp.dot(fi, p, preferred_element_type=F32)
        qr, qi = q[:FFT_B], q[FFT_B:]
        tc, ts = tc_ref[j], ts_ref[j]
        o_ref[:, j, :] = jnp.concatenate([qr * tc - qi * ts, qi * tc + qr * ts], axis=0)


def _fft_stage2_spectrum(a4, tabs, na):
    ch = a4.shape[-1]
    ng = na // FFT_G
    return pl.pallas_call(
        _fft2_spec_body,
        grid=(ng,),
        in_specs=[
            pl.BlockSpec((None, FFT_G, FFT_B, ch), lambda c: (0, c, 0, 0)),
            pl.BlockSpec((None, FFT_G, FFT_B, ch), lambda c: (0, ng + c, 0, 0)),
            pl.BlockSpec((2 * FFT_B, 2 * FFT_B), lambda c: (0, 0)),
        ],
        out_specs=pl.BlockSpec((FFT_G, 2 * FFT_B, ch), lambda c: (c, 0, 0)),
        out_shape=jax.ShapeDtypeStruct((na, 2 * FFT_B, ch), F32),
        compiler_params=_cparams(("parallel",)),
        name="fft_stage2_spectrum",
    )(a4, a4, tabs["f2"])


def _fft_stage2_conv(a4, hspec, tabs, na):
    bz, _, _, ch = a4.shape
    ng = na // FFT_G
    return pl.pallas_call(
        _fft2_conv_body,
        grid=(bz, ng),
        in_specs=[
            pl.BlockSpec((None, FFT_G, FFT_B, ch), lambda z, c: (z, c, 0, 0)),
            pl.BlockSpec((None, FFT_G, FFT_B, ch), lambda z, c: (z, ng + c, 0, 0)),
            pl.BlockSpec((FFT_G, 2 * FFT_B, ch), lambda z, c: (c, 0, 0)),
            pl.BlockSpec((2 * FFT_B, 2 * FFT_B), lambda z, c: (0, 0)),
            pl.BlockSpec((2 * FFT_B, 2 * FFT_B), lambda z, c: (0, 0)),
            pl.BlockSpec((FFT_G, FFT_B, 1), lambda z, c: (c, 0, 0)),
            pl.BlockSpec((FFT_G, FFT_B, 1), lambda z, c: (c, 0, 0)),
        ],
        out_specs=pl.BlockSpec((None, 2 * FFT_B, FFT_G, ch), lambda z, c: (z, 0, c, 0)),
        out_shape=jax.ShapeDtypeStruct((bz, 2 * FFT_B, na, ch), F32),
        compiler_params=_cparams(("parallel", "parallel")),
        name="fft_stage2_conv",
    )(a4, a4, hspec, tabs["f2"], tabs["f2i"], tabs["twc_c"], tabs["tws_c"])


def _fft3_body(br_ref, bi_ref, f_ref, o_ref):
    f = f_ref[...].astype(BF16)
    for j in range(FFT_G):
        v = jnp.concatenate([br_ref[j], bi_ref[j]], axis=0).astype(BF16)
        o_ref[:, j, :] = jnp.dot(f, v, preferred_element_type=F32)


def _fft_stage3(b4, tabs, na, a_lo, a_cnt):
    bz, _, _, ch = b4.shape
    ng = FFT_B // FFT_G
    return pl.pallas_call(
        _fft3_body,
        grid=(bz, ng),
        in_specs=[
            pl.BlockSpec((None, FFT_G, na, ch), lambda z, b: (z, b, 0, 0)),
            pl.BlockSpec((None, FFT_G, na, ch), lambda z, b: (z, ng + b, 0, 0)),
            pl.BlockSpec((a_cnt, 2 * na), lambda z, b: (0, 0)),
        ],
        out_specs=pl.BlockSpec((None, a_cnt, FFT_G, ch), lambda z, b: (z, 0, b, 0)),
        out_shape=jax.ShapeDtypeStruct((bz, a_cnt, FFT_B, ch), F32),
        compiler_params=_cparams(("parallel", "parallel")),
        name="fft_stage3",
    )(b4, b4, tabs["f3"][a_lo:a_lo + a_cnt])


def _hy_gate_body(x0_ref, y_ref, z_ref, b_ref, o_ref):
    o_ref[...] = x0_ref[...] * (y_ref[...] + b_ref[...] * z_ref[...])


def _hyena(p, bsz, seq, short_w, short_b, w1, b1, w2, b2, w3, hy_bias, tt):
    width = w3.shape[1]
    x0, z = _hyena_prep(p, short_w, short_b, bsz, seq, tt)
    hu, hs = _hyena_filter(seq, w1, b1, w2, b2, w3)
    na = max(2 * seq // FFT_B, 16)
    ka = max(seq // FFT_B, 16)
    tabs = _dft_tables(na)
    pad = ka * FFT_B - seq

    def rows4(a, lead):
        if pad:
            a = jnp.pad(a, ((0, 0), (0, pad), (0, 0)))
        return a.reshape(lead, ka, FFT_B, width)

    ones = jnp.ones((1, width), F32)
    hspec = _fft_stage2_spectrum(_fft_stage1(rows4(hu[None], 1), 1.0 / hs, tabs, na), tabs, na)
    a4 = _fft_stage1(rows4(z, bsz), ones, tabs, na)
    b4 = _fft_stage2_conv(a4, hspec, tabs, na)
    if pad:
        y = _fft_stage3(b4, tabs, na, 0, na).reshape(bsz, na * FFT_B, width)[:, seq // 2: seq // 2 + seq]
    else:
        y = _fft_stage3(b4, tabs, na, seq // 2 // FFT_B, ka).reshape(bsz, seq, width)
    m = bsz * seq
    return _rows_call(_hy_gate_body, [x0.reshape(m, width), y.reshape(m, width), z.reshape(m, width)],
                      [hy_bias.reshape(1, -1)], [], [((m, width), F32)], tt, seq, "hyena_gate")[0]


def _s5_tables(lam_re, lam_im, log_dt, b_re, b_im, c_re, c_im, nsteps):
    t_len, hdim = S5_CHUNK, S5_GROUP
    lam = lax.complex(jnp.minimum(lam_re.astype(F32), -1e-4), lam_im.astype(F32))
    dt = jnp.exp(log_dt.astype(F32))[..., None]
    lam_dt = lam * dt
    lam_bar = jnp.exp(lam_dt)
    b_bar = ((lam_bar - 1.0) / lam)[..., None] * lax.complex(b_re.astype(F32), b_im.astype(F32))
    cm = lax.complex(c_re.astype(F32), c_im.astype(F32))
    ks = jnp.arange(t_len + 1, dtype=F32)
    pw = jnp.exp(ks[:, None, None, None] * lam_dt[None])
    g = lam.shape[1]
    tabs = {}
    pin = [pw[:t_len, 0][::-1], pw[:t_len, 1]]
    pout = [pw[1:, 0], pw[1:, 1][::-1]]
    for d in range(2):
        win = pin[d][:, :, :, None] * b_bar[d][None]
        win = jnp.transpose(win, (1, 0, 3, 2)).reshape(g, t_len * hdim, -1)
        tabs[f"win{d}"] = jnp.concatenate([jnp.real(win), jnp.imag(win)], axis=-1)
        wout = pout[d][:, :, None, :] * cm[d][None]
        wout = jnp.transpose(wout, (1, 3, 0, 2)).reshape(g, -1, t_len * hdim)
        tabs[f"wout{d}"] = jnp.concatenate([jnp.real(wout), -jnp.imag(wout)], axis=1)
        mu = jnp.exp((t_len * 2.0 ** jnp.arange(nsteps, dtype=F32))[:, None, None] * lam_dt[d][None])
        mr, mi = jnp.real(mu), jnp.imag(mu)
        tabs[f"m1{d}"] = jnp.concatenate([mr, mr], axis=-1)[:, :, None, :]
        tabs[f"m2{d}"] = jnp.concatenate([-mi, mi], axis=-1)[:, :, None, :]
        mu1 = jnp.exp(t_len * lam_dt[d])
        tabs[f"mu1{d}"] = jnp.concatenate([jnp.real(mu1), jnp.real(mu1)], axis=-1)[:, None, :]
        tabs[f"mu2{d}"] = jnp.concatenate([-jnp.imag(mu1), jnp.imag(mu1)], axis=-1)[:, None, :]
    kern = [jnp.real(jnp.einsum("ghp,tgp,gpk->tghk", cm[d], pw[:t_len, d], b_bar[d], precision=HI)) for d in range(2)]
    s_idx = jnp.arange(t_len)[:, None]
    t_idx = jnp.arange(t_len)[None, :]
    fwd = jnp.where((t_idx >= s_idx)[:, :, None, None, None], kern[0][jnp.maximum(t_idx - s_idx, 0)], 0.0)
    bwd = jnp.where((s_idx >= t_idx)[:, :, None, None, None], kern[1][jnp.maximum(s_idx - t_idx, 0)], 0.0)
    d0 = fwd + bwd
    tabs["d0"] = jnp.transpose(d0, (2, 0, 4, 1, 3)).reshape(g, t_len * hdim, t_len * hdim)
    return tabs


def _s5_body(u_ref, d0_ref, win0_ref, win1_ref, wout0_ref, wout1_ref, m10_ref, m20_ref, m11_ref, m21_ref,
             mu10_ref, mu20_ref, mu11_ref, mu21_ref, s0_ref, s1_ref, y_ref, f0_ref, f1_ref, *, bsz, nc, nsteps):
    rows = bsz * nc
    half = S5_STATE
    u = u_ref[...].astype(BF16)
    ridx = lax.broadcasted_iota(jnp.int32, (rows, 1), 0)
    j = ridx % nc

    def swap(a):
        return pltpu.roll(a, half, axis=1)

    def rows_of(init_ref):
        r = jnp.broadcast_to(init_ref[0:1, :], (rows, 2 * half))
        for b in range(1, bsz):
            r = jnp.where(ridx >= b * nc, init_ref[b:b + 1, :], r)
        return r

    e0 = jnp.dot(u, win0_ref[...].astype(BF16), preferred_element_type=F32)
    e1 = jnp.dot(u, win1_ref[...].astype(BF16), preferred_element_type=F32)
    s_f = jnp.where(j == 0, rows_of(s0_ref), pltpu.roll(e0, 1, axis=0))
    s_b = jnp.where(j == nc - 1, rows_of(s1_ref), pltpu.roll(e1, rows - 1, axis=0))
    for k in range(nsteps):
        step = 2 ** k
        sh = jnp.where(j >= step, pltpu.roll(s_f, step, axis=0), 0.0)
        s_f = s_f + m10_ref[k] * sh + m20_ref[k] * swap(sh)
        sh = jnp.where(j < nc - step, pltpu.roll(s_b, rows - step, axis=0), 0.0)
        s_b = s_b + m11_ref[k] * sh + m21_ref[k] * swap(sh)
    y = jnp.dot(u, d0_ref[...].astype(BF16), preferred_element_type=F32)
    y = y + jnp.dot(s_f.astype(BF16), wout0_ref[...].astype(BF16), preferred_element_type=F32)
    y = y + jnp.dot(s_b.astype(BF16), wout1_ref[...].astype(BF16), preferred_element_type=F32)
    y_ref[...] = y
    t_f = mu10_ref[...] * s_f + mu20_ref[...] * swap(s_f) + e0
    t_b = mu11_ref[...] * s_b + mu21_ref[...] * swap(s_b) + e1
    for b in range(bsz):
        f0_ref[b:b + 1, :] = t_f[(b + 1) * nc - 1:(b + 1) * nc, :]
        f1_ref[b:b + 1, :] = t_b[b * nc:b * nc + 1, :]


def _s5_core(u2, bsz, seq, tabs, init0, init1):
    width = u2.shape[1]
    g = width // S5_GROUP
    t_len = S5_CHUNK
    nc = seq // t_len
    rows = bsz * nc
    cols = t_len * S5_GROUP
    nsteps = max(1, math.ceil(math.log2(nc)))
    assert nsteps <= tabs["m10"].shape[0]
    ug = u2.reshape(bsz, nc, t_len, g, S5_GROUP).transpose(3, 0, 1, 2, 4).reshape(g, rows, cols)

    def per_group(shape):
        return pl.BlockSpec((None,) + shape, lambda i: (i,) + (0,) * len(shape))

    def per_group_steps(width2):
        return pl.BlockSpec((nsteps, None, 1, width2), lambda i: (0, i, 0, 0))

    p2 = 2 * S5_STATE
    in_specs = [per_group((rows, cols)), per_group((cols, cols)), per_group((cols, p2)), per_group((cols, p2)),
                per_group((p2, cols)), per_group((p2, cols))] + [per_group_steps(p2)] * 4 \
        + [per_group((1, p2))] * 4 + [per_group((bsz, p2))] * 2
    yg, f0, f1 = pl.pallas_call(
        functools.partial(_s5_body, bsz=bsz, nc=nc, nsteps=nsteps),
        grid=(g,),
        in_specs=in_specs,
        out_specs=[per_group((rows, cols)), per_group((bsz, p2)), per_group((bsz, p2))],
        out_shape=[jax.ShapeDtypeStruct((g, rows, cols), F32), jax.ShapeDtypeStruct((g, bsz, p2), F32),
                   jax.ShapeDtypeStruct((g, bsz, p2), F32)],
        compiler_params=_cparams(("parallel",)),
        name="s5_scan",
    )(ug, tabs["d0"], tabs["win0"], tabs["win1"], tabs["wout0"], tabs["wout1"],
      tabs["m10"][:nsteps], tabs["m20"][:nsteps], tabs["m11"][:nsteps], tabs["m21"][:nsteps],
      tabs["mu10"], tabs["mu20"], tabs["mu11"], tabs["mu21"], init0, init1)
    y = yg.reshape(g, bsz, nc, t_len, S5_GROUP).transpose(1, 2, 3, 0, 4).reshape(bsz * seq, width)
    return y, f0, f1


def _s5_glu_body(u_ref, y_ref, d_ref, w_ref, b_ref, o_ref):
    y = d_ref[...] * u_ref[...] + y_ref[...]
    g = jax.nn.gelu(y)
    o_ref[...] = g * jax.nn.sigmoid(jnp.dot(g.astype(BF16), w_ref[...], preferred_element_type=F32) + b_ref[...])


def _s5_glu(u2, y2, d_skip, glu_w, glu_b, tm, name):
    m, width = u2.shape
    return _rows_call(_s5_glu_body, [u2, y2], [d_skip.reshape(1, -1), glu_w.astype(BF16), glu_b.reshape(1, -1)], [],
                      [((m, width), F32)], tm, m, name)[0]


def _s5_mixer(u_lat, u_ctx, bsz, seq, seq_c, lam_re, lam_im, log_dt, b_re, b_im, c_re, c_im, d_skip, glu_w, glu_b,
              tm, tm_c):
    nsteps = max(1, math.ceil(math.log2(seq // S5_CHUNK)))
    tabs = _s5_tables(lam_re, lam_im, log_dt, b_re, b_im, c_re, c_im, nsteps)
    g = u_lat.shape[1] // S5_GROUP
    zero = jnp.zeros((g, bsz, 2 * S5_STATE), F32)
    y_ctx, f0, f1 = _s5_core(u_ctx, bsz, seq_c, tabs, zero, zero)
    y_lat, _, _ = _s5_core(u_lat, bsz, seq, tabs, f0, f1)
    out_lat = _s5_glu(u_lat, y_lat, d_skip, glu_w, glu_b, tm, "s5_glu")
    out_ctx = _s5_glu(u_ctx, y_ctx, d_skip, glu_w, glu_b, tm_c, "s5_glu_ctx")
    return out_lat, out_ctx


def _rope_body(*refs, shift, heads, scale, rope):
    if rope:
        x_ref, cos_ref, sin_ref, o_ref = refs
    else:
        x_ref, o_ref = refs
    x = x_ref[...]
    w = x.shape[1]
    if rope:
        lane = lax.broadcasted_iota(jnp.int32, x.shape, 1)
        partner = jnp.where(lane % (2 * shift) < shift, pltpu.roll(x, w - shift, axis=1), pltpu.roll(x, shift, axis=1))
        x = x * cos_ref[...] + partner * sin_ref[...]
    if scale != 1.0:
        x = x * scale
    hd = w // heads
    for h in range(heads):
        o_ref[h] = x[:, h * hd:(h + 1) * hd]


def _rope_heads(x2, tables, shift, heads, scale, tm, rows_per_batch, name):
    m, w = x2.shape
    body = functools.partial(_rope_body, shift=shift, heads=heads, scale=scale, rope=tables is not None)
    return _rows_call(body, [x2], [], [], [((heads, m, w // heads), F32)], tm, rows_per_batch, name,
                      periodic=list(tables) if tables is not None else [])[0]


def _rope_tables_1d(seq, hd, heads):
    half = hd // 2
    inv = ROPE_BASE ** (-jnp.arange(half, dtype=F32) / half)
    ang = jnp.arange(seq, dtype=F32)[:, None] * inv[None]
    cos, sin = jnp.cos(ang), jnp.sin(ang)
    return jnp.tile(jnp.concatenate([cos, cos], -1), (1, heads)), jnp.tile(jnp.concatenate([-sin, sin], -1), (1, heads))


def _rope_tables_2d(seq, hd, heads):
    q = hd // 4
    inv = ROPE_BASE ** (-jnp.arange(q, dtype=F32) / q)
    n_rows = seq // GRID_W
    row = jnp.repeat(jnp.arange(n_rows, dtype=F32), GRID_W)
    col = jnp.tile(jnp.arange(GRID_W, dtype=F32), n_rows)
    ar, ac = row[:, None] * inv[None], col[:, None] * inv[None]
    cos = jnp.concatenate([jnp.cos(ar), jnp.cos(ar), jnp.cos(ac), jnp.cos(ac)], -1)
    sin = jnp.concatenate([-jnp.sin(ar), jnp.sin(ar), -jnp.sin(ac), jnp.sin(ac)], -1)
    return jnp.tile(cos, (1, heads)), jnp.tile(sin, (1, heads))


def _kv_update(k, kdec, v):
    kd = (k * kdec).astype(BF16)
    return lax.dot_general(kd, v.astype(BF16), (((0,), (0,)), ((), ())), preferred_element_type=F32)


def _ret_bwd_body(k_ref, v_ref, kdec_ref, gc_ref, init_ref, sprev_ref, fin_ref, s_ref):
    @pl.when(pl.program_id(1) == 0)
    def _():
        s_ref[...] = init_ref[...]

    for h in range(k_ref.shape[0]):
        s = s_ref[h]
        sprev_ref[h] = s
        s = gc_ref[h] * s + _kv_update(k_ref[h], kdec_ref[h], v_ref[h])
        s_ref[h] = s
        fin_ref[h] = s


def _ret_main_body(q_ref, k_ref, v_ref, g_ref, mask_ref, qdf_ref, kdf_ref, qdb_ref, gc_ref, init_ref, sb_ref,
                   o_ref, fin_ref, s_ref):
    @pl.when(pl.program_id(1) == 0)
    def _():
        s_ref[...] = init_ref[...]

    for h in range(q_ref.shape[0]):
        q, k, v = q_ref[h], k_ref[h], v_ref[h]
        vb = v.astype(BF16)
        sc = lax.dot_general(q.astype(BF16), k.astype(BF16), (((1,), (1,)), ((), ())), preferred_element_type=F32)
        o = jnp.dot((sc * mask_ref[h]).astype(BF16), vb, preferred_element_type=F32)
        s = s_ref[h]
        o = o + jnp.dot((q * qdf_ref[h]).astype(BF16), s.astype(BF16), preferred_element_type=F32)
        o = o + jnp.dot((q * qdb_ref[h]).astype(BF16), sb_ref[h].astype(BF16), preferred_element_type=F32)
        o = o * lax.rsqrt(jnp.mean(o * o, axis=-1, keepdims=True) + EPS)
        o_ref[h] = o * jax.nn.silu(g_ref[h])
        s = gc_ref[h] * s + _kv_update(k, kdf_ref[h], v)
        s_ref[h] = s
        fin_ref[h] = s


def _ret_tables(decay_logit):
    cl = RET_CHUNK
    log_g = jax.nn.log_sigmoid(decay_logit.astype(F32))
    idx = jnp.arange(cl, dtype=F32)
    diff = idx[:, None] - idx[None, :]
    mf = jnp.where(diff[None] >= 0, jnp.exp(jnp.maximum(diff, 0.0)[None] * log_g[0][:, None, None]), 0.0)
    mb = jnp.where(diff[None] <= 0, jnp.exp(jnp.maximum(-diff, 0.0)[None] * log_g[1][:, None, None]), 0.0)

    def col(e, d):
        return jnp.exp(e[None, :] * log_g[d][:, None])[:, :, None]

    return dict(mask=mf + mb, qdf=col(idx + 1.0, 0), kdf=col(cl - 1.0 - idx, 0), qdb=col(cl - idx, 1), kdb=col(idx, 1),
                gcf=jnp.exp(cl * log_g[0])[:, None, None], gcb=jnp.exp(cl * log_g[1])[:, None, None])


def _retention(q, k, v, g, bsz, seq, tabs, init_f, init_b):
    heads, _, dk = q.shape
    dv = v.shape[2]
    cl = RET_CHUNK
    n = seq // cl

    def whole(a):
        return pl.BlockSpec(a.shape, lambda b, i, nd=a.ndim: (0,) * nd)

    state_spec = pl.BlockSpec((None, heads, dk, dv), lambda b, i: (b, 0, 0, 0))
    state_shape = jax.ShapeDtypeStruct((bsz, heads, dk, dv), F32)
    rev = lambda b, i: (0, b * n + (n - 1 - i), 0)
    sprev_b, fin_b = pl.pallas_call(
        _ret_bwd_body,
        grid=(bsz, n),
        in_specs=[pl.BlockSpec((heads, cl, dk), rev), pl.BlockSpec((heads, cl, dv), rev),
                  whole(tabs["kdb"]), whole(tabs["gcb"]), state_spec],
        out_specs=[pl.BlockSpec((None, None, heads, dk, dv), lambda b, i: (b, n - 1 - i, 0, 0, 0)), state_spec],
        out_shape=[jax.ShapeDtypeStruct((bsz, n, heads, dk, dv), F32), state_shape],
        scratch_shapes=[pltpu.VMEM((heads, dk, dv), F32)],
        compiler_params=_cparams(("parallel", "arbitrary")),
        name="retention_backward_states",
    )(k, v, tabs["kdb"], tabs["gcb"], init_b)
    fwd = lambda b, i: (0, b * n + i, 0)
    o, fin_f = pl.pallas_call(
        _ret_main_body,
        grid=(bsz, n),
        in_specs=[pl.BlockSpec((heads, cl, dk), fwd), pl.BlockSpec((heads, cl, dk), fwd),
                  pl.BlockSpec((heads, cl, dv), fwd), pl.BlockSpec((heads, cl, dv), fwd),
                  whole(tabs["mask"]), whole(tabs["qdf"]), whole(tabs["kdf"]), whole(tabs["qdb"]), whole(tabs["gcf"]),
                  state_spec, pl.BlockSpec((None, None, heads, dk, dv), lambda b, i: (b, i, 0, 0, 0))],
        out_specs=[pl.BlockSpec((heads, cl, dv), fwd), state_spec],
        out_shape=[jax.ShapeDtypeStruct((heads, bsz * seq, dv), F32), state_shape],
        scratch_shapes=[pltpu.VMEM((heads, dk, dv), F32)],
        compiler_params=_cparams(("parallel", "arbitrary")),
        name="retention",
    )(q, k, v, g, tabs["mask"], tabs["qdf"], tabs["kdf"], tabs["qdb"], tabs["gcf"], init_f, sprev_b)
    return o, fin_f, fin_b


def _swa_body(*refs, local, seq, scale):
    if local:
        q_ref, kp_ref, kc_ref, kn_ref, vp_ref, vc_ref, vn_ref, kx_ref, vx_ref, sink_ref, o_ref = refs
    else:
        q_ref, kx_ref, vx_ref, sink_ref, o_ref = refs
    i = pl.program_id(1)
    bk = q_ref.shape[1]
    nkv = kx_ref.shape[0]
    grp = q_ref.shape[0] // nkv
    nt = (((1,), (1,)), ((), ()))
    if local:
        row = lax.broadcasted_iota(jnp.int32, (bk, 3 * bk), 0)
        col = lax.broadcasted_iota(jnp.int32, (bk, 3 * bk), 1)
        key_pos = (i - 1) * bk + col
        valid = (jnp.abs(col - (row + bk)) <= SWA_WINDOW) & (key_pos >= 0) & (key_pos < seq)
    for kv in range(nkv):
        kx = kx_ref[kv].astype(BF16)
        vx = vx_ref[kv].astype(BF16)
        if local:
            kl = jnp.concatenate([kp_ref[kv], kc_ref[kv], kn_ref[kv]], axis=0).astype(BF16)
            vl = jnp.concatenate([vp_ref[kv], vc_ref[kv], vn_ref[kv]], axis=0).astype(BF16)
        for gi in range(grp):
            h = kv * grp + gi
            q = q_ref[h].astype(BF16)
            sink = sink_ref[h]
            s_x = lax.dot_general(q, kx, nt, preferred_element_type=F32) * scale
            m = jnp.maximum(jnp.max(s_x, axis=-1, keepdims=True), sink)
            if local:
                s_l = lax.dot_general(q, kl, nt, preferred_element_type=F32) * scale
                s_l = jnp.where(valid, s_l, NEG_INF)
                m = jnp.maximum(m, jnp.max(s_l, axis=-1, keepdims=True))
            p_x = jnp.exp(s_x - m)
            den = jnp.sum(p_x, axis=-1, keepdims=True) + jnp.exp(sink - m)
            o = jnp.dot(p_x.astype(BF16), vx, preferred_element_type=F32)
            if local:
                p_l = jnp.exp(s_l - m)
                den = den + jnp.sum(p_l, axis=-1, keepdims=True)
                o = o + jnp.dot(p_l.astype(BF16), vl, preferred_element_type=F32)
            o_ref[h] = o / den


def _swa(q, k, v, kx, vx, sink, bsz, seq, seq_c, local):
    hq, m, hd = q.shape
    hkv = kx.shape[0]
    bk = SWA_BLOCK
    nb = seq // bk
    scale = hd ** -0.5
    cur = lambda b, i: (0, b * nb + i, 0)
    prv = lambda b, i: (0, b * nb + jnp.maximum(i - 1, 0), 0)
    nxt = lambda b, i: (0, b * nb + jnp.minimum(i + 1, nb - 1), 0)
    ctx_spec = pl.BlockSpec((hkv, seq_c, hd), lambda b, i: (0, b, 0))
    sink_spec = pl.BlockSpec((hq, 1, 1), lambda b, i: (0, 0, 0))
    kvb = lambda f: pl.BlockSpec((hkv, bk, hd), f)
    in_specs = [pl.BlockSpec((hq, bk, hd), cur)]
    args = [q]
    if local:
        in_specs += [kvb(prv), kvb(cur), kvb(nxt), kvb(prv), kvb(cur), kvb(nxt)]
        args += [k, k, k, v, v, v]
    in_specs += [ctx_spec, ctx_spec, sink_spec]
    args += [kx, vx, sink.astype(F32).reshape(hq, 1, 1)]
    return pl.pallas_call(
        functools.partial(_swa_body, local=local, seq=seq, scale=scale),
        grid=(bsz, nb),
        in_specs=in_specs,
        out_specs=pl.BlockSpec((hq, bk, hd), cur),
        out_shape=jax.ShapeDtypeStruct((hq, m, hd), F32),
        compiler_params=_cparams(("parallel", "parallel")),
        name="swa" if local else "swa_ctx",
    )(*args)


def _mixer_cd_core(lat, cx, bsz, seq, seq_c, decay_logit, sink, with_ctx, tm, tm_c):
    dk = lat["rq"].shape[1] // RET_HEADS
    hd = lat["sq"].shape[1] // SWA_Q_HEADS
    t1 = _rope_tables_1d(seq, dk, RET_HEADS)
    rq = _rope_heads(lat["rq"], t1, dk // 2, RET_HEADS, dk ** -0.5, tm, seq, "ret_rope_q")
    rk = _rope_heads(lat["rk"], t1, dk // 2, RET_HEADS, 1.0, tm, seq, "ret_rope_k")
    rqc = _rope_heads(cx["rq"], None, 0, RET_HEADS, dk ** -0.5, tm_c, seq_c, "ret_heads_q_ctx")
    rkc = _rope_heads(cx["rk"], None, 0, RET_HEADS, 1.0, tm_c, seq_c, "ret_heads_k_ctx")
    tabs = _ret_tables(decay_logit)
    dv = lat["rv"].shape[2]
    zero = jnp.zeros((bsz, RET_HEADS, dk, dv), F32)
    ro_c, fin_f, fin_b = _retention(rqc, rkc, cx["rv"], cx["rg"], bsz, seq_c, tabs, zero, zero)
    ro_l, _, _ = _retention(rq, rk, lat["rv"], lat["rg"], bsz, seq, tabs, fin_f, fin_b)
    t2 = _rope_tables_2d(seq, hd, 1)
    sq = _rope_heads(lat["sq"], [jnp.tile(t, (1, SWA_Q_HEADS)) for t in t2], hd // 4, SWA_Q_HEADS, 1.0, tm, seq,
                     "swa_rope_q")
    sk = _rope_heads(lat["sk"], [jnp.tile(t, (1, SWA_KV_HEADS)) for t in t2], hd // 4, SWA_KV_HEADS, 1.0, tm, seq,
                     "swa_rope_k")
    skc = _rope_heads(cx["sk"], None, 0, SWA_KV_HEADS, 1.0, tm_c, seq_c, "swa_heads_k_ctx")
    so_l = _swa(sq, sk, lat["sv"], skc, cx["sv"], sink, bsz, seq, seq_c, True)
    so_c = None
    if with_ctx:
        sqc = _rope_heads(cx["sq"], None, 0, SWA_Q_HEADS, 1.0, tm_c, seq_c, "swa_heads_q_ctx")
        so_c = _swa(sqc, None, None, skc, cx["sv"], sink, bsz, seq_c, seq_c, False)
    return ro_l, so_l, (ro_c if with_ctx else None), so_c


def _router_body(x_ref, g_ref, rw_ref, rb_ref, tril_ref, sc_ref, sh_ref, hx_ref, ti_ref, gt_ref, pos_ref, cnt_ref):
    @pl.when(pl.program_id(0) == 0)
    def _():
        cnt_ref[...] = jnp.zeros_like(cnt_ref)

    hx = _norm_mod(x_ref[...], g_ref[...], sc_ref[...], sh_ref[...])
    hx_ref[...] = hx
    logits = jnp.dot(hx, rw_ref[...], precision=HI, preferred_element_type=F32) + rb_ref[...]
    lane = lax.broadcasted_iota(jnp.int32, logits.shape, 1).astype(F32)
    rem = logits
    vals, hots = [], []
    for k in range(TOP_K):
        m = jnp.max(rem, axis=-1, keepdims=True)
        idx = jnp.min(jnp.where(rem == m, lane, float(N_EXPERTS)), axis=-1, keepdims=True)
        hot = lane == idx
        rem = jnp.where(hot, NEG_INF, rem)
        vals.append(m)
        hots.append(hot.astype(F32))
        ti_ref[:, k:k + 1] = idx.astype(jnp.int32)
    exps = [jnp.exp(v - vals[0]) for v in vals]
    den = exps[0] + exps[1] + exps[2] + exps[3]
    for k in range(TOP_K):
        gt_ref[:, k:k + 1] = exps[k] / den
    sel = hots[0] + hots[1] + hots[2] + hots[3]
    before = jnp.dot(tril_ref[...], sel.astype(BF16), preferred_element_type=F32) + cnt_ref[...]
    for k in range(TOP_K):
        pos_ref[:, k:k + 1] = jnp.sum(hots[k] * before, axis=-1, keepdims=True).astype(jnp.int32)
    cnt_ref[...] += jnp.sum(sel, axis=0, keepdims=True)


def _moe_route(x2, g, sc, sh, router_w, router_b, tr, rows_per_batch):
    n, d = x2.shape
    tiles_per_batch = rows_per_batch // tr
    tril = jnp.asarray(np.tril(np.ones((tr, tr), np.float32), -1)).astype(BF16)
    whole = lambda a: pl.BlockSpec(a.shape, lambda i, nd=a.ndim: (0,) * nd)
    bat = pl.BlockSpec((None, 1, d), lambda i: (i // tiles_per_batch, 0, 0))
    g2, rb2 = g.reshape(1, -1), router_b.reshape(1, -1)
    small = lambda dt: jax.ShapeDtypeStruct((n, TOP_K), dt)
    small_spec = pl.BlockSpec((tr, TOP_K), lambda i: (i, 0))
    return pl.pallas_call(
        _router_body,
        grid=(n // tr,),
        in_specs=[pl.BlockSpec((tr, d), lambda i: (i, 0)), whole(g2), whole(router_w), whole(rb2), whole(tril), bat, bat],
        out_specs=[pl.BlockSpec((tr, d), lambda i: (i, 0)), small_spec, small_spec, small_spec,
                   pl.BlockSpec((1, N_EXPERTS), lambda i: (0, 0))],
        out_shape=[jax.ShapeDtypeStruct((n, d), F32), small(jnp.int32), small(F32), small(jnp.int32),
                   jax.ShapeDtypeStruct((1, N_EXPERTS), F32)],
        compiler_params=_cparams(("arbitrary",)),
        name="moe_router",
    )(x2, g2, router_w, rb2, tril, sc, sh)


def _row_copy(src, s_row, dst, d_row, sem):
    return pltpu.make_async_copy(src.at[pl.ds(s_row, 1)], dst.at[pl.ds(d_row, 1)], sem)


def _dispatch_body(dest_ref, hx_ref, xs_in_ref, xs_ref, sem, *, td):
    del xs_in_ref
    base = pl.program_id(0) * td

    def issue(n, carry):
        for k in range(TOP_K):
            _row_copy(hx_ref, base + n, xs_ref, dest_ref[n * TOP_K + k], sem).start()
        return carry

    def drain(n, carry):
        for k in range(TOP_K):
            _row_copy(hx_ref, 0, xs_ref, 0, sem).wait()
        return carry

    lax.fori_loop(0, td, issue, 0)
    lax.fori_loop(0, td, drain, 0)


def _moe_dispatch(hx, dest_flat, n_slots, td):
    n, d = hx.shape
    zeros = jnp.zeros((n_slots, d), F32)
    return pl.pallas_call(
        functools.partial(_dispatch_body, td=td),
        grid=(n // td,),
        in_specs=[pl.BlockSpec((td * TOP_K,), lambda i: (i,), memory_space=pltpu.SMEM),
                  pl.BlockSpec(memory_space=pl.ANY), pl.BlockSpec(memory_space=pl.ANY)],
        out_specs=pl.BlockSpec(memory_space=pl.ANY),
        out_shape=jax.ShapeDtypeStruct((n_slots, d), F32),
        scratch_shapes=[pltpu.SemaphoreType.DMA(())],
        input_output_aliases={2: 0},
        compiler_params=pltpu.CompilerParams(dimension_semantics=("arbitrary",), has_side_effects=True),
        name="moe_dispatch",
    )(dest_flat, hx, zeros)


def _ffn_body(be_ref, nu_ref, x_ref, wgu_ref, bgu_ref, wdn_ref, bdn_ref, o_ref, wgu_bf, wdn_bf):
    j = pl.program_id(0)
    e = be_ref[j]
    prev = be_ref[jnp.maximum(j - 1, 0)]

    @pl.when((j == 0) | (e != prev))
    def _():
        wgu_bf[...] = wgu_ref[...].astype(BF16)
        wdn_bf[...] = wdn_ref[...].astype(BF16)

    @pl.when(j < nu_ref[0])
    def _():
        f = wdn_ref.shape[0]
        gu = jnp.dot(x_ref[...].astype(BF16), wgu_bf[...], preferred_element_type=F32) + bgu_ref[...]
        gate = jnp.minimum(gu[:, :f], SWIGLU_LIMIT)
        up = jnp.clip(gu[:, f:], -SWIGLU_LIMIT, SWIGLU_LIMIT)
        act = gate * jax.nn.sigmoid(SWIGLU_ALPHA * gate) * (up + 1.0)
        o_ref[...] = jnp.dot(act.astype(BF16), wdn_bf[...], preferred_element_type=F32) + bdn_ref[...]

    @pl.when(j >= nu_ref[0])
    def _():
        o_ref[...] = jnp.zeros_like(o_ref)


def _moe_ffn(xs, block_exp, n_used, w_gu, b_gu, w_dn, b_dn, tm):
    n_slots, d = xs.shape
    n_exp, _, f2 = w_gu.shape
    f = w_dn.shape[1]
    blk = lambda j, be, nu: (jnp.minimum(j, nu[0] - 1), 0)
    exp3 = lambda j, be, nu: (be[j], 0, 0)
    grid_spec = pltpu.PrefetchScalarGridSpec(
        num_scalar_prefetch=2,
        grid=(n_slots // tm,),
        in_specs=[pl.BlockSpec((tm, d), blk), pl.BlockSpec((None, d, f2), exp3), pl.BlockSpec((None, 1, f2), exp3),
                  pl.BlockSpec((None, f, d), exp3), pl.BlockSpec((None, 1, d), exp3)],
        out_specs=pl.BlockSpec((tm, d), lambda j, be, nu: (j, 0)),
        scratch_shapes=[pltpu.VMEM((d, f2), BF16), pltpu.VMEM((f, d), BF16)],
    )
    return pl.pallas_call(
        _ffn_body,
        grid_spec=grid_spec,
        out_shape=jax.ShapeDtypeStruct((n_slots, d), F32),
        compiler_params=_cparams(("arbitrary",)),
        name="moe_ffn",
    )(block_exp, n_used, xs, w_gu, b_gu.reshape(n_exp, 1, f2), w_dn, b_dn.reshape(n_exp, 1, d))


def _combine_body(dest_ref, gt_ref, x_ref, ys_ref, g_ref, o_ref, buf, sem, *, tc):
    def issue(n, carry):
        for k in range(TOP_K):
            pltpu.make_async_copy(ys_ref.at[pl.ds(dest_ref[n * TOP_K + k], 1)], buf.at[k, pl.ds(n, 1)], sem).start()
        return carry

    def drain(n, carry):
        for k in range(TOP_K):
            pltpu.make_async_copy(ys_ref.at[pl.ds(0, 1)], buf.at[0, pl.ds(0, 1)], sem).wait()
        return carry

    lax.fori_loop(0, tc, issue, 0)
    lax.fori_loop(0, tc, drain, 0)
    acc = gt_ref[:, 0:1] * buf[0]
    for k in range(1, TOP_K):
        acc = acc + gt_ref[:, k:k + 1] * buf[k]
    o_ref[...] = x_ref[...] + g_ref[...] * acc


def _moe_combine(ys, dest_flat, gates, x2, gate2, tc, rows_per_batch):
    n, d = x2.shape
    tiles_per_batch = rows_per_batch // tc
    return pl.pallas_call(
        functools.partial(_combine_body, tc=tc),
        grid=(n // tc,),
        in_specs=[pl.BlockSpec((tc * TOP_K,), lambda i: (i,), memory_space=pltpu.SMEM),
                  pl.BlockSpec((tc, TOP_K), lambda i: (i, 0)), pl.BlockSpec((tc, d), lambda i: (i, 0)),
                  pl.BlockSpec(memory_space=pl.ANY),
                  pl.BlockSpec((None, 1, d), lambda i: (i // tiles_per_batch, 0, 0))],
        out_specs=pl.BlockSpec((tc, d), lambda i: (i, 0)),
        out_shape=jax.ShapeDtypeStruct((n, d), F32),
        scratch_shapes=[pltpu.VMEM((TOP_K, tc, d), F32), pltpu.SemaphoreType.DMA(())],
        compiler_params=_cparams(("arbitrary",)),
        name="moe_combine",
    )(dest_flat, gates, x2, ys, gate2)


def _moe_layer(x2, g, sc, sh, gate2, rows_per_batch, router_w, router_b, w_gu, b_gu, w_dn, b_dn, tr, tc):
    n, _ = x2.shape
    tm = MOE_TM
    hx, top_i, gates, pos, counts = _moe_route(x2, g, sc, sh, router_w, router_b, tr, rows_per_batch)
    counts = counts[0].astype(jnp.int32)
    padded = (counts + tm - 1) // tm * tm
    pad_end = jnp.cumsum(padded)
    pad_start = pad_end - padded
    n_blocks = n * TOP_K // tm + N_EXPERTS
    n_used = (pad_end[-1] // tm).astype(jnp.int32)
    blk_ids = jnp.arange(n_blocks, dtype=jnp.int32)
    block_exp = jnp.minimum(jnp.searchsorted(pad_end, blk_ids * tm, side="right"), N_EXPERTS - 1).astype(jnp.int32)
    block_exp = jnp.where(blk_ids < n_used, block_exp, block_exp[jnp.maximum(n_used - 1, 0)])
    dest = (pad_start[top_i] + pos).astype(jnp.int32).reshape(-1)
    xs = _moe_dispatch(hx, dest, n_blocks * tm, tr)
    ys = _moe_ffn(xs, block_exp, n_used.reshape(1), w_gu, b_gu, w_dn, b_dn, tm)
    return _moe_combine(ys, dest, gates, x2, gate2, tc, rows_per_batch)


def _mod_body(c_ref, w_ref, b_ref, o_ref):
    o_ref[...] = jnp.dot(jax.nn.silu(c_ref[...]), w_ref[...], precision=HI, preferred_element_type=F32) + b_ref[...]


def _modulation(cc, mod_w, mod_b):
    depth, d, d6 = mod_w.shape
    r = cc.shape[0]
    return pl.pallas_call(
        _mod_body,
        grid=(depth, d6 // d),
        in_specs=[pl.BlockSpec((r, d), lambda l, j: (0, 0)), pl.BlockSpec((None, d, d), lambda l, j: (l, 0, j)),
                  pl.BlockSpec((None, 1, d), lambda l, j: (l, 0, j))],
        out_specs=pl.BlockSpec((None, r, d), lambda l, j: (l, 0, j)),
        out_shape=jax.ShapeDtypeStruct((depth, r, d6), F32),
        compiler_params=_cparams(("parallel", "parallel")),
        name="modulation",
    )(cc, mod_w, mod_b.reshape(depth, 1, d6))


def kernel(x, c, ctx, c_ctx, mod_w, mod_b, norm1_g, norm2_g, ab_w_in, ab_w_out, hy_short_w, hy_short_b, hy_f_w1, hy_f_b1, hy_f_w2, hy_f_b2, hy_f_w3, hy_bias, s5_lambda_re, s5_lambda_im, s5_log_dt, s5_b_re, s5_b_im, s5_c_re, s5_c_im, s5_d, s5_glu_w, s5_glu_b, cd_w_in, cd_w_out, ret_decay_logit, swa_sink, router_w, router_b, exp_w_gu, exp_b_gu, exp_w_down, exp_b_down, final_g):
    bsz, seq, d = x.shape
    seq_c = ctx.shape[1]
    depth = mod_w.shape[0]
    m_l, m_c = bsz * seq, bsz * seq_c
    tm, tm_c, tt = 512, 256, 256
    xl = x.reshape(m_l, d)
    xc = ctx.reshape(m_c, d)
    cc = jnp.concatenate([c, c_ctx[None], jnp.zeros((SUBLANES - bsz - 1, d), F32)], axis=0)
    mods = _modulation(cc, mod_w, mod_b)
    hy_w = hy_f_w3.shape[2]
    for layer in range(depth):
        with_ctx = layer < depth - 1
        i = layer // 2
        sh1, sc1, g1, sh2, sc2, g2 = [t[:, None, :] for t in jnp.split(mods[layer, :bsz], 6, axis=-1)]
        csh1, csc1, cg1, csh2, csc2, cg2 = [t[:, None, :] for t in jnp.split(mods[layer, bsz:bsz + 1], 6, axis=-1)]
        if layer % 2 == 0:
            w_in, w_out = ab_w_in[i], ab_w_out[i]
            splits = [(3 * hy_w, 0), (w_in.shape[1] - 3 * hy_w, 0)]
            pa, pb = _norm_mod_matmul(xl, norm1_g[layer], sc1, sh1, w_in, splits, tm, seq, "ab_in")
            pac, pbc = _norm_mod_matmul(xc, norm1_g[layer], csc1, csh1, w_in, splits, tm_c, m_c, "ab_in_ctx")
            hy = (hy_short_w[i], hy_short_b[i], hy_f_w1[i], hy_f_b1[i], hy_f_w2[i], hy_f_b2[i], hy_f_w3[i], hy_bias[i])
            ya = _hyena(pa, bsz, seq, *hy, tt)
            yb, ybc = _s5_mixer(pb, pbc, bsz, seq, seq_c, s5_lambda_re[i], s5_lambda_im[i], s5_log_dt[i], s5_b_re[i],
                                s5_b_im[i], s5_c_re[i], s5_c_im[i], s5_d[i], s5_glu_w[i], s5_glu_b[i], tm, tm_c)
            ws = [w_out[:hy_w], w_out[hy_w:]]
            xl = _out_proj([ya, yb], ws, xl, g1, tm, seq, "ab_out")
            if with_ctx:
                yac = _hyena(pac, bsz, seq_c, *hy, tt)
                xc = _out_proj([yac, ybc], ws, xc, cg1, tm_c, m_c, "ab_out_ctx")
        else:
            w_in, w_out = cd_w_in[i], cd_w_out[i]
            qk = RET_HEADS * (d // 16)
            vw = 2 * qk
            qw = SWA_Q_HEADS * (d // 16)
            kw = SWA_KV_HEADS * (d // 16)
            splits = [(qk, 0), (qk, 0), (vw, RET_HEADS), (vw, RET_HEADS), (qw, 0), (kw, 0), (kw, SWA_KV_HEADS)]
            names = ("rq", "rk", "rv", "rg", "sq", "sk", "sv")
            lat = dict(zip(names, _norm_mod_matmul(xl, norm1_g[layer], sc1, sh1, w_in, splits, tm, seq, "cd_in")))
            cx = dict(zip(names, _norm_mod_matmul(xc, norm1_g[layer], csc1, csh1, w_in, splits, tm_c, m_c, "cd_in_ctx")))
            ro_l, so_l, ro_c, so_c = _mixer_cd_core(lat, cx, bsz, seq, seq_c, ret_decay_logit[i], swa_sink[i],
                                                    with_ctx, tm, tm_c)
            ws = [w_out[:vw].reshape(RET_HEADS, vw // RET_HEADS, d), w_out[vw:].reshape(SWA_Q_HEADS, qw // SWA_Q_HEADS, d)]
            xl = _out_proj([ro_l, so_l], ws, xl, g1, tm, seq, "cd_out")
            if with_ctx:
                xc = _out_proj([ro_c, so_c], ws, xc, cg1, tm_c, m_c, "cd_out_ctx")
        moe_w = (router_w[layer], router_b[layer], exp_w_gu[layer], exp_b_gu[layer], exp_w_down[layer], exp_b_down[layer])
        xl = _moe_layer(xl, norm2_g[layer], sc2, sh2, g2, seq, *moe_w, tm, tm_c)
        if with_ctx:
            xc = _moe_layer(xc, norm2_g[layer], csc2, csh2, cg2, m_c, *moe_w, tm_c, tm_c)
    out = _rows_call(_final_norm_body, [xl], [final_g.reshape(1, -1)], [], [((m_l, d), F32)], tm, m_l, "final_norm")[0]
    return out.reshape(bsz, seq, d)
```

```python
import functools
import math

import numpy as np
import jax
import jax.numpy as jnp
from jax import lax
from jax.experimental import pallas as pl
from jax.experimental.pallas import tpu as pltpu

F32 = jnp.float32
BF16 = jnp.bfloat16
HI = lax.Precision.HIGHEST

EPS = 1e-6
NEG_INF = -1e30
ROPE_BASE = 10000.0
GRID_W = 64

HY_SHORT = 3
HY_BANDS = 16
HY_SHIFT = 0.05
HY_FAST_DECAY = math.log(1e-2) / 0.3
HY_SLOW_DECAY = math.log(1e-2) / 1.5
S5_GROUP = 16
S5_STATE = 64
S5_CHUNK = 8
RET_HEADS = 4
RET_CHUNK = 128
SWA_Q_HEADS = 8
SWA_KV_HEADS = 2
SWA_WINDOW = 128
SWA_BLOCK = 128
N_EXPERTS = 32
TOP_K = 4
SWIGLU_LIMIT = 7.0
SWIGLU_ALPHA = 1.702

LANES = 128
SUBLANES = 8
VMEM_LIMIT = 52 * 2**20
FFT_B = 128
MOE_TM = 256


def _cparams(sem):
    return pltpu.CompilerParams(dimension_semantics=sem, vmem_limit_bytes=VMEM_LIMIT)


def _rows_call(body, rows, consts, batched, outs, tm, rows_per_batch, name, periodic=()):
    m = rows[0].shape[-2]
    assert m % tm == 0 and rows_per_batch % tm == 0
    tiles_per_batch = rows_per_batch // tm

    def row_spec(shape):
        if len(shape) == 2:
            return pl.BlockSpec((tm, shape[1]), lambda i: (i, 0))
        return pl.BlockSpec((shape[0], tm, shape[2]), lambda i: (0, i, 0))

    in_specs = [row_spec(a.shape) for a in rows]
    for a in periodic:
        in_specs.append(pl.BlockSpec((tm, a.shape[1]), lambda i: (i % tiles_per_batch, 0)))
    for a in consts:
        in_specs.append(pl.BlockSpec(a.shape, lambda i, n=a.ndim: (0,) * n))
    for a in batched:
        in_specs.append(pl.BlockSpec((None, 1, a.shape[2]), lambda i: (i // tiles_per_batch, 0, 0)))
    out_specs = [row_spec(s) for s, _ in outs]
    out_shape = [jax.ShapeDtypeStruct(s, d) for s, d in outs]
    res = pl.pallas_call(
        body,
        grid=(m // tm,),
        in_specs=in_specs,
        out_specs=out_specs,
        out_shape=out_shape,
        compiler_params=_cparams(("parallel",)),
        name=name,
    )(*rows, *periodic, *consts, *batched)
    return res


def _norm_mod(x, g, sc, sh):
    y = x * lax.rsqrt(jnp.mean(x * x, axis=-1, keepdims=True) + EPS) * g
    return y * (1.0 + sc) + sh


def _norm_mod_matmul_body(x_ref, g_ref, w_ref, sc_ref, sh_ref, *o_refs, splits):
    h = _norm_mod(x_ref[...], g_ref[...], sc_ref[...], sh_ref[...])
    r = jnp.dot(h.astype(BF16), w_ref[...], preferred_element_type=F32)
    off = 0
    for o_ref, n in zip(o_refs, splits):
        if o_ref.ndim == 3:
            hd = o_ref.shape[2]
            for h_i in range(o_ref.shape[0]):
                o_ref[h_i] = r[:, off + h_i * hd: off + (h_i + 1) * hd]
        else:
            o_ref[...] = r[:, off:off + n]
        off += n


def _norm_mod_matmul(x2, g, sc, sh, w, splits, tm, rows_per_batch, name):
    m = x2.shape[0]
    outs = []
    for n, heads in splits:
        outs.append(((heads, m, n // heads), F32) if heads else ((m, n), F32))
    splits = [n for n, _ in splits]
    body = functools.partial(_norm_mod_matmul_body, splits=splits)
    return _rows_call(body, [x2], [g.reshape(1, -1), w.astype(BF16)], [sc, sh], outs, tm, rows_per_batch, name)


def _out_proj_body(*refs, n_in):
    a_refs = refs[:n_in]
    x_ref = refs[n_in]
    w_refs = refs[n_in + 1: 2 * n_in + 1]
    g_ref = refs[2 * n_in + 1]
    o_ref = refs[2 * n_in + 2]
    acc = None
    for a_ref, w_ref in zip(a_refs, w_refs):
        if a_ref.ndim == 3:
            for h_i in range(a_ref.shape[0]):
                t = jnp.dot(a_ref[h_i].astype(BF16), w_ref[h_i], preferred_element_type=F32)
                acc = t if acc is None else acc + t
        else:
            t = jnp.dot(a_ref[...].astype(BF16), w_ref[...], preferred_element_type=F32)
            acc = t if acc is None else acc + t
    o_ref[...] = x_ref[...] + g_ref[...] * acc


def _out_proj(parts, ws, x2, gate, tm, rows_per_batch, name):
    m, d = x2.shape
    body = functools.partial(_out_proj_body, n_in=len(parts))
    ws = [w.astype(BF16) for w in ws]
    return _rows_call(body, list(parts) + [x2], ws, [gate], [((m, d), F32)], tm, rows_per_batch, name)[0]


def _final_norm_body(x_ref, g_ref, o_ref):
    x = x_ref[...]
    o_ref[...] = x * lax.rsqrt(jnp.mean(x * x, axis=-1, keepdims=True) + EPS) * g_ref[...]


def _hy_prep_body(u_ref, p_ref, n_ref, w_ref, b_ref, x0_ref, z_ref, *, width):
    i = pl.program_id(1)
    last = pl.num_programs(1) - 1
    u = u_ref[...]
    tt = u.shape[0]
    prev_row = jnp.where(i == 0, 0.0, p_ref[SUBLANES - 1:SUBLANES, :])
    next_row = jnp.where(i == last, 0.0, n_ref[0:1, :])
    rows = lax.broadcasted_iota(jnp.int32, (tt, 1), 0)
    up = jnp.where(rows == 0, prev_row, pltpu.roll(u, 1, axis=0))
    dn = jnp.where(rows == tt - 1, next_row, pltpu.roll(u, tt - 1, axis=0))
    y = w_ref[0:1, :] * up + w_ref[1:2, :] * u + w_ref[2:3, :] * dn + b_ref[...]
    x0_ref[...] = y[:, :width]
    z_ref[...] = y[:, 2 * width:] * y[:, width:2 * width]


def _hyena_prep(p, short_w, short_b, bsz, seq, tt):
    w3 = p.shape[1]
    width = w3 // 3
    p3 = p.reshape(bsz, seq, w3)
    nt = seq // tt
    sub = tt // SUBLANES
    nsub = seq // SUBLANES
    body = functools.partial(_hy_prep_body, width=width)
    x0, z = pl.pallas_call(
        body,
        grid=(bsz, nt),
        in_specs=[
            pl.BlockSpec((None, tt, w3), lambda b, i: (b, i, 0)),
            pl.BlockSpec((None, SUBLANES, w3), lambda b, i: (b, jnp.maximum(i * sub - 1, 0), 0)),
            pl.BlockSpec((None, SUBLANES, w3), lambda b, i: (b, jnp.minimum((i + 1) * sub, nsub - 1), 0)),
            pl.BlockSpec((HY_SHORT, w3), lambda b, i: (0, 0)),
            pl.BlockSpec((1, w3), lambda b, i: (0, 0)),
        ],
        out_specs=[pl.BlockSpec((None, tt, width), lambda b, i: (b, i, 0))] * 2,
        out_shape=[jax.ShapeDtypeStruct((bsz, seq, width), F32)] * 2,
        compiler_params=_cparams(("parallel", "parallel")),
        name="hyena_prep",
    )(p3, p3, p3, short_w, short_b.reshape(1, -1))
    return x0, z


def _hy_filter_body(t_ref, w_ref, lag_ref, bands_ref, deltas_ref, w1t_ref, w1c_ref, w1s_ref, b1_ref,
                    w2_ref, b2_ref, w3_ref, h_ref, s_ref):
    i = pl.program_id(0)
    arg = w_ref[...] * bands_ref[...]
    pre = (t_ref[...] * w1t_ref[...]
           + jnp.dot(jnp.cos(arg), w1c_ref[...], precision=HI, preferred_element_type=F32)
           + jnp.dot(-jnp.sin(arg), w1s_ref[...], precision=HI, preferred_element_type=F32)
           + b1_ref[...])
    h1 = jnp.sin(pre)
    h2 = jnp.sin(jnp.dot(h1, w2_ref[...], precision=HI, preferred_element_type=F32) + b2_ref[...])
    h3 = jnp.dot(h2, w3_ref[...], precision=HI, preferred_element_type=F32)
    h = h3 * (jnp.exp(-lag_ref[...] * deltas_ref[...]) + HY_SHIFT)
    h_ref[...] = h

    @pl.when(i == 0)
    def _():
        s_ref[...] = jnp.zeros_like(s_ref)

    s_ref[...] += jnp.sum(jnp.abs(h), axis=0, keepdims=True)


def _hyena_filter(seq, w1, b1, w2, b2, w3):
    width = w3.shape[1]
    pos = jnp.arange(seq, dtype=F32)
    t = (pos / seq)[:, None]
    w = (2.0 * math.pi * pos / seq)[:, None]
    lag = (jnp.abs(pos - seq // 2) / (seq / 2))[:, None]
    bands = jnp.linspace(1e-4, HY_BANDS - 1, HY_BANDS, dtype=F32)[None]
    deltas = jnp.abs(jnp.linspace(HY_FAST_DECAY, HY_SLOW_DECAY, width, dtype=F32))[None]
    tl = min(seq, 1024)
    col = pl.BlockSpec((tl, 1), lambda i: (i, 0))

    def whole(a):
        return pl.BlockSpec(a.shape, lambda i, n=a.ndim: (0,) * n)

    consts = [bands, deltas, w1[0:1], w1[1:1 + HY_BANDS], w1[1 + HY_BANDS:], b1.reshape(1, -1),
              w2, b2.reshape(1, -1), w3]
    h, s = pl.pallas_call(
        _hy_filter_body,
        grid=(seq // tl,),
        in_specs=[col, col, col] + [whole(a) for a in consts],
        out_specs=[pl.BlockSpec((tl, width), lambda i: (i, 0)), pl.BlockSpec((1, width), lambda i: (0, 0))],
        out_shape=[jax.ShapeDtypeStruct((seq, width), F32), jax.ShapeDtypeStruct((1, width), F32)],
        compiler_params=_cparams(("arbitrary",)),
        name="hyena_filter",
    )(t, w, lag, *consts)
    return h, s


def _dft_tables(na):
    n = na * FFT_B
    a = np.arange(na)
    b = np.arange(FFT_B)
    ang1 = 2.0 * np.pi * np.outer(a, a) / na
    ang2 = 2.0 * np.pi * np.outer(b, b) / FFT_B
    angt = 2.0 * np.pi * np.outer(b, a) / n
    c2, s2 = np.cos(ang2), np.sin(ang2)
    tabs = dict(
        f1=np.concatenate([np.cos(ang1), -np.sin(ang1)], axis=0),
        f2=np.block([[c2, s2], [-s2, c2]]),
        f2i=np.block([[c2, -s2], [s2, c2]]),
        f3=np.concatenate([np.cos(ang1), -np.sin(ang1)], axis=1) / n,
        twc_b=np.cos(angt)[:, :, None], tws_b=np.sin(angt)[:, :, None],
        twc_c=np.cos(angt).T[:, :, None], tws_c=np.sin(angt).T[:, :, None],
    )
    return {k: jnp.asarray(v, F32) for k, v in tabs.items()}


FFT_G = SUBLANES


def _fft1_body(x_ref, f_ref, tc_ref, ts_ref, sc_ref, o_ref, *, na):
    f = f_ref[...].astype(BF16)
    for j in range(FFT_G):
        x = (x_ref[:, j, :] * sc_ref[...]).astype(BF16)
        g = jnp.dot(f, x, preferred_element_type=F32)
        gr, gi = g[:na], g[na:]
        tc, ts = tc_ref[j], ts_ref[j]
        o_ref[:, j, :] = jnp.concatenate([gr * tc + gi * ts, gi * tc - gr * ts], axis=0)


def _fft_stage1(x4, scale, tabs, na):
    bz, ka, _, ch = x4.shape
    return pl.pallas_call(
        functools.partial(_fft1_body, na=na),
        grid=(bz, FFT_B // FFT_G),
        in_specs=[
            pl.BlockSpec((None, ka, FFT_G, ch), lambda z, b: (z, 0, b, 0)),
            pl.BlockSpec((2 * na, ka), lambda z, b: (0, 0)),
            pl.BlockSpec((FFT_G, na, 1), lambda z, b: (b, 0, 0)),
            pl.BlockSpec((FFT_G, na, 1), lambda z, b: (b, 0, 0)),
            pl.BlockSpec((1, ch), lambda z, b: (0, 0)),
        ],
        out_specs=pl.BlockSpec((None, 2 * na, FFT_G, ch), lambda z, b: (z, 0, b, 0)),
        out_shape=jax.ShapeDtypeStruct((bz, 2 * na, FFT_B, ch), F32),
        compiler_params=_cparams(("parallel", "parallel")),
        name="fft_stage1",
    )(x4, tabs["f1"][:, :ka], tabs["twc_b"], tabs["tws_b"], scale)


def _fft2_spec_body(ar_ref, ai_ref, f_ref, o_ref):
    f = f_ref[...].astype(BF16)
    for j in range(FFT_G):
        v = jnp.concatenate([ar_ref[j], ai_ref[j]], axis=0).astype(BF16)
        o_ref[j] = jnp.dot(f, v, preferred_element_type=F32)


def _fft2_conv_body(ar_ref, ai_ref, h_ref, f_ref, fi_ref, tc_ref, ts_ref, o_ref):
    f = f_ref[...].astype(BF16)
    fi = fi_ref[...].astype(BF16)
    for j in range(FFT_G):
        v = jnp.concatenate([ar_ref[j], ai_ref[j]], axis=0).astype(BF16)
        x = jnp.dot(f, v, preferred_element_type=F32)
        xr, xi = x[:FFT_B], x[FFT_B:]
        hr, hi = h_ref[j, :FFT_B, :], h_ref[j, FFT_B:, :]
        p = jnp.concatenate([xr * hr - xi * hi, xr * hi + xi * hr], axis=0).astype(BF16)
        q = jnp.dot(fi, p, preferred_element_type=F32)
        qr, qi = q[:FFT_B], q[FFT_B:]
        tc, ts = tc_ref[j], ts_ref[j]
        o_ref[:, j, :] = jnp.concatenate([qr * tc - qi * ts, qi * tc + qr * ts], axis=0)


def _fft_stage2_spectrum(a4, tabs, na):
    ch = a4.shape[-1]
    ng = na // FFT_G
    return pl.pallas_call(
        _fft2_spec_body,
        grid=(ng,),
        in_specs=[
            pl.BlockSpec((None, FFT_G, FFT_B, ch), lambda c: (0, c, 0, 0)),
            pl.BlockSpec((None, FFT_G, FFT_B, ch), lambda c: (0, ng + c, 0, 0)),
            pl.BlockSpec((2 * FFT_B, 2 * FFT_B), lambda c: (0, 0)),
        ],
        out_specs=pl.BlockSpec((FFT_G, 2 * FFT_B, ch), lambda c: (c, 0, 0)),
        out_shape=jax.ShapeDtypeStruct((na, 2 * FFT_B, ch), F32),
        compiler_params=_cparams(("parallel",)),
        name="fft_stage2_spectrum",
    )(a4, a4, tabs["f2"])


def _fft_stage2_conv(a4, hspec, tabs, na):
    bz, _, _, ch = a4.shape
    ng = na // FFT_G
    return pl.pallas_call(
        _fft2_conv_body,
        grid=(bz, ng),
        in_specs=[
            pl.BlockSpec((None, FFT_G, FFT_B, ch), lambda z, c: (z, c, 0, 0)),
            pl.BlockSpec((None, FFT_G, FFT_B, ch), lambda z, c: (z, ng + c, 0, 0)),
            pl.BlockSpec((FFT_G, 2 * FFT_B, ch), lambda z, c: (c, 0, 0)),
            pl.BlockSpec((2 * FFT_B, 2 * FFT_B), lambda z, c: (0, 0)),
            pl.BlockSpec((2 * FFT_B, 2 * FFT_B), lambda z, c: (0, 0)),
            pl.BlockSpec((FFT_G, FFT_B, 1), lambda z, c: (c, 0, 0)),
            pl.BlockSpec((FFT_G, FFT_B, 1), lambda z, c: (c, 0, 0)),
        ],
        out_specs=pl.BlockSpec((None, 2 * FFT_B, FFT_G, ch), lambda z, c: (z, 0, c, 0)),
        out_shape=jax.ShapeDtypeStruct((bz, 2 * FFT_B, na, ch), F32),
        compiler_params=_cparams(("parallel", "parallel")),
        name="fft_stage2_conv",
    )(a4, a4, hspec, tabs["f2"], tabs["f2i"], tabs["twc_c"], tabs["tws_c"])


def _fft3_body(br_ref, bi_ref, f_ref, o_ref):
    f = f_ref[...].astype(BF16)
    for j in range(FFT_G):
        v = jnp.concatenate([br_ref[j], bi_ref[j]], axis=0).astype(BF16)
        o_ref[:, j, :] = jnp.dot(f, v, preferred_element_type=F32)


def _fft_stage3(b4, tabs, na, a_lo, a_cnt):
    bz, _, _, ch = b4.shape
    ng = FFT_B // FFT_G
    return pl.pallas_call(
        _fft3_body,
        grid=(bz, ng),
        in_specs=[
            pl.BlockSpec((None, FFT_G, na, ch), lambda z, b: (z, b, 0, 0)),
            pl.BlockSpec((None, FFT_G, na, ch), lambda z, b: (z, ng + b, 0, 0)),
            pl.BlockSpec((a_cnt, 2 * na), lambda z, b: (0, 0)),
        ],
        out_specs=pl.BlockSpec((None, a_cnt, FFT_G, ch), lambda z, b: (z, 0, b, 0)),
        out_shape=jax.ShapeDtypeStruct((bz, a_cnt, FFT_B, ch), F32),
        compiler_params=_cparams(("parallel", "parallel")),
        name="fft_stage3",
    )(b4, b4, tabs["f3"][a_lo:a_lo + a_cnt])


def _hy_gate_body(x0_ref, y_ref, z_ref, b_ref, o_ref):
    o_ref[...] = x0_ref[...] * (y_ref[...] + b_ref[...] * z_ref[...])


def _hyena(p, bsz, seq, short_w, short_b, w1, b1, w2, b2, w3, hy_bias, tt):
    width = w3.shape[1]
    x0, z = _hyena_prep(p, short_w, short_b, bsz, seq, tt)
    hu, hs = _hyena_filter(seq, w1, b1, w2, b2, w3)
    na = max(2 * seq // FFT_B, 16)
    ka = max(seq // FFT_B, 16)
    tabs = _dft_tables(na)
    pad = ka * FFT_B - seq

    def rows4(a, lead):
        if pad:
            a = jnp.pad(a, ((0, 0), (0, pad), (0, 0)))
        return a.reshape(lead, ka, FFT_B, width)

    ones = jnp.ones((1, width), F32)
    hspec = _fft_stage2_spectrum(_fft_stage1(rows4(hu[None], 1), 1.0 / hs, tabs, na), tabs, na)
    a4 = _fft_stage1(rows4(z, bsz), ones, tabs, na)
    b4 = _fft_stage2_conv(a4, hspec, tabs, na)
    if pad:
        y = _fft_stage3(b4, tabs, na, 0, na).reshape(bsz, na * FFT_B, width)[:, seq // 2: seq // 2 + seq]
    else:
        y = _fft_stage3(b4, tabs, na, seq // 2 // FFT_B, ka).reshape(bsz, seq, width)
    m = bsz * seq
    return _rows_call(_hy_gate_body, [x0.reshape(m, width), y.reshape(m, width), z.reshape(m, width)],
                      [hy_bias.reshape(1, -1)], [], [((m, width), F32)], tt, seq, "hyena_gate")[0]


def _s5_tables(lam_re, lam_im, log_dt, b_re, b_im, c_re, c_im, nsteps):
    t_len, hdim = S5_CHUNK, S5_GROUP
    lam = lax.complex(jnp.minimum(lam_re.astype(F32), -1e-4), lam_im.astype(F32))
    dt = jnp.exp(log_dt.astype(F32))[..., None]
    lam_dt = lam * dt
    lam_bar = jnp.exp(lam_dt)
    b_bar = ((lam_bar - 1.0) / lam)[..., None] * lax.complex(b_re.astype(F32), b_im.astype(F32))
    cm = lax.complex(c_re.astype(F32), c_im.astype(F32))
    ks = jnp.arange(t_len + 1, dtype=F32)
    pw = jnp.exp(ks[:, None, None, None] * lam_dt[None])
    g = lam.shape[1]
    tabs = {}
    pin = [pw[:t_len, 0][::-1], pw[:t_len, 1]]
    pout = [pw[1:, 0], pw[1:, 1][::-1]]
    for d in range(2):
        win = pin[d][:, :, :, None] * b_bar[d][None]
        win = jnp.transpose(win, (1, 0, 3, 2)).reshape(g, t_len * hdim, -1)
        tabs[f"win{d}"] = jnp.concatenate([jnp.real(win), jnp.imag(win)], axis=-1)
        wout = pout[d][:, :, None, :] * cm[d][None]
        wout = jnp.transpose(wout, (1, 3, 0, 2)).reshape(g, -1, t_len * hdim)
        tabs[f"wout{d}"] = jnp.concatenate([jnp.real(wout), -jnp.imag(wout)], axis=1)
        mu = jnp.exp((t_len * 2.0 ** jnp.arange(nsteps, dtype=F32))[:, None, None] * lam_dt[d][None])
        mr, mi = jnp.real(mu), jnp.imag(mu)
        tabs[f"m1{d}"] = jnp.concatenate([mr, mr], axis=-1)[:, :, None, :]
        tabs[f"m2{d}"] = jnp.concatenate([-mi, mi], axis=-1)[:, :, None, :]
        mu1 = jnp.exp(t_len * lam_dt[d])
        tabs[f"mu1{d}"] = jnp.concatenate([jnp.real(mu1), jnp.real(mu1)], axis=-1)[:, None, :]
        tabs[f"mu2{d}"] = jnp.concatenate([-jnp.imag(mu1), jnp.imag(mu1)], axis=-1)[:, None, :]
    kern = [jnp.real(jnp.einsum("ghp,tgp,gpk->tghk", cm[d], pw[:t_len, d], b_bar[d], precision=HI)) for d in range(2)]
    s_idx = jnp.arange(t_len)[:, None]
    t_idx = jnp.arange(t_len)[None, :]
    fwd = jnp.where((t_idx >= s_idx)[:, :, None, None, None], kern[0][jnp.maximum(t_idx - s_idx, 0)], 0.0)
    bwd = jnp.where((s_idx >= t_idx)[:, :, None, None, None], kern[1][jnp.maximum(s_idx - t_idx, 0)], 0.0)
    d0 = fwd + bwd
    tabs["d0"] = jnp.transpose(d0, (2, 0, 4, 1, 3)).reshape(g, t_len * hdim, t_len * hdim)
    return tabs


def _s5_body(u_ref, d0_ref, win0_ref, win1_ref, wout0_ref, wout1_ref, m10_ref, m20_ref, m11_ref, m21_ref,
             mu10_ref, mu20_ref, mu11_ref, mu21_ref, s0_ref, s1_ref, y_ref, f0_ref, f1_ref, *, bsz, nc, nsteps):
    rows = bsz * nc
    half = S5_STATE
    u = u_ref[...].astype(BF16)
    ridx = lax.broadcasted_iota(jnp.int32, (rows, 1), 0)
    j = ridx % nc

    def swap(a):
        return pltpu.roll(a, half, axis=1)

    def rows_of(init_ref):
        r = jnp.broadcast_to(init_ref[0:1, :], (rows, 2 * half))
        for b in range(1, bsz):
            r = jnp.where(ridx >= b * nc, init_ref[b:b + 1, :], r)
        return r

    e0 = jnp.dot(u, win0_ref[...].astype(BF16), preferred_element_type=F32)
    e1 = jnp.dot(u, win1_ref[...].astype(BF16), preferred_element_type=F32)
    s_f = jnp.where(j == 0, rows_of(s0_ref), pltpu.roll(e0, 1, axis=0))
    s_b = jnp.where(j == nc - 1, rows_of(s1_ref), pltpu.roll(e1, rows - 1, axis=0))
    for k in range(nsteps):
        step = 2 ** k
        sh = jnp.where(j >= step, pltpu.roll(s_f, step, axis=0), 0.0)
        s_f = s_f + m10_ref[k] * sh + m20_ref[k] * swap(sh)
        sh = jnp.where(j < nc - step, pltpu.roll(s_b, rows - step, axis=0), 0.0)
        s_b = s_b + m11_ref[k] * sh + m21_ref[k] * swap(sh)
    y = jnp.dot(u, d0_ref[...].astype(BF16), preferred_element_type=F32)
    y = y + jnp.dot(s_f.astype(BF16), wout0_ref[...].astype(BF16), preferred_element_type=F32)
    y = y + jnp.dot(s_b.astype(BF16), wout1_ref[...].astype(BF16), preferred_element_type=F32)
    y_ref[...] = y
    t_f = mu10_ref[...] * s_f + mu20_ref[...] * swap(s_f) + e0
    t_b = mu11_ref[...] * s_b + mu21_ref[...] * swap(s_b) + e1
    for b in range(bsz):
        f0_ref[b:b + 1, :] = t_f[(b + 1) * nc - 1:(b + 1) * nc, :]
        f1_ref[b:b + 1, :] = t_b[b * nc:b * nc + 1, :]


def _s5_core(u2, bsz, seq, tabs, init0, init1):
    width = u2.shape[1]
    g = width // S5_GROUP
    t_len = S5_CHUNK
    nc = seq // t_len
    rows = bsz * nc
    cols = t_len * S5_GROUP
    nsteps = max(1, math.ceil(math.log2(nc)))
    assert nsteps <= tabs["m10"].shape[0]
    ug = u2.reshape(bsz, nc, t_len, g, S5_GROUP).transpose(3, 0, 1, 2, 4).reshape(g, rows, cols)

    def per_group(shape):
        return pl.BlockSpec((None,) + shape, lambda i: (i,) + (0,) * len(shape))

    def per_group_steps(width2):
        return pl.BlockSpec((nsteps, None, 1, width2), lambda i: (0, i, 0, 0))

    p2 = 2 * S5_STATE
    in_specs = [per_group((rows, cols)), per_group((cols, cols)), per_group((cols, p2)), per_group((cols, p2)),
                per_group((p2, cols)), per_group((p2, cols))] + [per_group_steps(p2)] * 4 \
        + [per_group((1, p2))] * 4 + [per_group((bsz, p2))] * 2
    yg, f0, f1 = pl.pallas_call(
        functools.partial(_s5_body, bsz=bsz, nc=nc, nsteps=nsteps),
        grid=(g,),
        in_specs=in_specs,
        out_specs=[per_group((rows, cols)), per_group((bsz, p2)), per_group((bsz, p2))],
        out_shape=[jax.ShapeDtypeStruct((g, rows, cols), F32), jax.ShapeDtypeStruct((g, bsz, p2), F32),
                   jax.ShapeDtypeStruct((g, bsz, p2), F32)],
        compiler_params=_cparams(("parallel",)),
        name="s5_scan",
    )(ug, tabs["d0"], tabs["win0"], tabs["win1"], tabs["wout0"], tabs["wout1"],
      tabs["m10"][:nsteps], tabs["m20"][:nsteps], tabs["m11"][:nsteps], tabs["m21"][:nsteps],
      tabs["mu10"], tabs["mu20"], tabs["mu11"], tabs["mu21"], init0, init1)
    y = yg.reshape(g, bsz, nc, t_len, S5_GROUP).transpose(1, 2, 3, 0, 4).reshape(bsz * seq, width)
    return y, f0, f1


def _s5_glu_body(u_ref, y_ref, d_ref, w_ref, b_ref, o_ref):
    y = d_ref[...] * u_ref[...] + y_ref[...]
    g = jax.nn.gelu(y)
    o_ref[...] = g * jax.nn.sigmoid(jnp.dot(g.astype(BF16), w_ref[...], preferred_element_type=F32) + b_ref[...])


def _s5_glu(u2, y2, d_skip, glu_w, glu_b, tm, name):
    m, width = u2.shape
    return _rows_call(_s5_glu_body, [u2, y2], [d_skip.reshape(1, -1), glu_w.astype(BF16), glu_b.reshape(1, -1)], [],
                      [((m, width), F32)], tm, m, name)[0]


def _s5_mixer(u_lat, u_ctx, bsz, seq, seq_c, lam_re, lam_im, log_dt, b_re, b_im, c_re, c_im, d_skip, glu_w, glu_b,
              tm, tm_c):
    nsteps = max(1, math.ceil(math.log2(seq // S5_CHUNK)))
    tabs = _s5_tables(lam_re, lam_im, log_dt, b_re, b_im, c_re, c_im, nsteps)
    g = u_lat.shape[1] // S5_GROUP
    zero = jnp.zeros((g, bsz, 2 * S5_STATE), F32)
    y_ctx, f0, f1 = _s5_core(u_ctx, bsz, seq_c, tabs, zero, zero)
    y_lat, _, _ = _s5_core(u_lat, bsz, seq, tabs, f0, f1)
    out_lat = _s5_glu(u_lat, y_lat, d_skip, glu_w, glu_b, tm, "s5_glu")
    out_ctx = _s5_glu(u_ctx, y_ctx, d_skip, glu_w, glu_b, tm_c, "s5_glu_ctx")
    return out_lat, out_ctx


def _rope_body(*refs, shift, heads, scale, rope):
    if rope:
        x_ref, cos_ref, sin_ref, o_ref = refs
    else:
        x_ref, o_ref = refs
    x = x_ref[...]
    w = x.shape[1]
    if rope:
        lane = lax.broadcasted_iota(jnp.int32, x.shape, 1)
        partner = jnp.where(lane % (2 * shift) < shift, pltpu.roll(x, w - shift, axis=1), pltpu.roll(x, shift, axis=1))
        x = x * cos_ref[...] + partner * sin_ref[...]
    if scale != 1.0:
        x = x * scale
    hd = w // heads
    for h in range(heads):
        o_ref[h] = x[:, h * hd:(h + 1) * hd]


def _rope_heads(x2, tables, shift, heads, scale, tm, rows_per_batch, name):
    m, w = x2.shape
    body = functools.partial(_rope_body, shift=shift, heads=heads, scale=scale, rope=tables is not None)
    return _rows_call(body, [x2], [], [], [((heads, m, w // heads), F32)], tm, rows_per_batch, name,
                      periodic=list(tables) if tables is not None else [])[0]


def _rope_tables_1d(seq, hd, heads):
    half = hd // 2
    inv = ROPE_BASE ** (-jnp.arange(half, dtype=F32) / half)
    ang = jnp.arange(seq, dtype=F32)[:, None] * inv[None]
    cos, sin = jnp.cos(ang), jnp.sin(ang)
    return jnp.tile(jnp.concatenate([cos, cos], -1), (1, heads)), jnp.tile(jnp.concatenate([-sin, sin], -1), (1, heads))


def _rope_tables_2d(seq, hd, heads):
    q = hd // 4
    inv = ROPE_BASE ** (-jnp.arange(q, dtype=F32) / q)
    n_rows = seq // GRID_W
    row = jnp.repeat(jnp.arange(n_rows, dtype=F32), GRID_W)
    col = jnp.tile(jnp.arange(GRID_W, dtype=F32), n_rows)
    ar, ac = row[:, None] * inv[None], col[:, None] * inv[None]
    cos = jnp.concatenate([jnp.cos(ar), jnp.cos(ar), jnp.cos(ac), jnp.cos(ac)], -1)
    sin = jnp.concatenate([-jnp.sin(ar), jnp.sin(ar), -jnp.sin(ac), jnp.sin(ac)], -1)
    return jnp.tile(cos, (1, heads)), jnp.tile(sin, (1, heads))


def _kv_update(k, kdec, v):
    kd = (k * kdec).astype(BF16)
    return lax.dot_general(kd, v.astype(BF16), (((0,), (0,)), ((), ())), preferred_element_type=F32)


def _ret_bwd_body(k_ref, v_ref, kdec_ref, gc_ref, init_ref, sprev_ref, fin_ref, s_ref):
    @pl.when(pl.program_id(1) == 0)
    def _():
        s_ref[...] = init_ref[...]

    for h in range(k_ref.shape[0]):
        s = s_ref[h]
        sprev_ref[h] = s
        s = gc_ref[h] * s + _kv_update(k_ref[h], kdec_ref[h], v_ref[h])
        s_ref[h] = s
        fin_ref[h] = s


def _ret_main_body(q_ref, k_ref, v_ref, g_ref, mask_ref, qdf_ref, kdf_ref, qdb_ref, gc_ref, init_ref, sb_ref,
                   o_ref, fin_ref, s_ref):
    @pl.when(pl.program_id(1) == 0)
    def _():
        s_ref[...] = init_ref[...]

    for h in range(q_ref.shape[0]):
        q, k, v = q_ref[h], k_ref[h], v_ref[h]
        vb = v.astype(BF16)
        sc = lax.dot_general(q.astype(BF16), k.astype(BF16), (((1,), (1,)), ((), ())), preferred_element_type=F32)
        o = jnp.dot((sc * mask_ref[h]).astype(BF16), vb, preferred_element_type=F32)
        s = s_ref[h]
        o = o + jnp.dot((q * qdf_ref[h]).astype(BF16), s.astype(BF16), preferred_element_type=F32)
        o = o + jnp.dot((q * qdb_ref[h]).astype(BF16), sb_ref[h].astype(BF16), preferred_element_type=F32)
        o = o * lax.rsqrt(jnp.mean(o * o, axis=-1, keepdims=True) + EPS)
        o_ref[h] = o * jax.nn.silu(g_ref[h])
        s = gc_ref[h] * s + _kv_update(k, kdf_ref[h], v)
        s_ref[h] = s
        fin_ref[h] = s


def _ret_tables(decay_logit):
    cl = RET_CHUNK
    log_g = jax.nn.log_sigmoid(decay_logit.astype(F32))
    idx = jnp.arange(cl, dtype=F32)
    diff = idx[:, None] - idx[None, :]
    mf = jnp.where(diff[None] >= 0, jnp.exp(jnp.maximum(diff, 0.0)[None] * log_g[0][:, None, None]), 0.0)
    mb = jnp.where(diff[None] <= 0, jnp.exp(jnp.maximum(-diff, 0.0)[None] * log_g[1][:, None, None]), 0.0)

    def col(e, d):
        return jnp.exp(e[None, :] * log_g[d][:, None])[:, :, None]

    return dict(mask=mf + mb, qdf=col(idx + 1.0, 0), kdf=col(cl - 1.0 - idx, 0), qdb=col(cl - idx, 1), kdb=col(idx, 1),
                gcf=jnp.exp(cl * log_g[0])[:, None, None], gcb=jnp.exp(cl * log_g[1])[:, None, None])


def _retention(q, k, v, g, bsz, seq, tabs, init_f, init_b):
    heads, _, dk = q.shape
    dv = v.shape[2]
    cl = RET_CHUNK
    n = seq // cl

    def whole(a):
        return pl.BlockSpec(a.shape, lambda b, i, nd=a.ndim: (0,) * nd)

    state_spec = pl.BlockSpec((None, heads, dk, dv), lambda b, i: (b, 0, 0, 0))
    state_shape = jax.ShapeDtypeStruct((bsz, heads, dk, dv), F32)
    rev = lambda b, i: (0, b * n + (n - 1 - i), 0)
    sprev_b, fin_b = pl.pallas_call(
        _ret_bwd_body,
        grid=(bsz, n),
        in_specs=[pl.BlockSpec((heads, cl, dk), rev), pl.BlockSpec((heads, cl, dv), rev),
                  whole(tabs["kdb"]), whole(tabs["gcb"]), state_spec],
        out_specs=[pl.BlockSpec((None, None, heads, dk, dv), lambda b, i: (b, n - 1 - i, 0, 0, 0)), state_spec],
        out_shape=[jax.ShapeDtypeStruct((bsz, n, heads, dk, dv), F32), state_shape],
        scratch_shapes=[pltpu.VMEM((heads, dk, dv), F32)],
        compiler_params=_cparams(("parallel", "arbitrary")),
        name="retention_backward_states",
    )(k, v, tabs["kdb"], tabs["gcb"], init_b)
    fwd = lambda b, i: (0, b * n + i, 0)
    o, fin_f = pl.pallas_call(
        _ret_main_body,
        grid=(bsz, n),
        in_specs=[pl.BlockSpec((heads, cl, dk), fwd), pl.BlockSpec((heads, cl, dk), fwd),
                  pl.BlockSpec((heads, cl, dv), fwd), pl.BlockSpec((heads, cl, dv), fwd),
                  whole(tabs["mask"]), whole(tabs["qdf"]), whole(tabs["kdf"]), whole(tabs["qdb"]), whole(tabs["gcf"]),
                  state_spec, pl.BlockSpec((None, None, heads, dk, dv), lambda b, i: (b, i, 0, 0, 0))],
        out_specs=[pl.BlockSpec((heads, cl, dv), fwd), state_spec],
        out_shape=[jax.ShapeDtypeStruct((heads, bsz * seq, dv), F32), state_shape],
        scratch_shapes=[pltpu.VMEM((heads, dk, dv), F32)],
        compiler_params=_cparams(("parallel", "arbitrary")),
        name="retention",
    )(q, k, v, g, tabs["mask"], tabs["qdf"], tabs["kdf"], tabs["qdb"], tabs["gcf"], init_f, sprev_b)
    return o, fin_f, fin_b


def _swa_body(*refs, local, seq, scale):
    if local:
        q_ref, kp_ref, kc_ref, kn_ref, vp_ref, vc_ref, vn_ref, kx_ref, vx_ref, sink_ref, o_ref = refs
    else:
        q_ref, kx_ref, vx_ref, sink_ref, o_ref = refs
    i = pl.program_id(1)
    bk = q_ref.shape[1]
    nkv = kx_ref.shape[0]
    grp = q_ref.shape[0] // nkv
    nt = (((1,), (1,)), ((), ()))
    if local:
        row = lax.broadcasted_iota(jnp.int32, (bk, 3 * bk), 0)
        col = lax.broadcasted_iota(jnp.int32, (bk, 3 * bk), 1)
        key_pos = (i - 1) * bk + col
        valid = (jnp.abs(col - (row + bk)) <= SWA_WINDOW) & (key_pos >= 0) & (key_pos < seq)
    for kv in range(nkv):
        kx = kx_ref[kv].astype(BF16)
        vx = vx_ref[kv].astype(BF16)
        if local:
            kl = jnp.concatenate([kp_ref[kv], kc_ref[kv], kn_ref[kv]], axis=0).astype(BF16)
            vl = jnp.concatenate([vp_ref[kv], vc_ref[kv], vn_ref[kv]], axis=0).astype(BF16)
        for gi in range(grp):
            h = kv * grp + gi
            q = q_ref[h].astype(BF16)
            sink = sink_ref[h]
            s_x = lax.dot_general(q, kx, nt, preferred_element_type=F32) * scale
            m = jnp.maximum(jnp.max(s_x, axis=-1, keepdims=True), sink)
            if local:
                s_l = lax.dot_general(q, kl, nt, preferred_element_type=F32) * scale
                s_l = jnp.where(valid, s_l, NEG_INF)
                m = jnp.maximum(m, jnp.max(s_l, axis=-1, keepdims=True))
            p_x = jnp.exp(s_x - m)
            den = jnp.sum(p_x, axis=-1, keepdims=True) + jnp.exp(sink - m)
            o = jnp.dot(p_x.astype(BF16), vx, preferred_element_type=F32)
            if local:
                p_l = jnp.exp(s_l - m)
                den = den + jnp.sum(p_l, axis=-1, keepdims=True)
                o = o + jnp.dot(p_l.astype(BF16), vl, preferred_element_type=F32)
            o_ref[h] = o / den


def _swa(q, k, v, kx, vx, sink, bsz, seq, seq_c, local):
    hq, m, hd = q.shape
    hkv = kx.shape[0]
    bk = SWA_BLOCK
    nb = seq // bk
    scale = hd ** -0.5
    cur = lambda b, i: (0, b * nb + i, 0)
    prv = lambda b, i: (0, b * nb + jnp.maximum(i - 1, 0), 0)
    nxt = lambda b, i: (0, b * nb + jnp.minimum(i + 1, nb - 1), 0)
    ctx_spec = pl.BlockSpec((hkv, seq_c, hd), lambda b, i: (0, b, 0))
    sink_spec = pl.BlockSpec((hq, 1, 1), lambda b, i: (0, 0, 0))
    kvb = lambda f: pl.BlockSpec((hkv, bk, hd), f)
    in_specs = [pl.BlockSpec((hq, bk, hd), cur)]
    args = [q]
    if local:
        in_specs += [kvb(prv), kvb(cur), kvb(nxt), kvb(prv), kvb(cur), kvb(nxt)]
        args += [k, k, k, v, v, v]
    in_specs += [ctx_spec, ctx_spec, sink_spec]
    args += [kx, vx, sink.astype(F32).reshape(hq, 1, 1)]
    return pl.pallas_call(
        functools.partial(_swa_body, local=local, seq=seq, scale=scale),
        grid=(bsz, nb),
        in_specs=in_specs,
        out_specs=pl.BlockSpec((hq, bk, hd), cur),
        out_shape=jax.ShapeDtypeStruct((hq, m, hd), F32),
        compiler_params=_cparams(("parallel", "parallel")),
        name="swa" if local else "swa_ctx",
    )(*args)


def _mixer_cd_core(lat, cx, bsz, seq, seq_c, decay_logit, sink, with_ctx, tm, tm_c):
    dk = lat["rq"].shape[1] // RET_HEADS
    hd = lat["sq"].shape[1] // SWA_Q_HEADS
    t1 = _rope_tables_1d(seq, dk, RET_HEADS)
    rq = _rope_heads(lat["rq"], t1, dk // 2, RET_HEADS, dk ** -0.5, tm, seq, "ret_rope_q")
    rk = _rope_heads(lat["rk"], t1, dk // 2, RET_HEADS, 1.0, tm, seq, "ret_rope_k")
    rqc = _rope_heads(cx["rq"], None, 0, RET_HEADS, dk ** -0.5, tm_c, seq_c, "ret_heads_q_ctx")
    rkc = _rope_heads(cx["rk"], None, 0, RET_HEADS, 1.0, tm_c, seq_c, "ret_heads_k_ctx")
    tabs = _ret_tables(decay_logit)
    dv = lat["rv"].shape[2]
    zero = jnp.zeros((bsz, RET_HEADS, dk, dv), F32)
    ro_c, fin_f, fin_b = _retention(rqc, rkc, cx["rv"], cx["rg"], bsz, seq_c, tabs, zero, zero)
    ro_l, _, _ = _retention(rq, rk, lat["rv"], lat["rg"], bsz, seq, tabs, fin_f, fin_b)
    t2 = _rope_tables_2d(seq, hd, 1)
    sq = _rope_heads(lat["sq"], [jnp.tile(t, (1, SWA_Q_HEADS)) for t in t2], hd // 4, SWA_Q_HEADS, 1.0, tm, seq,
                     "swa_rope_q")
    sk = _rope_heads(lat["sk"], [jnp.tile(t, (1, SWA_KV_HEADS)) for t in t2], hd // 4, SWA_KV_HEADS, 1.0, tm, seq,
                     "swa_rope_k")
    skc = _rope_heads(cx["sk"], None, 0, SWA_KV_HEADS, 1.0, tm_c, seq_c, "swa_heads_k_ctx")
    so_l = _swa(sq, sk, lat["sv"], skc, cx["sv"], sink, bsz, seq, seq_c, True)
    so_c = None
    if with_ctx:
        sqc = _rope_heads(cx["sq"], None, 0, SWA_Q_HEADS, 1.0, tm_c, seq_c, "swa_heads_q_ctx")
        so_c = _swa(sqc, None, None, skc, cx["sv"], sink, bsz, seq_c, seq_c, False)
    return ro_l, so_l, (ro_c if with_ctx else None), so_c


def _router_body(x_ref, g_ref, rw_ref, rb_ref, tril_ref, sc_ref, sh_ref, hx_ref, ti_ref, gt_ref, pos_ref, cnt_ref):
    @pl.when(pl.program_id(0) == 0)
    def _():
        cnt_ref[...] = jnp.zeros_like(cnt_ref)

    hx = _norm_mod(x_ref[...], g_ref[...], sc_ref[...], sh_ref[...])
    hx_ref[...] = hx
    logits = jnp.dot(hx, rw_ref[...], precision=HI, preferred_element_type=F32) + rb_ref[...]
    lane = lax.broadcasted_iota(jnp.int32, logits.shape, 1).astype(F32)
    rem = logits
    vals, hots = [], []
    for k in range(TOP_K):
        m = jnp.max(rem, axis=-1, keepdims=True)
        idx = jnp.min(jnp.where(rem == m, lane, float(N_EXPERTS)), axis=-1, keepdims=True)
        hot = lane == idx
        rem = jnp.where(hot, NEG_INF, rem)
        vals.append(m)
        hots.append(hot.astype(F32))
        ti_ref[:, k:k + 1] = idx.astype(jnp.int32)
    exps = [jnp.exp(v - vals[0]) for v in vals]
    den = exps[0] + exps[1] + exps[2] + exps[3]
    for k in range(TOP_K):
        gt_ref[:, k:k + 1] = exps[k] / den
    sel = hots[0] + hots[1] + hots[2] + hots[3]
    before = jnp.dot(tril_ref[...], sel.astype(BF16), preferred_element_type=F32) + cnt_ref[...]
    for k in range(TOP_K):
        pos_ref[:, k:k + 1] = jnp.sum(hots[k] * before, axis=-1, keepdims=True).astype(jnp.int32)
    cnt_ref[...] += jnp.sum(sel, axis=0, keepdims=True)


def _moe_route(x2, g, sc, sh, router_w, router_b, tr, rows_per_batch):
    n, d = x2.shape
    tiles_per_batch = rows_per_batch // tr
    tril = jnp.asarray(np.tril(np.ones((tr, tr), np.float32), -1)).astype(BF16)
    whole = lambda a: pl.BlockSpec(a.shape, lambda i, nd=a.ndim: (0,) * nd)
    bat = pl.BlockSpec((None, 1, d), lambda i: (i // tiles_per_batch, 0, 0))
    g2, rb2 = g.reshape(1, -1), router_b.reshape(1, -1)
    small = lambda dt: jax.ShapeDtypeStruct((n, TOP_K), dt)
    small_spec = pl.BlockSpec((tr, TOP_K), lambda i: (i, 0))
    return pl.pallas_call(
        _router_body,
        grid=(n // tr,),
        in_specs=[pl.BlockSpec((tr, d), lambda i: (i, 0)), whole(g2), whole(router_w), whole(rb2), whole(tril), bat, bat],
        out_specs=[pl.BlockSpec((tr, d), lambda i: (i, 0)), small_spec, small_spec, small_spec,
                   pl.BlockSpec((1, N_EXPERTS), lambda i: (0, 0))],
        out_shape=[jax.ShapeDtypeStruct((n, d), F32), small(jnp.int32), small(F32), small(jnp.int32),
                   jax.ShapeDtypeStruct((1, N_EXPERTS), F32)],
        compiler_params=_cparams(("arbitrary",)),
        name="moe_router",
    )(x2, g2, router_w, rb2, tril, sc, sh)


def _row_copy(src, s_row, dst, d_row, sem):
    return pltpu.make_async_copy(src.at[pl.ds(s_row, 1)], dst.at[pl.ds(d_row, 1)], sem)


def _dispatch_body(dest_ref, hx_ref, xs_in_ref, xs_ref, sem, *, td):
    del xs_in_ref

    def issue(n, carry):
        for k in range(TOP_K):
            _row_copy(hx_ref, n, xs_ref, dest_ref[n * TOP_K + k], sem).start()
        return carry

    def drain(n, carry):
        for k in range(TOP_K):
            _row_copy(hx_ref, 0, xs_ref, 0, sem).wait()
        return carry

    lax.fori_loop(0, td, issue, 0)
    lax.fori_loop(0, td, drain, 0)


def _moe_dispatch(hx, dest_flat, n_slots, td):
    n, d = hx.shape
    zeros = jnp.zeros((n_slots, d), F32)
    return pl.pallas_call(
        functools.partial(_dispatch_body, td=td),
        grid=(n // td,),
        in_specs=[pl.BlockSpec((td * TOP_K,), lambda i: (i,), memory_space=pltpu.SMEM),
                  pl.BlockSpec((td, d), lambda i: (i, 0)), pl.BlockSpec(memory_space=pl.ANY)],
        out_specs=pl.BlockSpec(memory_space=pl.ANY),
        out_shape=jax.ShapeDtypeStruct((n_slots, d), F32),
        scratch_shapes=[pltpu.SemaphoreType.DMA(())],
        input_output_aliases={2: 0},
        compiler_params=pltpu.CompilerParams(dimension_semantics=("arbitrary",), has_side_effects=True,
                                             vmem_limit_bytes=VMEM_LIMIT),
        name="moe_dispatch",
    )(dest_flat, hx, zeros)


def _ffn_body(be_ref, nu_ref, x_ref, wgu_ref, bgu_ref, wdn_ref, bdn_ref, o_ref, wgu_bf, wdn_bf):
    j = pl.program_id(0)
    e = be_ref[j]
    prev = be_ref[jnp.maximum(j - 1, 0)]

    @pl.when((j == 0) | (e != prev))
    def _():
        wgu_bf[...] = wgu_ref[...].astype(BF16)
        wdn_bf[...] = wdn_ref[...].astype(BF16)

    @pl.when(j < nu_ref[0])
    def _():
        f = wdn_ref.shape[0]
        gu = jnp.dot(x_ref[...].astype(BF16), wgu_bf[...], preferred_element_type=F32) + bgu_ref[...]
        gate = jnp.minimum(gu[:, :f], SWIGLU_LIMIT)
        up = jnp.clip(gu[:, f:], -SWIGLU_LIMIT, SWIGLU_LIMIT)
        act = gate * jax.nn.sigmoid(SWIGLU_ALPHA * gate) * (up + 1.0)
        o_ref[...] = jnp.dot(act.astype(BF16), wdn_bf[...], preferred_element_type=F32) + bdn_ref[...]

    @pl.when(j >= nu_ref[0])
    def _():
        o_ref[...] = jnp.zeros_like(o_ref)


def _moe_ffn(xs, block_exp, n_used, layer, w_gu, b_gu, w_dn, b_dn, tm):
    n_slots, d = xs.shape
    depth, n_exp, _, f2 = w_gu.shape
    f = w_dn.shape[2]
    blk = lambda j, be, nu: (jnp.minimum(j, nu[0] - 1), 0)
    exp4 = lambda j, be, nu: (layer, be[j], 0, 0)
    grid_spec = pltpu.PrefetchScalarGridSpec(
        num_scalar_prefetch=2,
        grid=(n_slots // tm,),
        in_specs=[pl.BlockSpec((tm, d), blk), pl.BlockSpec((None, None, d, f2), exp4),
                  pl.BlockSpec((None, None, 1, f2), exp4), pl.BlockSpec((None, None, f, d), exp4),
                  pl.BlockSpec((None, None, 1, d), exp4)],
        out_specs=pl.BlockSpec((tm, d), lambda j, be, nu: (j, 0)),
        scratch_shapes=[pltpu.VMEM((d, f2), BF16), pltpu.VMEM((f, d), BF16)],
    )
    return pl.pallas_call(
        _ffn_body,
        grid_spec=grid_spec,
        out_shape=jax.ShapeDtypeStruct((n_slots, d), F32),
        compiler_params=_cparams(("arbitrary",)),
        name="moe_ffn",
    )(block_exp, n_used, xs, w_gu, b_gu.reshape(depth, n_exp, 1, f2), w_dn, b_dn.reshape(depth, n_exp, 1, d))


def _combine_body(dest_ref, gt_ref, x_ref, ys_ref, g_ref, o_ref, buf, sem, *, tc):
    def issue(n, carry):
        for k in range(TOP_K):
            pltpu.make_async_copy(ys_ref.at[pl.ds(dest_ref[n * TOP_K + k], 1)], buf.at[k, pl.ds(n, 1)], sem).start()
        return carry

    def drain(n, carry):
        for k in range(TOP_K):
            pltpu.make_async_copy(ys_ref.at[pl.ds(0, 1)], buf.at[0, pl.ds(0, 1)], sem).wait()
        return carry

    lax.fori_loop(0, tc, issue, 0)
    lax.fori_loop(0, tc, drain, 0)
    acc = gt_ref[:, 0:1] * buf[0]
    for k in range(1, TOP_K):
        acc = acc + gt_ref[:, k:k + 1] * buf[k]
    o_ref[...] = x_ref[...] + g_ref[...] * acc


def _moe_combine(ys, dest_flat, gates, x2, gate2, tc, rows_per_batch):
    n, d = x2.shape
    tiles_per_batch = rows_per_batch // tc
    return pl.pallas_call(
        functools.partial(_combine_body, tc=tc),
        grid=(n // tc,),
        in_specs=[pl.BlockSpec((tc * TOP_K,), lambda i: (i,), memory_space=pltpu.SMEM),
                  pl.BlockSpec((tc, TOP_K), lambda i: (i, 0)), pl.BlockSpec((tc, d), lambda i: (i, 0)),
                  pl.BlockSpec(memory_space=pl.ANY),
                  pl.BlockSpec((None, 1, d), lambda i: (i // tiles_per_batch, 0, 0))],
        out_specs=pl.BlockSpec((tc, d), lambda i: (i, 0)),
        out_shape=jax.ShapeDtypeStruct((n, d), F32),
        scratch_shapes=[pltpu.VMEM((TOP_K, tc, d), F32), pltpu.SemaphoreType.DMA(())],
        compiler_params=_cparams(("arbitrary",)),
        name="moe_combine",
    )(dest_flat, gates, x2, ys, gate2)


def _moe_layer(x2, g, sc, sh, gate2, rows_per_batch, router_w, router_b, layer, w_gu, b_gu, w_dn, b_dn, tr, tc):
    n, _ = x2.shape
    tm = MOE_TM
    hx, top_i, gates, pos, counts = _moe_route(x2, g, sc, sh, router_w, router_b, tr, rows_per_batch)
    counts = counts[0].astype(jnp.int32)
    padded = (counts + tm - 1) // tm * tm
    pad_end = jnp.cumsum(padded)
    pad_start = pad_end - padded
    n_blocks = n * TOP_K // tm + N_EXPERTS
    n_used = (pad_end[-1] // tm).astype(jnp.int32)
    blk_ids = jnp.arange(n_blocks, dtype=jnp.int32)
    last_row = jnp.minimum(blk_ids, n_used - 1) * tm
    block_exp = jnp.sum((pad_end[None, :] <= last_row[:, None]).astype(jnp.int32), axis=1)
    block_exp = jnp.minimum(block_exp, N_EXPERTS - 1)
    dest = (pad_start[top_i] + pos).astype(jnp.int32).reshape(-1)
    xs = _moe_dispatch(hx, dest, n_blocks * tm, tr)
    ys = _moe_ffn(xs, block_exp, n_used.reshape(1), layer, w_gu, b_gu, w_dn, b_dn, tm)
    return _moe_combine(ys, dest, gates, x2, gate2, tc, rows_per_batch)


def _mod_body(c_ref, w_ref, b_ref, o_ref):
    o_ref[...] = jnp.dot(jax.nn.silu(c_ref[...]), w_ref[...], precision=HI, preferred_element_type=F32) + b_ref[...]


def _modulation(cc, mod_w, mod_b):
    depth, d, d6 = mod_w.shape
    r = cc.shape[0]
    return pl.pallas_call(
        _mod_body,
        grid=(depth, d6 // d),
        in_specs=[pl.BlockSpec((r, d), lambda l, j: (0, 0)), pl.BlockSpec((None, d, d), lambda l, j: (l, 0, j)),
                  pl.BlockSpec((None, 1, d), lambda l, j: (l, 0, j))],
        out_specs=pl.BlockSpec((None, r, d), lambda l, j: (l, 0, j)),
        out_shape=jax.ShapeDtypeStruct((depth, r, d6), F32),
        compiler_params=_cparams(("parallel", "parallel")),
        name="modulation",
    )(cc, mod_w, mod_b.reshape(depth, 1, d6))


def kernel(x, c, ctx, c_ctx, mod_w, mod_b, norm1_g, norm2_g, ab_w_in, ab_w_out, hy_short_w, hy_short_b, hy_f_w1, hy_f_b1, hy_f_w2, hy_f_b2, hy_f_w3, hy_bias, s5_lambda_re, s5_lambda_im, s5_log_dt, s5_b_re, s5_b_im, s5_c_re, s5_c_im, s5_d, s5_glu_w, s5_glu_b, cd_w_in, cd_w_out, ret_decay_logit, swa_sink, router_w, router_b, exp_w_gu, exp_b_gu, exp_w_down, exp_b_down, final_g):
    bsz, seq, d = x.shape
    seq_c = ctx.shape[1]
    depth = mod_w.shape[0]
    m_l, m_c = bsz * seq, bsz * seq_c
    tm, tm_c, tt = 512, 256, 256
    xl = x.reshape(m_l, d)
    xc = ctx.reshape(m_c, d)
    cc = jnp.concatenate([c, c_ctx[None], jnp.zeros((SUBLANES - bsz - 1, d), F32)], axis=0)
    mods = _modulation(cc, mod_w, mod_b)
    hy_w = hy_f_w3.shape[2]
    for layer in range(depth):
        with_ctx = layer < depth - 1
        i = layer // 2
        sh1, sc1, g1, sh2, sc2, g2 = [t[:, None, :] for t in jnp.split(mods[layer, :bsz], 6, axis=-1)]
        csh1, csc1, cg1, csh2, csc2, cg2 = [t[:, None, :] for t in jnp.split(mods[layer, bsz:bsz + 1], 6, axis=-1)]
        if layer % 2 == 0:
            w_in, w_out = ab_w_in[i], ab_w_out[i]
            splits = [(3 * hy_w, 0), (w_in.shape[1] - 3 * hy_w, 0)]
            pa, pb = _norm_mod_matmul(xl, norm1_g[layer], sc1, sh1, w_in, splits, tm, seq, "ab_in")
            pac, pbc = _norm_mod_matmul(xc, norm1_g[layer], csc1, csh1, w_in, splits, tm_c, m_c, "ab_in_ctx")
            hy = (hy_short_w[i], hy_short_b[i], hy_f_w1[i], hy_f_b1[i], hy_f_w2[i], hy_f_b2[i], hy_f_w3[i], hy_bias[i])
            ya = _hyena(pa, bsz, seq, *hy, tt)
            yb, ybc = _s5_mixer(pb, pbc, bsz, seq, seq_c, s5_lambda_re[i], s5_lambda_im[i], s5_log_dt[i], s5_b_re[i],
                                s5_b_im[i], s5_c_re[i], s5_c_im[i], s5_d[i], s5_glu_w[i], s5_glu_b[i], tm, tm_c)
            ws = [w_out[:hy_w], w_out[hy_w:]]
            xl = _out_proj([ya, yb], ws, xl, g1, tm, seq, "ab_out")
            if with_ctx:
                yac = _hyena(pac, bsz, seq_c, *hy, tt)
                xc = _out_proj([yac, ybc], ws, xc, cg1, tm_c, m_c, "ab_out_ctx")
        else:
            w_in, w_out = cd_w_in[i], cd_w_out[i]
            qk = RET_HEADS * (d // 16)
            vw = 2 * qk
            qw = SWA_Q_HEADS * (d // 16)
            kw = SWA_KV_HEADS * (d // 16)
            splits = [(qk, 0), (qk, 0), (vw, RET_HEADS), (vw, RET_HEADS), (qw, 0), (kw, 0), (kw, SWA_KV_HEADS)]
            names = ("rq", "rk", "rv", "rg", "sq", "sk", "sv")
            lat = dict(zip(names, _norm_mod_matmul(xl, norm1_g[layer], sc1, sh1, w_in, splits, tm, seq, "cd_in")))
            cx = dict(zip(names, _norm_mod_matmul(xc, norm1_g[layer], csc1, csh1, w_in, splits, tm_c, m_c, "cd_in_ctx")))
            ro_l, so_l, ro_c, so_c = _mixer_cd_core(lat, cx, bsz, seq, seq_c, ret_decay_logit[i], swa_sink[i],
                                                    with_ctx, tm, tm_c)
            ws = [w_out[:vw].reshape(RET_HEADS, vw // RET_HEADS, d), w_out[vw:].reshape(SWA_Q_HEADS, qw // SWA_Q_HEADS, d)]
            xl = _out_proj([ro_l, so_l], ws, xl, g1, tm, seq, "cd_out")
            if with_ctx:
                xc = _out_proj([ro_c, so_c], ws, xc, cg1, tm_c, m_c, "cd_out_ctx")
        moe_w = (router_w[layer], router_b[layer], layer, exp_w_gu, exp_b_gu, exp_w_down, exp_b_down)
        xl = _moe_layer(xl, norm2_g[layer], sc2, sh2, g2, seq, *moe_w, tm, tm_c)
        if with_ctx:
            xc = _moe_layer(xc, norm2_g[layer], csc2, csh2, cg2, m_c, *moe_w, tm_c, tm_c)
    out = _rows_call(_final_norm_body, [xl], [final_g.reshape(1, -1)], [], [((m_l, d), F32)], tm, m_l, "final_norm")[0]
    return out.reshape(bsz, seq, d)
```

```python
import functools
import math

import numpy as np
import jax
import jax.numpy as jnp
from jax import lax
from jax.experimental import pallas as pl
from jax.experimental.pallas import tpu as pltpu

F32 = jnp.float32
BF16 = jnp.bfloat16
HI = lax.Precision.HIGHEST

EPS = 1e-6
NEG_INF = -1e30
ROPE_BASE = 10000.0
GRID_W = 64

HY_SHORT = 3
HY_BANDS = 16
HY_SHIFT = 0.05
HY_FAST_DECAY = math.log(1e-2) / 0.3
HY_SLOW_DECAY = math.log(1e-2) / 1.5
S5_GROUP = 16
S5_STATE = 64
S5_CHUNK = 8
RET_HEADS = 4
RET_CHUNK = 128
SWA_Q_HEADS = 8
SWA_KV_HEADS = 2
SWA_WINDOW = 128
SWA_BLOCK = 128
N_EXPERTS = 32
TOP_K = 4
SWIGLU_LIMIT = 7.0
SWIGLU_ALPHA = 1.702

LANES = 128
SUBLANES = 8
VMEM_LIMIT = 52 * 2**20
FFT_B = 128
MOE_TM = 512
DMA_UNROLL = 4


def _cparams(sem):
    return pltpu.CompilerParams(dimension_semantics=sem, vmem_limit_bytes=VMEM_LIMIT)


def _rows_call(body, rows, consts, batched, outs, tm, rows_per_batch, name, periodic=()):
    m = rows[0].shape[-2]
    assert m % tm == 0 and rows_per_batch % tm == 0
    tiles_per_batch = rows_per_batch // tm

    def row_spec(shape):
        if len(shape) == 2:
            return pl.BlockSpec((tm, shape[1]), lambda i: (i, 0))
        return pl.BlockSpec((shape[0], tm, shape[2]), lambda i: (0, i, 0))

    in_specs = [row_spec(a.shape) for a in rows]
    for a in periodic:
        in_specs.append(pl.BlockSpec((tm, a.shape[1]), lambda i: (i % tiles_per_batch, 0)))
    for a in consts:
        in_specs.append(pl.BlockSpec(a.shape, lambda i, n=a.ndim: (0,) * n))
    for a in batched:
        in_specs.append(pl.BlockSpec((None, 1, a.shape[2]), lambda i: (i // tiles_per_batch, 0, 0)))
    out_specs = [row_spec(s) for s, _ in outs]
    out_shape = [jax.ShapeDtypeStruct(s, d) for s, d in outs]
    res = pl.pallas_call(
        body,
        grid=(m // tm,),
        in_specs=in_specs,
        out_specs=out_specs,
        out_shape=out_shape,
        compiler_params=_cparams(("parallel",)),
        name=name,
    )(*rows, *periodic, *consts, *batched)
    return res


def _norm_mod(x, g, sc, sh):
    y = x * lax.rsqrt(jnp.mean(x * x, axis=-1, keepdims=True) + EPS) * g
    return y * (1.0 + sc) + sh


def _norm_mod_matmul_body(x_ref, g_ref, w_ref, sc_ref, sh_ref, *o_refs, splits):
    h = _norm_mod(x_ref[...], g_ref[...], sc_ref[...], sh_ref[...])
    r = jnp.dot(h.astype(BF16), w_ref[...], preferred_element_type=F32)
    off = 0
    for o_ref, n in zip(o_refs, splits):
        if o_ref.ndim == 3:
            hd = o_ref.shape[2]
            for h_i in range(o_ref.shape[0]):
                o_ref[h_i] = r[:, off + h_i * hd: off + (h_i + 1) * hd]
        else:
            o_ref[...] = r[:, off:off + n]
        off += n


def _norm_mod_matmul(x2, g, sc, sh, w, splits, tm, rows_per_batch, name):
    m = x2.shape[0]
    outs = []
    for n, heads in splits:
        outs.append(((heads, m, n // heads), F32) if heads else ((m, n), F32))
    splits = [n for n, _ in splits]
    body = functools.partial(_norm_mod_matmul_body, splits=splits)
    return _rows_call(body, [x2], [g.reshape(1, -1), w.astype(BF16)], [sc, sh], outs, tm, rows_per_batch, name)


def _out_proj_body(*refs, n_in):
    a_refs = refs[:n_in]
    x_ref = refs[n_in]
    w_refs = refs[n_in + 1: 2 * n_in + 1]
    g_ref = refs[2 * n_in + 1]
    o_ref = refs[2 * n_in + 2]
    acc = None
    for a_ref, w_ref in zip(a_refs, w_refs):
        if a_ref.ndim == 3:
            for h_i in range(a_ref.shape[0]):
                t = jnp.dot(a_ref[h_i].astype(BF16), w_ref[h_i], preferred_element_type=F32)
                acc = t if acc is None else acc + t
        else:
            t = jnp.dot(a_ref[...].astype(BF16), w_ref[...], preferred_element_type=F32)
            acc = t if acc is None else acc + t
    o_ref[...] = x_ref[...] + g_ref[...] * acc


def _out_proj(parts, ws, x2, gate, tm, rows_per_batch, name):
    m, d = x2.shape
    body = functools.partial(_out_proj_body, n_in=len(parts))
    ws = [w.astype(BF16) for w in ws]
    return _rows_call(body, list(parts) + [x2], ws, [gate], [((m, d), F32)], tm, rows_per_batch, name)[0]


def _final_norm_body(x_ref, g_ref, o_ref):
    x = x_ref[...]
    o_ref[...] = x * lax.rsqrt(jnp.mean(x * x, axis=-1, keepdims=True) + EPS) * g_ref[...]


def _hy_prep_body(u_ref, p_ref, n_ref, w_ref, b_ref, x0_ref, z_ref, *, width):
    i = pl.program_id(1)
    last = pl.num_programs(1) - 1
    u = u_ref[...]
    tt = u.shape[0]
    prev_row = jnp.where(i == 0, 0.0, p_ref[SUBLANES - 1:SUBLANES, :])
    next_row = jnp.where(i == last, 0.0, n_ref[0:1, :])
    rows = lax.broadcasted_iota(jnp.int32, (tt, 1), 0)
    up = jnp.where(rows == 0, prev_row, pltpu.roll(u, 1, axis=0))
    dn = jnp.where(rows == tt - 1, next_row, pltpu.roll(u, tt - 1, axis=0))
    y = w_ref[0:1, :] * up + w_ref[1:2, :] * u + w_ref[2:3, :] * dn + b_ref[...]
    x0_ref[...] = y[:, :width]
    z_ref[...] = y[:, 2 * width:] * y[:, width:2 * width]


def _hyena_prep(p, short_w, short_b, bsz, seq, tt):
    w3 = p.shape[1]
    width = w3 // 3
    p3 = p.reshape(bsz, seq, w3)
    nt = seq // tt
    sub = tt // SUBLANES
    nsub = seq // SUBLANES
    body = functools.partial(_hy_prep_body, width=width)
    x0, z = pl.pallas_call(
        body,
        grid=(bsz, nt),
        in_specs=[
            pl.BlockSpec((None, tt, w3), lambda b, i: (b, i, 0)),
            pl.BlockSpec((None, SUBLANES, w3), lambda b, i: (b, jnp.maximum(i * sub - 1, 0), 0)),
            pl.BlockSpec((None, SUBLANES, w3), lambda b, i: (b, jnp.minimum((i + 1) * sub, nsub - 1), 0)),
            pl.BlockSpec((HY_SHORT, w3), lambda b, i: (0, 0)),
            pl.BlockSpec((1, w3), lambda b, i: (0, 0)),
        ],
        out_specs=[pl.BlockSpec((None, tt, width), lambda b, i: (b, i, 0))] * 2,
        out_shape=[jax.ShapeDtypeStruct((bsz, seq, width), F32)] * 2,
        compiler_params=_cparams(("parallel", "parallel")),
        name="hyena_prep",
    )(p3, p3, p3, short_w, short_b.reshape(1, -1))
    return x0, z


def _hy_filter_body(t_ref, w_ref, lag_ref, bands_ref, deltas_ref, w1t_ref, w1c_ref, w1s_ref, b1_ref,
                    w2_ref, b2_ref, w3_ref, h_ref, s_ref):
    i = pl.program_id(0)
    arg = w_ref[...] * bands_ref[...]
    pre = (t_ref[...] * w1t_ref[...]
           + jnp.dot(jnp.cos(arg), w1c_ref[...], precision=HI, preferred_element_type=F32)
           + jnp.dot(-jnp.sin(arg), w1s_ref[...], precision=HI, preferred_element_type=F32)
           + b1_ref[...])
    h1 = jnp.sin(pre)
    h2 = jnp.sin(jnp.dot(h1, w2_ref[...], precision=HI, preferred_element_type=F32) + b2_ref[...])
    h3 = jnp.dot(h2, w3_ref[...], precision=HI, preferred_element_type=F32)
    h = h3 * (jnp.exp(-lag_ref[...] * deltas_ref[...]) + HY_SHIFT)
    h_ref[...] = h

    @pl.when(i == 0)
    def _():
        s_ref[...] = jnp.zeros_like(s_ref)

    s_ref[...] += jnp.sum(jnp.abs(h), axis=0, keepdims=True)


def _hyena_filter(seq, w1, b1, w2, b2, w3):
    width = w3.shape[1]
    pos = jnp.arange(seq, dtype=F32)
    t = (pos / seq)[:, None]
    w = (2.0 * math.pi * pos / seq)[:, None]
    lag = (jnp.abs(pos - seq // 2) / (seq / 2))[:, None]
    bands = jnp.linspace(1e-4, HY_BANDS - 1, HY_BANDS, dtype=F32)[None]
    deltas = jnp.abs(jnp.linspace(HY_FAST_DECAY, HY_SLOW_DECAY, width, dtype=F32))[None]
    tl = min(seq, 1024)
    col = pl.BlockSpec((tl, 1), lambda i: (i, 0))

    def whole(a):
        return pl.BlockSpec(a.shape, lambda i, n=a.ndim: (0,) * n)

    consts = [bands, deltas, w1[0:1], w1[1:1 + HY_BANDS], w1[1 + HY_BANDS:], b1.reshape(1, -1),
              w2, b2.reshape(1, -1), w3]
    h, s = pl.pallas_call(
        _hy_filter_body,
        grid=(seq // tl,),
        in_specs=[col, col, col] + [whole(a) for a in consts],
        out_specs=[pl.BlockSpec((tl, width), lambda i: (i, 0)), pl.BlockSpec((1, width), lambda i: (0, 0))],
        out_shape=[jax.ShapeDtypeStruct((seq, width), F32), jax.ShapeDtypeStruct((1, width), F32)],
        compiler_params=_cparams(("arbitrary",)),
        name="hyena_filter",
    )(t, w, lag, *consts)
    return h, s


def _dft_tables(na):
    n = na * FFT_B
    a = np.arange(na)
    b = np.arange(FFT_B)
    ang1 = 2.0 * np.pi * np.outer(a, a) / na
    ang2 = 2.0 * np.pi * np.outer(b, b) / FFT_B
    angt = 2.0 * np.pi * np.outer(b, a) / n
    c2, s2 = np.cos(ang2), np.sin(ang2)
    tabs = dict(
        f1=np.concatenate([np.cos(ang1), -np.sin(ang1)], axis=0),
        f2=np.block([[c2, s2], [-s2, c2]]),
        f2i=np.block([[c2, -s2], [s2, c2]]),
        f3=np.concatenate([np.cos(ang1), -np.sin(ang1)], axis=1) / n,
        twc_b=np.cos(angt)[:, :, None], tws_b=np.sin(angt)[:, :, None],
        twc_c=np.cos(angt).T[:, :, None], tws_c=np.sin(angt).T[:, :, None],
    )
    return {k: jnp.asarray(v, F32) for k, v in tabs.items()}


FFT_G = SUBLANES
FFT_CW = 512


def _fft1_body(x_ref, f_ref, tc_ref, ts_ref, sc_ref, o_ref, *, na):
    f = f_ref[...].astype(BF16)
    for j in range(FFT_G):
        x = (x_ref[:, j, :] * sc_ref[...]).astype(BF16)
        g = jnp.dot(f, x, preferred_element_type=F32)
        gr, gi = g[:na], g[na:]
        tc, ts = tc_ref[j], ts_ref[j]
        o_ref[:, j, :] = jnp.concatenate([gr * tc + gi * ts, gi * tc - gr * ts], axis=0)


def _fft_stage1(x4, scale, tabs, na):
    bz, ka, _, ch = x4.shape
    cw = min(ch, FFT_CW)
    return pl.pallas_call(
        functools.partial(_fft1_body, na=na),
        grid=(bz, ch // cw, FFT_B // FFT_G),
        in_specs=[
            pl.BlockSpec((None, ka, FFT_G, cw), lambda z, q, b: (z, 0, b, q)),
            pl.BlockSpec((2 * na, ka), lambda z, q, b: (0, 0)),
            pl.BlockSpec((FFT_G, na, 1), lambda z, q, b: (b, 0, 0)),
            pl.BlockSpec((FFT_G, na, 1), lambda z, q, b: (b, 0, 0)),
            pl.BlockSpec((1, cw), lambda z, q, b: (0, q)),
        ],
        out_specs=pl.BlockSpec((None, 2 * na, FFT_G, cw), lambda z, q, b: (z, 0, b, q)),
        out_shape=jax.ShapeDtypeStruct((bz, 2 * na, FFT_B, ch), F32),
        compiler_params=_cparams(("parallel", "parallel", "parallel")),
        name="fft_stage1",
    )(x4, tabs["f1"][:, :ka], tabs["twc_b"], tabs["tws_b"], scale)


def _fft2_spec_body(ar_ref, ai_ref, f_ref, o_ref):
    f = f_ref[...].astype(BF16)
    for j in range(FFT_G):
        v = jnp.concatenate([ar_ref[j], ai_ref[j]], axis=0).astype(BF16)
        o_ref[j] = jnp.dot(f, v, preferred_element_type=F32)


def _fft2_conv_body(ar_ref, ai_ref, h_ref, f_ref, fi_ref, tc_ref, ts_ref, o_ref):
    f = f_ref[...].astype(BF16)
    fi = fi_ref[...].astype(BF16)
    for j in range(FFT_G):
        v = jnp.concatenate([ar_ref[j], ai_ref[j]], axis=0).astype(BF16)
        x = jnp.dot(f, v, preferred_element_type=F32)
        xr, xi = x[:FFT_B], x[FFT_B:]
        hr, hi = h_ref[j, :FFT_B, :], h_ref[j, FFT_B:, :]
        p = jnp.concatenate([xr * hr - xi * hi, xr * hi + xi * hr], axis=0).astype(BF16)
        q = jnp.dot(fi, p, preferred_element_type=F32)
        qr, qi = q[:FFT_B], q[FFT_B:]
        tc, ts = tc_ref[j], ts_ref[j]
        o_ref[:, j, :] = jnp.concatenate([qr * tc - qi * ts, qi * tc + qr * ts], axis=0)


def _fft_stage2_spectrum(a4, tabs, na):
    ch = a4.shape[-1]
    cw = min(ch, FFT_CW)
    ng = na // FFT_G
    return pl.pallas_call(
        _fft2_spec_body,
        grid=(ch // cw, ng),
        in_specs=[
            pl.BlockSpec((None, FFT_G, FFT_B, cw), lambda q, c: (0, c, 0, q)),
            pl.BlockSpec((None, FFT_G, FFT_B, cw), lambda q, c: (0, ng + c, 0, q)),
            pl.BlockSpec((2 * FFT_B, 2 * FFT_B), lambda q, c: (0, 0)),
        ],
        out_specs=pl.BlockSpec((FFT_G, 2 * FFT_B, cw), lambda q, c: (c, 0, q)),
        out_shape=jax.ShapeDtypeStruct((na, 2 * FFT_B, ch), F32),
        compiler_params=_cparams(("parallel", "parallel")),
        name="fft_stage2_spectrum",
    )(a4, a4, tabs["f2"])


def _fft_stage2_conv(a4, hspec, tabs, na):
    bz, _, _, ch = a4.shape
    cw = min(ch, FFT_CW)
    ng = na // FFT_G
    return pl.pallas_call(
        _fft2_conv_body,
        grid=(ch // cw, ng, bz),
        in_specs=[
            pl.BlockSpec((None, FFT_G, FFT_B, cw), lambda q, c, z: (z, c, 0, q)),
            pl.BlockSpec((None, FFT_G, FFT_B, cw), lambda q, c, z: (z, ng + c, 0, q)),
            pl.BlockSpec((FFT_G, 2 * FFT_B, cw), lambda q, c, z: (c, 0, q)),
            pl.BlockSpec((2 * FFT_B, 2 * FFT_B), lambda q, c, z: (0, 0)),
            pl.BlockSpec((2 * FFT_B, 2 * FFT_B), lambda q, c, z: (0, 0)),
            pl.BlockSpec((FFT_G, FFT_B, 1), lambda q, c, z: (c, 0, 0)),
            pl.BlockSpec((FFT_G, FFT_B, 1), lambda q, c, z: (c, 0, 0)),
        ],
        out_specs=pl.BlockSpec((None, 2 * FFT_B, FFT_G, cw), lambda q, c, z: (z, 0, c, q)),
        out_shape=jax.ShapeDtypeStruct((bz, 2 * FFT_B, na, ch), F32),
        compiler_params=_cparams(("parallel", "parallel", "parallel")),
        name="fft_stage2_conv",
    )(a4, a4, hspec, tabs["f2"], tabs["f2i"], tabs["twc_c"], tabs["tws_c"])


def _fft3_body(br_ref, bi_ref, f_ref, o_ref):
    f = f_ref[...].astype(BF16)
    for j in range(FFT_G):
        v = jnp.concatenate([br_ref[j], bi_ref[j]], axis=0).astype(BF16)
        o_ref[:, j, :] = jnp.dot(f, v, preferred_element_type=F32)


def _fft_stage3(b4, tabs, na, a_lo, a_cnt):
    bz, _, _, ch = b4.shape
    cw = min(ch, FFT_CW)
    ng = FFT_B // FFT_G
    return pl.pallas_call(
        _fft3_body,
        grid=(bz, ch // cw, ng),
        in_specs=[
            pl.BlockSpec((None, FFT_G, na, cw), lambda z, q, b: (z, b, 0, q)),
            pl.BlockSpec((None, FFT_G, na, cw), lambda z, q, b: (z, ng + b, 0, q)),
            pl.BlockSpec((a_cnt, 2 * na), lambda z, q, b: (0, 0)),
        ],
        out_specs=pl.BlockSpec((None, a_cnt, FFT_G, cw), lambda z, q, b: (z, 0, b, q)),
        out_shape=jax.ShapeDtypeStruct((bz, a_cnt, FFT_B, ch), F32),
        compiler_params=_cparams(("parallel", "parallel", "parallel")),
        name="fft_stage3",
    )(b4, b4, tabs["f3"][a_lo:a_lo + a_cnt])


def _hy_gate_body(x0_ref, y_ref, z_ref, b_ref, o_ref):
    o_ref[...] = x0_ref[...] * (y_ref[...] + b_ref[...] * z_ref[...])


def _hyena(p, bsz, seq, short_w, short_b, w1, b1, w2, b2, w3, hy_bias, tt):
    width = w3.shape[1]
    x0, z = _hyena_prep(p, short_w, short_b, bsz, seq, tt)
    hu, hs = _hyena_filter(seq, w1, b1, w2, b2, w3)
    na = max(2 * seq // FFT_B, 16)
    ka = max(seq // FFT_B, 16)
    tabs = _dft_tables(na)
    pad = ka * FFT_B - seq

    def rows4(a, lead):
        if pad:
            a = jnp.pad(a, ((0, 0), (0, pad), (0, 0)))
        return a.reshape(lead, ka, FFT_B, width)

    ones = jnp.ones((1, width), F32)
    hspec = _fft_stage2_spectrum(_fft_stage1(rows4(hu[None], 1), 1.0 / hs, tabs, na), tabs, na)
    a4 = _fft_stage1(rows4(z, bsz), ones, tabs, na)
    b4 = _fft_stage2_conv(a4, hspec, tabs, na)
    if pad:
        y = _fft_stage3(b4, tabs, na, 0, na).reshape(bsz, na * FFT_B, width)[:, seq // 2: seq // 2 + seq]
    else:
        y = _fft_stage3(b4, tabs, na, seq // 2 // FFT_B, ka).reshape(bsz, seq, width)
    m = bsz * seq
    return _rows_call(_hy_gate_body, [x0.reshape(m, width), y.reshape(m, width), z.reshape(m, width)],
                      [hy_bias.reshape(1, -1)], [], [((m, width), F32)], tt, seq, "hyena_gate")[0]


def _s5_tables(lam_re, lam_im, log_dt, b_re, b_im, c_re, c_im, nsteps):
    t_len, hdim = S5_CHUNK, S5_GROUP
    lam = lax.complex(jnp.minimum(lam_re.astype(F32), -1e-4), lam_im.astype(F32))
    dt = jnp.exp(log_dt.astype(F32))[..., None]
    lam_dt = lam * dt
    lam_bar = jnp.exp(lam_dt)
    b_bar = ((lam_bar - 1.0) / lam)[..., None] * lax.complex(b_re.astype(F32), b_im.astype(F32))
    cm = lax.complex(c_re.astype(F32), c_im.astype(F32))
    ks = jnp.arange(t_len + 1, dtype=F32)
    pw = jnp.exp(ks[:, None, None, None] * lam_dt[None])
    g = lam.shape[1]
    tabs = {}
    pin = [pw[:t_len, 0][::-1], pw[:t_len, 1]]
    pout = [pw[1:, 0], pw[1:, 1][::-1]]
    for d in range(2):
        win = pin[d][:, :, :, None] * b_bar[d][None]
        win = jnp.transpose(win, (1, 0, 3, 2)).reshape(g, t_len * hdim, -1)
        tabs[f"win{d}"] = jnp.concatenate([jnp.real(win), jnp.imag(win)], axis=-1)
        wout = pout[d][:, :, None, :] * cm[d][None]
        wout = jnp.transpose(wout, (1, 3, 0, 2)).reshape(g, -1, t_len * hdim)
        tabs[f"wout{d}"] = jnp.concatenate([jnp.real(wout), -jnp.imag(wout)], axis=1)
        mu = jnp.exp((t_len * 2.0 ** jnp.arange(nsteps, dtype=F32))[:, None, None] * lam_dt[d][None])
        mr, mi = jnp.real(mu), jnp.imag(mu)
        tabs[f"m1{d}"] = jnp.concatenate([mr, mr], axis=-1)[:, :, None, :]
        tabs[f"m2{d}"] = jnp.concatenate([-mi, mi], axis=-1)[:, :, None, :]
        mu1 = jnp.exp(t_len * lam_dt[d])
        tabs[f"mu1{d}"] = jnp.concatenate([jnp.real(mu1), jnp.real(mu1)], axis=-1)[:, None, :]
        tabs[f"mu2{d}"] = jnp.concatenate([-jnp.imag(mu1), jnp.imag(mu1)], axis=-1)[:, None, :]
    kern = [jnp.real(jnp.einsum("ghp,tgp,gpk->tghk", cm[d], pw[:t_len, d], b_bar[d], precision=HI)) for d in range(2)]
    s_idx = jnp.arange(t_len)[:, None]
    t_idx = jnp.arange(t_len)[None, :]
    fwd = jnp.where((t_idx >= s_idx)[:, :, None, None, None], kern[0][jnp.maximum(t_idx - s_idx, 0)], 0.0)
    bwd = jnp.where((s_idx >= t_idx)[:, :, None, None, None], kern[1][jnp.maximum(s_idx - t_idx, 0)], 0.0)
    d0 = fwd + bwd
    tabs["d0"] = jnp.transpose(d0, (2, 0, 4, 1, 3)).reshape(g, t_len * hdim, t_len * hdim)
    return tabs


S5_GPB = LANES // S5_GROUP


def _s5_perm_table():
    p = np.zeros((S5_CHUNK, S5_GPB, LANES, LANES), np.float32)
    for t in range(S5_CHUNK):
        for g in range(S5_GPB):
            for h in range(S5_GROUP):
                p[t, g, S5_GROUP * g + h, S5_GROUP * t + h] = 1.0
    return jnp.asarray(p)


def _s5_body(u_ref, perm_ref, permt_ref, d0_ref, win0_ref, win1_ref, wout0_ref, wout1_ref, m10_ref, m20_ref, m11_ref,
             m21_ref, mu10_ref, mu20_ref, mu11_ref, mu21_ref, s0_ref, s1_ref, y_ref, f0_ref, f1_ref, *, nc, nsteps):
    half = S5_STATE
    t_len = S5_CHUNK
    j = lax.broadcasted_iota(jnp.int32, (nc, 1), 0)

    def swap(a):
        return pltpu.roll(a, half, axis=1)

    xs = [u_ref[pl.ds(t, nc, stride=t_len), :].astype(BF16) for t in range(t_len)]
    ys = []
    for g in range(S5_GPB):
        u = None
        for t in range(t_len):
            part = jnp.dot(xs[t], perm_ref[t, g].astype(BF16), preferred_element_type=F32)
            u = part if u is None else u + part
        u = u.astype(BF16)
        e0 = jnp.dot(u, win0_ref[g].astype(BF16), preferred_element_type=F32)
        e1 = jnp.dot(u, win1_ref[g].astype(BF16), preferred_element_type=F32)
        s_f = jnp.where(j == 0, s0_ref[g:g + 1, :], pltpu.roll(e0, 1, axis=0))
        s_b = jnp.where(j == nc - 1, s1_ref[g:g + 1, :], pltpu.roll(e1, nc - 1, axis=0))
        for k in range(nsteps):
            step = 2 ** k
            sh = jnp.where(j >= step, pltpu.roll(s_f, step, axis=0), 0.0)
            s_f = s_f + m10_ref[k, g] * sh + m20_ref[k, g] * swap(sh)
            sh = jnp.where(j < nc - step, pltpu.roll(s_b, nc - step, axis=0), 0.0)
            s_b = s_b + m11_ref[k, g] * sh + m21_ref[k, g] * swap(sh)
        y = jnp.dot(u, d0_ref[g].astype(BF16), preferred_element_type=F32)
        y = y + jnp.dot(s_f.astype(BF16), wout0_ref[g].astype(BF16), preferred_element_type=F32)
        y = y + jnp.dot(s_b.astype(BF16), wout1_ref[g].astype(BF16), preferred_element_type=F32)
        y_hi = y.astype(BF16)
        ys.append((y_hi, (y - y_hi.astype(F32)).astype(BF16)))
        t_f = mu10_ref[g] * s_f + mu20_ref[g] * swap(s_f) + e0
        t_b = mu11_ref[g] * s_b + mu21_ref[g] * swap(s_b) + e1
        f0_ref[g:g + 1, :] = t_f[nc - 1:nc, :]
        f1_ref[g:g + 1, :] = t_b[0:1, :]
    for t in range(t_len):
        out = None
        for g in range(S5_GPB):
            pt = permt_ref[t, g].astype(BF16)
            part = jnp.dot(ys[g][0], pt, preferred_element_type=F32) + jnp.dot(ys[g][1], pt, preferred_element_type=F32)
            out = part if out is None else out + part
        y_ref[pl.ds(t, nc, stride=t_len), :] = out


def _s5_core(u2, bsz, seq, tabs, init0, init1):
    m, width = u2.shape
    g = width // S5_GROUP
    nq = width // LANES
    t_len = S5_CHUNK
    nc = seq // t_len
    cols = t_len * S5_GROUP
    nsteps = max(1, math.ceil(math.log2(nc)))
    assert nsteps <= tabs["m10"].shape[0] and cols == LANES
    perm = _s5_perm_table()
    permt = jnp.swapaxes(perm, 2, 3)

    def whole(a):
        return pl.BlockSpec(a.shape, lambda q, b, nd=a.ndim: (0,) * nd)

    def per_q(shape):
        return pl.BlockSpec((S5_GPB,) + shape, lambda q, b: (q,) + (0,) * len(shape))

    p2 = 2 * S5_STATE
    steps_spec = pl.BlockSpec((nsteps, S5_GPB, 1, p2), lambda q, b: (0, q, 0, 0))
    state_spec = pl.BlockSpec((None, S5_GPB, p2), lambda q, b: (b, q, 0))
    seq_spec = pl.BlockSpec((seq, LANES), lambda q, b: (b, q))
    in_specs = [seq_spec, whole(perm), whole(permt), per_q((cols, cols)), per_q((cols, p2)), per_q((cols, p2)),
                per_q((p2, cols)), per_q((p2, cols))] + [steps_spec] * 4 + [per_q((1, p2))] * 4 + [state_spec] * 2
    state_shape = jax.ShapeDtypeStruct((bsz, g, p2), F32)
    y, f0, f1 = pl.pallas_call(
        functools.partial(_s5_body, nc=nc, nsteps=nsteps),
        grid=(nq, bsz),
        in_specs=in_specs,
        out_specs=[seq_spec, state_spec, state_spec],
        out_shape=[jax.ShapeDtypeStruct((m, width), F32), state_shape, state_shape],
        compiler_params=_cparams(("parallel", "parallel")),
        name="s5_scan",
    )(u2, perm, permt, tabs["d0"], tabs["win0"], tabs["win1"], tabs["wout0"], tabs["wout1"],
      tabs["m10"][:nsteps], tabs["m20"][:nsteps], tabs["m11"][:nsteps], tabs["m21"][:nsteps],
      tabs["mu10"], tabs["mu20"], tabs["mu11"], tabs["mu21"], init0, init1)
    return y, f0, f1


def _s5_glu_body(u_ref, y_ref, d_ref, w_ref, b_ref, o_ref):
    y = d_ref[...] * u_ref[...] + y_ref[...]
    g = jax.nn.gelu(y)
    o_ref[...] = g * jax.nn.sigmoid(jnp.dot(g.astype(BF16), w_ref[...], preferred_element_type=F32) + b_ref[...])


def _s5_glu(u2, y2, d_skip, glu_w, glu_b, tm, name):
    m, width = u2.shape
    return _rows_call(_s5_glu_body, [u2, y2], [d_skip.reshape(1, -1), glu_w.astype(BF16), glu_b.reshape(1, -1)], [],
                      [((m, width), F32)], tm, m, name)[0]


def _s5_mixer(u_lat, u_ctx, bsz, seq, seq_c, lam_re, lam_im, log_dt, b_re, b_im, c_re, c_im, d_skip, glu_w, glu_b,
              tm, tm_c):
    nsteps = max(1, math.ceil(math.log2(seq // S5_CHUNK)))
    tabs = _s5_tables(lam_re, lam_im, log_dt, b_re, b_im, c_re, c_im, nsteps)
    g = u_lat.shape[1] // S5_GROUP
    zero = jnp.zeros((bsz, g, 2 * S5_STATE), F32)
    y_ctx, f0, f1 = _s5_core(u_ctx, bsz, seq_c, tabs, zero, zero)
    y_lat, _, _ = _s5_core(u_lat, bsz, seq, tabs, f0, f1)
    out_lat = _s5_glu(u_lat, y_lat, d_skip, glu_w, glu_b, tm, "s5_glu")
    out_ctx = _s5_glu(u_ctx, y_ctx, d_skip, glu_w, glu_b, tm_c, "s5_glu_ctx")
    return out_lat, out_ctx


def _rope_body(*refs, shift, heads, scale, rope):
    if rope:
        x_ref, cos_ref, sin_ref, o_ref = refs
    else:
        x_ref, o_ref = refs
    x = x_ref[...]
    w = x.shape[1]
    if rope:
        lane = lax.broadcasted_iota(jnp.int32, x.shape, 1)
        partner = jnp.where(lane % (2 * shift) < shift, pltpu.roll(x, w - shift, axis=1), pltpu.roll(x, shift, axis=1))
        x = x * cos_ref[...] + partner * sin_ref[...]
    if scale != 1.0:
        x = x * scale
    hd = w // heads
    for h in range(heads):
        o_ref[h] = x[:, h * hd:(h + 1) * hd]


def _rope_heads(x2, tables, shift, heads, scale, tm, rows_per_batch, name):
    m, w = x2.shape
    body = functools.partial(_rope_body, shift=shift, heads=heads, scale=scale, rope=tables is not None)
    return _rows_call(body, [x2], [], [], [((heads, m, w // heads), F32)], tm, rows_per_batch, name,
                      periodic=list(tables) if tables is not None else [])[0]


def _rope_tables_1d(seq, hd, heads):
    half = hd // 2
    inv = ROPE_BASE ** (-jnp.arange(half, dtype=F32) / half)
    ang = jnp.arange(seq, dtype=F32)[:, None] * inv[None]
    cos, sin = jnp.cos(ang), jnp.sin(ang)
    return jnp.tile(jnp.concatenate([cos, cos], -1), (1, heads)), jnp.tile(jnp.concatenate([-sin, sin], -1), (1, heads))


def _rope_tables_2d(seq, hd, heads):
    q = hd // 4
    inv = ROPE_BASE ** (-jnp.arange(q, dtype=F32) / q)
    n_rows = seq // GRID_W
    row = jnp.repeat(jnp.arange(n_rows, dtype=F32), GRID_W)
    col = jnp.tile(jnp.arange(GRID_W, dtype=F32), n_rows)
    ar, ac = row[:, None] * inv[None], col[:, None] * inv[None]
    cos = jnp.concatenate([jnp.cos(ar), jnp.cos(ar), jnp.cos(ac), jnp.cos(ac)], -1)
    sin = jnp.concatenate([-jnp.sin(ar), jnp.sin(ar), -jnp.sin(ac), jnp.sin(ac)], -1)
    return jnp.tile(cos, (1, heads)), jnp.tile(sin, (1, heads))


def _kv_update(k, kdec, v):
    kd = (k * kdec).astype(BF16)
    return lax.dot_general(kd, v.astype(BF16), (((0,), (0,)), ((), ())), preferred_element_type=F32)


def _ret_bwd_body(k_ref, v_ref, kdec_ref, gc_ref, init_ref, sprev_ref, fin_ref, s_ref):
    @pl.when(pl.program_id(1) == 0)
    def _():
        s_ref[...] = init_ref[...]

    for h in range(k_ref.shape[0]):
        s = s_ref[h]
        sprev_ref[h] = s
        s = gc_ref[h] * s + _kv_update(k_ref[h], kdec_ref[h], v_ref[h])
        s_ref[h] = s
        fin_ref[h] = s


def _ret_main_body(q_ref, k_ref, v_ref, g_ref, mask_ref, qdf_ref, kdf_ref, qdb_ref, gc_ref, init_ref, sb_ref,
                   o_ref, fin_ref, s_ref):
    @pl.when(pl.program_id(1) == 0)
    def _():
        s_ref[...] = init_ref[...]

    for h in range(q_ref.shape[0]):
        q, k, v = q_ref[h], k_ref[h], v_ref[h]
        vb = v.astype(BF16)
        sc = lax.dot_general(q.astype(BF16), k.astype(BF16), (((1,), (1,)), ((), ())), preferred_element_type=F32)
        o = jnp.dot((sc * mask_ref[h]).astype(BF16), vb, preferred_element_type=F32)
        s = s_ref[h]
        o = o + jnp.dot((q * qdf_ref[h]).astype(BF16), s.astype(BF16), preferred_element_type=F32)
        o = o + jnp.dot((q * qdb_ref[h]).astype(BF16), sb_ref[h].astype(BF16), preferred_element_type=F32)
        o = o * lax.rsqrt(jnp.mean(o * o, axis=-1, keepdims=True) + EPS)
        o_ref[h] = o * jax.nn.silu(g_ref[h])
        s = gc_ref[h] * s + _kv_update(k, kdf_ref[h], v)
        s_ref[h] = s
        fin_ref[h] = s


def _ret_tables(decay_logit):
    cl = RET_CHUNK
    log_g = jax.nn.log_sigmoid(decay_logit.astype(F32))
    idx = jnp.arange(cl, dtype=F32)
    diff = idx[:, None] - idx[None, :]
    mf = jnp.where(diff[None] >= 0, jnp.exp(jnp.maximum(diff, 0.0)[None] * log_g[0][:, None, None]), 0.0)
    mb = jnp.where(diff[None] <= 0, jnp.exp(jnp.maximum(-diff, 0.0)[None] * log_g[1][:, None, None]), 0.0)

    def col(e, d):
        return jnp.exp(e[None, :] * log_g[d][:, None])[:, :, None]

    return dict(mask=mf + mb, qdf=col(idx + 1.0, 0), kdf=col(cl - 1.0 - idx, 0), qdb=col(cl - idx, 1), kdb=col(idx, 1),
                gcf=jnp.exp(cl * log_g[0])[:, None, None], gcb=jnp.exp(cl * log_g[1])[:, None, None])


def _retention(q, k, v, g, bsz, seq, tabs, init_f, init_b):
    heads, _, dk = q.shape
    dv = v.shape[2]
    cl = RET_CHUNK
    n = seq // cl

    def whole(a):
        return pl.BlockSpec(a.shape, lambda b, i, nd=a.ndim: (0,) * nd)

    state_spec = pl.BlockSpec((None, heads, dk, dv), lambda b, i: (b, 0, 0, 0))
    state_shape = jax.ShapeDtypeStruct((bsz, heads, dk, dv), F32)
    rev = lambda b, i: (0, b * n + (n - 1 - i), 0)
    sprev_b, fin_b = pl.pallas_call(
        _ret_bwd_body,
        grid=(bsz, n),
        in_specs=[pl.BlockSpec((heads, cl, dk), rev), pl.BlockSpec((heads, cl, dv), rev),
                  whole(tabs["kdb"]), whole(tabs["gcb"]), state_spec],
        out_specs=[pl.BlockSpec((None, None, heads, dk, dv), lambda b, i: (b, n - 1 - i, 0, 0, 0)), state_spec],
        out_shape=[jax.ShapeDtypeStruct((bsz, n, heads, dk, dv), F32), state_shape],
        scratch_shapes=[pltpu.VMEM((heads, dk, dv), F32)],
        compiler_params=_cparams(("parallel", "arbitrary")),
        name="retention_backward_states",
    )(k, v, tabs["kdb"], tabs["gcb"], init_b)
    fwd = lambda b, i: (0, b * n + i, 0)
    o, fin_f = pl.pallas_call(
        _ret_main_body,
        grid=(bsz, n),
        in_specs=[pl.BlockSpec((heads, cl, dk), fwd), pl.BlockSpec((heads, cl, dk), fwd),
                  pl.BlockSpec((heads, cl, dv), fwd), pl.BlockSpec((heads, cl, dv), fwd),
                  whole(tabs["mask"]), whole(tabs["qdf"]), whole(tabs["kdf"]), whole(tabs["qdb"]), whole(tabs["gcf"]),
                  state_spec, pl.BlockSpec((None, None, heads, dk, dv), lambda b, i: (b, i, 0, 0, 0))],
        out_specs=[pl.BlockSpec((heads, cl, dv), fwd), state_spec],
        out_shape=[jax.ShapeDtypeStruct((heads, bsz * seq, dv), F32), state_shape],
        scratch_shapes=[pltpu.VMEM((heads, dk, dv), F32)],
        compiler_params=_cparams(("parallel", "arbitrary")),
        name="retention",
    )(q, k, v, g, tabs["mask"], tabs["qdf"], tabs["kdf"], tabs["qdb"], tabs["gcf"], init_f, sprev_b)
    return o, fin_f, fin_b


def _swa_body(*refs, local, seq, scale):
    if local:
        q_ref, kp_ref, kc_ref, kn_ref, vp_ref, vc_ref, vn_ref, kx_ref, vx_ref, sink_ref, o_ref = refs
    else:
        q_ref, kx_ref, vx_ref, sink_ref, o_ref = refs
    i = pl.program_id(1)
    bk = q_ref.shape[1]
    nkv = kx_ref.shape[0]
    grp = q_ref.shape[0] // nkv
    nt = (((1,), (1,)), ((), ()))
    hd = q_ref.shape[2]
    rows = grp * bk
    if local:
        row = lax.broadcasted_iota(jnp.int32, (rows, 3 * bk), 0) % bk
        col = lax.broadcasted_iota(jnp.int32, (rows, 3 * bk), 1)
        key_pos = (i - 1) * bk + col
        valid = (jnp.abs(col - (row + bk)) <= SWA_WINDOW) & (key_pos >= 0) & (key_pos < seq)
    for kv in range(nkv):
        kx = kx_ref[kv].astype(BF16)
        vx = vx_ref[kv].astype(BF16)
        q = q_ref[kv * grp:(kv + 1) * grp].reshape(rows, hd).astype(BF16)
        sink = jnp.concatenate([jnp.broadcast_to(sink_ref[kv * grp + gi], (bk, 1)) for gi in range(grp)], axis=0)
        s_x = lax.dot_general(q, kx, nt, preferred_element_type=F32) * scale
        m = jnp.maximum(jnp.max(s_x, axis=-1, keepdims=True), sink)
        if local:
            kl = jnp.concatenate([kp_ref[kv], kc_ref[kv], kn_ref[kv]], axis=0).astype(BF16)
            vl = jnp.concatenate([vp_ref[kv], vc_ref[kv], vn_ref[kv]], axis=0).astype(BF16)
            s_l = lax.dot_general(q, kl, nt, preferred_element_type=F32) * scale
            s_l = jnp.where(valid, s_l, NEG_INF)
            m = jnp.maximum(m, jnp.max(s_l, axis=-1, keepdims=True))
        p_x = jnp.exp(s_x - m)
        den = jnp.sum(p_x, axis=-1, keepdims=True) + jnp.exp(sink - m)
        o = jnp.dot(p_x.astype(BF16), vx, preferred_element_type=F32)
        if local:
            p_l = jnp.exp(s_l - m)
            den = den + jnp.sum(p_l, axis=-1, keepdims=True)
            o = o + jnp.dot(p_l.astype(BF16), vl, preferred_element_type=F32)
        o_ref[kv * grp:(kv + 1) * grp] = (o / den).reshape(grp, bk, hd)


def _swa(q, k, v, kx, vx, sink, bsz, seq, seq_c, local):
    hq, m, hd = q.shape
    hkv = kx.shape[0]
    bk = SWA_BLOCK
    nb = seq // bk
    scale = hd ** -0.5
    cur = lambda b, i: (0, b * nb + i, 0)
    prv = lambda b, i: (0, b * nb + jnp.maximum(i - 1, 0), 0)
    nxt = lambda b, i: (0, b * nb + jnp.minimum(i + 1, nb - 1), 0)
    ctx_spec = pl.BlockSpec((hkv, seq_c, hd), lambda b, i: (0, b, 0))
    sink_spec = pl.BlockSpec((hq, 1, 1), lambda b, i: (0, 0, 0))
    kvb = lambda f: pl.BlockSpec((hkv, bk, hd), f)
    in_specs = [pl.BlockSpec((hq, bk, hd), cur)]
    args = [q]
    if local:
        in_specs += [kvb(prv), kvb(cur), kvb(nxt), kvb(prv), kvb(cur), kvb(nxt)]
        args += [k, k, k, v, v, v]
    in_specs += [ctx_spec, ctx_spec, sink_spec]
    args += [kx, vx, sink.astype(F32).reshape(hq, 1, 1)]
    return pl.pallas_call(
        functools.partial(_swa_body, local=local, seq=seq, scale=scale),
        grid=(bsz, nb),
        in_specs=in_specs,
        out_specs=pl.BlockSpec((hq, bk, hd), cur),
        out_shape=jax.ShapeDtypeStruct((hq, m, hd), F32),
        compiler_params=_cparams(("parallel", "parallel")),
        name="swa" if local else "swa_ctx",
    )(*args)


def _mixer_cd_core(lat, cx, bsz, seq, seq_c, decay_logit, sink, with_ctx, tm, tm_c):
    dk = lat["rq"].shape[1] // RET_HEADS
    hd = lat["sq"].shape[1] // SWA_Q_HEADS
    t1 = _rope_tables_1d(seq, dk, RET_HEADS)
    rq = _rope_heads(lat["rq"], t1, dk // 2, RET_HEADS, dk ** -0.5, tm, seq, "ret_rope_q")
    rk = _rope_heads(lat["rk"], t1, dk // 2, RET_HEADS, 1.0, tm, seq, "ret_rope_k")
    rqc = _rope_heads(cx["rq"], None, 0, RET_HEADS, dk ** -0.5, tm_c, seq_c, "ret_heads_q_ctx")
    rkc = _rope_heads(cx["rk"], None, 0, RET_HEADS, 1.0, tm_c, seq_c, "ret_heads_k_ctx")
    tabs = _ret_tables(decay_logit)
    dv = lat["rv"].shape[2]
    zero = jnp.zeros((bsz, RET_HEADS, dk, dv), F32)
    ro_c, fin_f, fin_b = _retention(rqc, rkc, cx["rv"], cx["rg"], bsz, seq_c, tabs, zero, zero)
    ro_l, _, _ = _retention(rq, rk, lat["rv"], lat["rg"], bsz, seq, tabs, fin_f, fin_b)
    t2 = _rope_tables_2d(seq, hd, 1)
    sq = _rope_heads(lat["sq"], [jnp.tile(t, (1, SWA_Q_HEADS)) for t in t2], hd // 4, SWA_Q_HEADS, 1.0, tm, seq,
                     "swa_rope_q")
    sk = _rope_heads(lat["sk"], [jnp.tile(t, (1, SWA_KV_HEADS)) for t in t2], hd // 4, SWA_KV_HEADS, 1.0, tm, seq,
                     "swa_rope_k")
    skc = _rope_heads(cx["sk"], None, 0, SWA_KV_HEADS, 1.0, tm_c, seq_c, "swa_heads_k_ctx")
    so_l = _swa(sq, sk, lat["sv"], skc, cx["sv"], sink, bsz, seq, seq_c, True)
    so_c = None
    if with_ctx:
        sqc = _rope_heads(cx["sq"], None, 0, SWA_Q_HEADS, 1.0, tm_c, seq_c, "swa_heads_q_ctx")
        so_c = _swa(sqc, None, None, skc, cx["sv"], sink, bsz, seq_c, seq_c, False)
    return ro_l, so_l, (ro_c if with_ctx else None), so_c


def _store_tile_rows(ref, val, base=0):
    r, d = val.shape
    sub = d // LANES
    for c in range(sub):
        ref[pl.ds(base + c, r, stride=sub), :] = val[:, c * LANES:(c + 1) * LANES]


def _load_tile_rows(ref, r, sub, base=0):
    return jnp.concatenate([ref[pl.ds(base + c, r, stride=sub), :] for c in range(sub)], axis=1)


def _router_body(x_ref, g_ref, rw_ref, rb_ref, tril_ref, sc_ref, sh_ref, hx_ref, ti_ref, gt_ref, pos_ref, cnt_ref):
    @pl.when(pl.program_id(0) == 0)
    def _():
        cnt_ref[...] = jnp.zeros_like(cnt_ref)

    hx = _norm_mod(x_ref[...], g_ref[...], sc_ref[...], sh_ref[...])
    _store_tile_rows(hx_ref, hx)
    logits = jnp.dot(hx, rw_ref[...], precision=HI, preferred_element_type=F32) + rb_ref[...]
    lane = lax.broadcasted_iota(jnp.int32, logits.shape, 1).astype(F32)
    rem = logits
    vals, hots = [], []
    for k in range(TOP_K):
        m = jnp.max(rem, axis=-1, keepdims=True)
        idx = jnp.min(jnp.where(rem == m, lane, float(N_EXPERTS)), axis=-1, keepdims=True)
        hot = lane == idx
        rem = jnp.where(hot, NEG_INF, rem)
        vals.append(m)
        hots.append(hot.astype(F32))
        ti_ref[:, k:k + 1] = idx.astype(jnp.int32)
    exps = [jnp.exp(v - vals[0]) for v in vals]
    den = exps[0] + exps[1] + exps[2] + exps[3]
    for k in range(TOP_K):
        gt_ref[:, k:k + 1] = exps[k] / den
    sel = hots[0] + hots[1] + hots[2] + hots[3]
    before = jnp.dot(tril_ref[...], sel.astype(BF16), preferred_element_type=F32) + cnt_ref[...]
    for k in range(TOP_K):
        pos_ref[:, k:k + 1] = jnp.sum(hots[k] * before, axis=-1, keepdims=True).astype(jnp.int32)
    cnt_ref[...] += jnp.sum(sel, axis=0, keepdims=True)


def _moe_route(x2, g, sc, sh, router_w, router_b, tr, rows_per_batch):
    n, d = x2.shape
    tiles_per_batch = rows_per_batch // tr
    tril = jnp.asarray(np.tril(np.ones((tr, tr), np.float32), -1)).astype(BF16)
    whole = lambda a: pl.BlockSpec(a.shape, lambda i, nd=a.ndim: (0,) * nd)
    bat = pl.BlockSpec((None, 1, d), lambda i: (i // tiles_per_batch, 0, 0))
    g2, rb2 = g.reshape(1, -1), router_b.reshape(1, -1)
    small = lambda dt: jax.ShapeDtypeStruct((n, TOP_K), dt)
    small_spec = pl.BlockSpec((tr, TOP_K), lambda i: (i, 0))
    return pl.pallas_call(
        _router_body,
        grid=(n // tr,),
        in_specs=[pl.BlockSpec((tr, d), lambda i: (i, 0)), whole(g2), whole(router_w), whole(rb2), whole(tril), bat, bat],
        out_specs=[pl.BlockSpec((tr * (d // LANES), LANES), lambda i: (i, 0)), small_spec, small_spec, small_spec,
                   pl.BlockSpec((1, N_EXPERTS), lambda i: (0, 0))],
        out_shape=[jax.ShapeDtypeStruct((n * (d // LANES), LANES), F32), small(jnp.int32), small(F32), small(jnp.int32),
                   jax.ShapeDtypeStruct((1, N_EXPERTS), F32)],
        compiler_params=_cparams(("arbitrary",)),
        name="moe_router",
    )(x2, g2, router_w, rb2, tril, sc, sh)


def _tile_row_copy(src, s_off, dst, d_off, sem, sub):
    return pltpu.make_async_copy(src.at[pl.ds(pl.multiple_of(s_off, sub), sub)],
                                 dst.at[pl.ds(pl.multiple_of(d_off, sub), sub)], sem)


def _dispatch_body(dest_ref, hx_ref, xs_in_ref, xs_ref, sem, *, td, sub):
    del xs_in_ref

    def issue(n, carry):
        for k in range(TOP_K):
            _tile_row_copy(hx_ref, n * sub, xs_ref, dest_ref[n * TOP_K + k], sem, sub).start()
        return carry

    def drain(n, carry):
        for k in range(TOP_K):
            _tile_row_copy(hx_ref, 0, xs_ref, 0, sem, sub).wait()
        return carry

    lax.fori_loop(0, td, issue, 0, unroll=DMA_UNROLL)
    lax.fori_loop(0, td, drain, 0, unroll=DMA_UNROLL)


def _moe_dispatch(hx, dest_off, n_slots, td, sub, xs_init=None):
    lanes = hx.shape[1]
    n = hx.shape[0] // sub
    zeros = jnp.zeros((n_slots * sub, lanes), F32) if xs_init is None else xs_init
    return pl.pallas_call(
        functools.partial(_dispatch_body, td=td, sub=sub),
        grid=(n // td,),
        in_specs=[pl.BlockSpec((td * TOP_K,), lambda i: (i,), memory_space=pltpu.SMEM),
                  pl.BlockSpec((td * sub, lanes), lambda i: (i, 0)), pl.BlockSpec(memory_space=pl.ANY)],
        out_specs=pl.BlockSpec(memory_space=pl.ANY),
        out_shape=jax.ShapeDtypeStruct((n_slots * sub, lanes), F32),
        scratch_shapes=[pltpu.SemaphoreType.DMA(())],
        input_output_aliases={2: 0},
        compiler_params=pltpu.CompilerParams(dimension_semantics=("arbitrary",), has_side_effects=True,
                                             vmem_limit_bytes=VMEM_LIMIT),
        name="moe_dispatch",
    )(dest_off, hx, zeros)


def _ffn_body(be_ref, nu_ref, x_ref, wgu_ref, bgu_ref, wdn_ref, bdn_ref, o_ref, wgu_bf, wdn_bf, *, tm):
    j = pl.program_id(0)
    e = be_ref[j]
    prev = be_ref[jnp.maximum(j - 1, 0)]

    @pl.when((j == 0) | (e != prev))
    def _():
        wgu_bf[...] = wgu_ref[...].astype(BF16)
        wdn_bf[...] = wdn_ref[...].astype(BF16)

    @pl.when(j < nu_ref[0])
    def _():
        f = wdn_ref.shape[0]
        x = _load_tile_rows(x_ref, tm, wgu_ref.shape[0] // LANES).astype(BF16)
        gu = jnp.dot(x, wgu_bf[...], preferred_element_type=F32) + bgu_ref[...]
        gate = jnp.minimum(gu[:, :f], SWIGLU_LIMIT)
        up = jnp.clip(gu[:, f:], -SWIGLU_LIMIT, SWIGLU_LIMIT)
        act = gate * jax.nn.sigmoid(SWIGLU_ALPHA * gate) * (up + 1.0)
        _store_tile_rows(o_ref, jnp.dot(act.astype(BF16), wdn_bf[...], preferred_element_type=F32) + bdn_ref[...])

    @pl.when(j >= nu_ref[0])
    def _():
        o_ref[...] = jnp.zeros_like(o_ref)


def _moe_ffn(xs, block_exp, n_used, layer, w_gu, b_gu, w_dn, b_dn, tm):
    depth, n_exp, d, f2 = w_gu.shape
    f = w_dn.shape[2]
    sub = d // LANES
    n_slots = xs.shape[0] // sub
    blk = lambda j, be, nu: (jnp.minimum(j, nu[0] - 1), 0)
    exp4 = lambda j, be, nu: (layer, be[j], 0, 0)
    grid_spec = pltpu.PrefetchScalarGridSpec(
        num_scalar_prefetch=2,
        grid=(n_slots // tm,),
        in_specs=[pl.BlockSpec((tm * sub, LANES), blk), pl.BlockSpec((None, None, d, f2), exp4),
                  pl.BlockSpec((None, None, 1, f2), exp4), pl.BlockSpec((None, None, f, d), exp4),
                  pl.BlockSpec((None, None, 1, d), exp4)],
        out_specs=pl.BlockSpec((tm * sub, LANES), lambda j, be, nu: (j, 0)),
        scratch_shapes=[pltpu.VMEM((d, f2), BF16), pltpu.VMEM((f, d), BF16)],
    )
    return pl.pallas_call(
        functools.partial(_ffn_body, tm=tm),
        grid_spec=grid_spec,
        out_shape=jax.ShapeDtypeStruct((n_slots * sub, LANES), F32),
        compiler_params=_cparams(("arbitrary",)),
        name="moe_ffn",
    )(block_exp, n_used, xs, w_gu, b_gu.reshape(depth, n_exp, 1, f2), w_dn, b_dn.reshape(depth, n_exp, 1, d))


def _combine_body(dest_ref, gt_ref, x_ref, ys_ref, g_ref, o_ref, buf, sem, *, tc, sub):
    def issue(n, carry):
        for k in range(TOP_K):
            _tile_row_copy(ys_ref, dest_ref[n * TOP_K + k], buf, (k * tc + n) * sub, sem, sub).start()
        return carry

    def drain(n, carry):
        for k in range(TOP_K):
            _tile_row_copy(ys_ref, 0, buf, 0, sem, sub).wait()
        return carry

    lax.fori_loop(0, tc, issue, 0, unroll=DMA_UNROLL)
    lax.fori_loop(0, tc, drain, 0, unroll=DMA_UNROLL)
    gates = [jnp.broadcast_to(gt_ref[:, k:k + 1], (tc, LANES)) for k in range(TOP_K)]
    for c in range(sub):
        cols = slice(c * LANES, (c + 1) * LANES)
        acc = gates[0] * buf[pl.ds(c, tc, stride=sub), :]
        for k in range(1, TOP_K):
            acc = acc + gates[k] * buf[pl.ds(k * tc * sub + c, tc, stride=sub), :]
        o_ref[:, cols] = x_ref[:, cols] + g_ref[:, cols] * acc


def _moe_combine(ys, dest_off, gates, x2, gate2, tc, rows_per_batch):
    n, d = x2.shape
    sub = d // LANES
    tiles_per_batch = rows_per_batch // tc
    return pl.pallas_call(
        functools.partial(_combine_body, tc=tc, sub=sub),
        grid=(n // tc,),
        in_specs=[pl.BlockSpec((tc * TOP_K,), lambda i: (i,), memory_space=pltpu.SMEM),
                  pl.BlockSpec((tc, TOP_K), lambda i: (i, 0)), pl.BlockSpec((tc, d), lambda i: (i, 0)),
                  pl.BlockSpec(memory_space=pl.ANY),
                  pl.BlockSpec((None, 1, d), lambda i: (i // tiles_per_batch, 0, 0))],
        out_specs=pl.BlockSpec((tc, d), lambda i: (i, 0)),
        out_shape=jax.ShapeDtypeStruct((n, d), F32),
        scratch_shapes=[pltpu.VMEM((TOP_K * tc * sub, LANES), F32), pltpu.SemaphoreType.DMA(())],
        compiler_params=_cparams(("arbitrary",)),
        name="moe_combine",
    )(dest_off, gates, x2, ys, gate2)


def _moe_layer(streams, g, router_w, router_b, layer, w_gu, b_gu, w_dn, b_dn):
    tm = MOE_TM
    sub = streams[0][0].shape[1] // LANES
    routed = [_moe_route(x2, g, sc, sh, router_w, router_b, tr, rpb) for x2, sc, sh, _, rpb, tr, _ in streams]
    counts = [r[4][0].astype(jnp.int32) for r in routed]
    total = functools.reduce(lambda a, b: a + b, counts)
    padded = (total + tm - 1) // tm * tm
    pad_end = jnp.cumsum(padded)
    n_tok = sum(s[0].shape[0] for s in streams)
    n_blocks = -(-n_tok * TOP_K // tm) + N_EXPERTS
    n_used = (pad_end[-1] // tm).astype(jnp.int32)
    blk_ids = jnp.arange(n_blocks, dtype=jnp.int32)
    last_row = jnp.minimum(blk_ids, n_used - 1) * tm
    block_exp = jnp.sum((pad_end[None, :] <= last_row[:, None]).astype(jnp.int32), axis=1)
    block_exp = jnp.minimum(block_exp, N_EXPERTS - 1)
    start = pad_end - padded
    xs, dests = None, []
    for (x2, _, _, _, _, tr, _), (hx, top_i, _, pos, _), cnt in zip(streams, routed, counts):
        dest_off = ((start[top_i] + pos) * sub).astype(jnp.int32).reshape(-1)
        xs = _moe_dispatch(hx, dest_off, n_blocks * tm, tr, sub, xs)
        dests.append(dest_off)
        start = start + cnt
    ys = _moe_ffn(xs, block_exp, n_used.reshape(1), layer, w_gu, b_gu, w_dn, b_dn, tm)
    return [_moe_combine(ys, dest_off, r[2], x2, gate2, tc, rpb)
            for (x2, _, _, gate2, rpb, _, tc), r, dest_off in zip(streams, routed, dests)]


def _mod_body(c_ref, w_ref, b_ref, o_ref):
    o_ref[...] = jnp.dot(jax.nn.silu(c_ref[...]), w_ref[...], precision=HI, preferred_element_type=F32) + b_ref[...]


def _modulation(cc, mod_w, mod_b):
    depth, d, d6 = mod_w.shape
    r = cc.shape[0]
    return pl.pallas_call(
        _mod_body,
        grid=(depth, d6 // d),
        in_specs=[pl.BlockSpec((r, d), lambda l, j: (0, 0)), pl.BlockSpec((None, d, d), lambda l, j: (l, 0, j)),
                  pl.BlockSpec((None, 1, d), lambda l, j: (l, 0, j))],
        out_specs=pl.BlockSpec((None, r, d), lambda l, j: (l, 0, j)),
        out_shape=jax.ShapeDtypeStruct((depth, r, d6), F32),
        compiler_params=_cparams(("parallel", "parallel")),
        name="modulation",
    )(cc, mod_w, mod_b.reshape(depth, 1, d6))


def kernel(x, c, ctx, c_ctx, mod_w, mod_b, norm1_g, norm2_g, ab_w_in, ab_w_out, hy_short_w, hy_short_b, hy_f_w1, hy_f_b1, hy_f_w2, hy_f_b2, hy_f_w3, hy_bias, s5_lambda_re, s5_lambda_im, s5_log_dt, s5_b_re, s5_b_im, s5_c_re, s5_c_im, s5_d, s5_glu_w, s5_glu_b, cd_w_in, cd_w_out, ret_decay_logit, swa_sink, router_w, router_b, exp_w_gu, exp_b_gu, exp_w_down, exp_b_down, final_g):
    bsz, seq, d = x.shape
    seq_c = ctx.shape[1]
    depth = mod_w.shape[0]
    m_l, m_c = bsz * seq, bsz * seq_c
    tm, tm_c, tt = 512, 256, 256
    xl = x.reshape(m_l, d)
    xc = ctx.reshape(m_c, d)
    cc = jnp.concatenate([c, c_ctx[None], jnp.zeros((SUBLANES - bsz - 1, d), F32)], axis=0)
    mods = _modulation(cc, mod_w, mod_b)
    hy_w = hy_f_w3.shape[2]
    for layer in range(depth):
        with_ctx = layer < depth - 1
        i = layer // 2
        sh1, sc1, g1, sh2, sc2, g2 = [t[:, None, :] for t in jnp.split(mods[layer, :bsz], 6, axis=-1)]
        csh1, csc1, cg1, csh2, csc2, cg2 = [t[:, None, :] for t in jnp.split(mods[layer, bsz:bsz + 1], 6, axis=-1)]
        if layer % 2 == 0:
            w_in, w_out = ab_w_in[i], ab_w_out[i]
            splits = [(3 * hy_w, 0), (w_in.shape[1] - 3 * hy_w, 0)]
            pa, pb = _norm_mod_matmul(xl, norm1_g[layer], sc1, sh1, w_in, splits, tm, seq, "ab_in")
            pac, pbc = _norm_mod_matmul(xc, norm1_g[layer], csc1, csh1, w_in, splits, tm_c, m_c, "ab_in_ctx")
            hy = (hy_short_w[i], hy_short_b[i], hy_f_w1[i], hy_f_b1[i], hy_f_w2[i], hy_f_b2[i], hy_f_w3[i], hy_bias[i])
            ya = _hyena(pa, bsz, seq, *hy, tt)
            yb, ybc = _s5_mixer(pb, pbc, bsz, seq, seq_c, s5_lambda_re[i], s5_lambda_im[i], s5_log_dt[i], s5_b_re[i],
                                s5_b_im[i], s5_c_re[i], s5_c_im[i], s5_d[i], s5_glu_w[i], s5_glu_b[i], tm, tm_c)
            ws = [w_out[:hy_w], w_out[hy_w:]]
            xl = _out_proj([ya, yb], ws, xl, g1, tm, seq, "ab_out")
            if with_ctx:
                yac = _hyena(pac, bsz, seq_c, *hy, tt)
                xc = _out_proj([yac, ybc], ws, xc, cg1, tm_c, m_c, "ab_out_ctx")
        else:
            w_in, w_out = cd_w_in[i], cd_w_out[i]
            qk = RET_HEADS * (d // 16)
            vw = 2 * qk
            qw = SWA_Q_HEADS * (d // 16)
            kw = SWA_KV_HEADS * (d // 16)
            splits = [(qk, 0), (qk, 0), (vw, RET_HEADS), (vw, RET_HEADS), (qw, 0), (kw, 0), (kw, SWA_KV_HEADS)]
            names = ("rq", "rk", "rv", "rg", "sq", "sk", "sv")
            lat = dict(zip(names, _norm_mod_matmul(xl, norm1_g[layer], sc1, sh1, w_in, splits, tm, seq, "cd_in")))
            cx = dict(zip(names, _norm_mod_matmul(xc, norm1_g[layer], csc1, csh1, w_in, splits, tm_c, m_c, "cd_in_ctx")))
            ro_l, so_l, ro_c, so_c = _mixer_cd_core(lat, cx, bsz, seq, seq_c, ret_decay_logit[i], swa_sink[i],
                                                    with_ctx, tm, tm_c)
            ws = [w_out[:vw].reshape(RET_HEADS, vw // RET_HEADS, d), w_out[vw:].reshape(SWA_Q_HEADS, qw // SWA_Q_HEADS, d)]
            xl = _out_proj([ro_l, so_l], ws, xl, g1, tm, seq, "cd_out")
            if with_ctx:
                xc = _out_proj([ro_c, so_c], ws, xc, cg1, tm_c, m_c, "cd_out_ctx")
        moe_w = (router_w[layer], router_b[layer], layer, exp_w_gu, exp_b_gu, exp_w_down, exp_b_down)
        streams = [(xl, sc2, sh2, g2, seq, tm, tm_c)]
        if with_ctx:
            streams.append((xc, csc2, csh2, cg2, m_c, tm_c, tm_c))
        outs = _moe_layer(streams, norm2_g[layer], *moe_w)
        xl = outs[0]
        if with_ctx:
            xc = outs[1]
    out = _rows_call(_final_norm_body, [xl], [final_g.reshape(1, -1)], [], [((m_l, d), F32)], tm, m_l, "final_norm")[0]
    return out.reshape(bsz, seq, d)
```

```python
import functools
import math

import numpy as np
import jax
import jax.numpy as jnp
from jax import lax
from jax.experimental import pallas as pl
from jax.experimental.pallas import tpu as pltpu

F32 = jnp.float32
BF16 = jnp.bfloat16
HI = lax.Precision.HIGHEST

EPS = 1e-6
NEG_INF = -1e30
ROPE_BASE = 10000.0
GRID_W = 64

HY_SHORT = 3
HY_BANDS = 16
HY_SHIFT = 0.05
HY_FAST_DECAY = math.log(1e-2) / 0.3
HY_SLOW_DECAY = math.log(1e-2) / 1.5
S5_GROUP = 16
S5_STATE = 64
S5_CHUNK = 8
RET_HEADS = 4
RET_CHUNK = 128
SWA_Q_HEADS = 8
SWA_KV_HEADS = 2
SWA_WINDOW = 128
SWA_BLOCK = 128
N_EXPERTS = 32
TOP_K = 4
SWIGLU_LIMIT = 7.0
SWIGLU_ALPHA = 1.702

LANES = 128
SUBLANES = 8
VMEM_LIMIT = 52 * 2**20
FFT_B = 128
MOE_TM = 512
DMA_UNROLL = 4


def _cparams(sem):
    return pltpu.CompilerParams(dimension_semantics=sem, vmem_limit_bytes=VMEM_LIMIT)


def _rows_call(body, rows, consts, batched, outs, tm, rows_per_batch, name, periodic=()):
    m = rows[0].shape[-2]
    assert m % tm == 0 and rows_per_batch % tm == 0
    tiles_per_batch = rows_per_batch // tm

    def row_spec(shape):
        if len(shape) == 2:
            return pl.BlockSpec((tm, shape[1]), lambda i: (i, 0))
        return pl.BlockSpec((shape[0], tm, shape[2]), lambda i: (0, i, 0))

    in_specs = [row_spec(a.shape) for a in rows]
    for a in periodic:
        in_specs.append(pl.BlockSpec((tm, a.shape[1]), lambda i: (i % tiles_per_batch, 0)))
    for a in consts:
        in_specs.append(pl.BlockSpec(a.shape, lambda i, n=a.ndim: (0,) * n))
    for a in batched:
        in_specs.append(pl.BlockSpec((None, 1, a.shape[2]), lambda i: (i // tiles_per_batch, 0, 0)))
    out_specs = [row_spec(s) for s, _ in outs]
    out_shape = [jax.ShapeDtypeStruct(s, d) for s, d in outs]
    res = pl.pallas_call(
        body,
        grid=(m // tm,),
        in_specs=in_specs,
        out_specs=out_specs,
        out_shape=out_shape,
        compiler_params=_cparams(("parallel",)),
        name=name,
    )(*rows, *periodic, *consts, *batched)
    return res


def _norm_mod(x, g, sc, sh):
    y = x * lax.rsqrt(jnp.mean(x * x, axis=-1, keepdims=True) + EPS) * g
    return y * (1.0 + sc) + sh


def _rotate_half(x, cos, sin, shift):
    w = x.shape[1]
    lane = lax.broadcasted_iota(jnp.int32, x.shape, 1)
    partner = jnp.where(lane % (2 * shift) < shift, pltpu.roll(x, w - shift, axis=1), pltpu.roll(x, shift, axis=1))
    return x * cos + partner * sin


def _norm_mod_matmul_body(x_ref, *refs, splits, n_tab):
    tabs = refs[:2 * n_tab]
    g_ref, w_ref, sc_ref, sh_ref = refs[2 * n_tab:2 * n_tab + 4]
    o_refs = refs[2 * n_tab + 4:]
    h = _norm_mod(x_ref[...], g_ref[...], sc_ref[...], sh_ref[...])
    r = jnp.dot(h.astype(BF16), w_ref[...], preferred_element_type=F32)
    off = 0
    for o_ref, (n, heads, post) in zip(o_refs, splits):
        x = r[:, off:off + n]
        if post is not None:
            table, shift, scale = post
            if table is not None:
                x = _rotate_half(x, tabs[2 * table][...], tabs[2 * table + 1][...], shift)
            if scale != 1.0:
                x = x * scale
        if heads:
            hd = n // heads
            for h_i in range(heads):
                o_ref[h_i] = x[:, h_i * hd:(h_i + 1) * hd]
        else:
            o_ref[...] = x
        off += n


def _norm_mod_matmul(x2, g, sc, sh, w, splits, tm, rows_per_batch, name, tables=()):
    m = x2.shape[0]
    outs = [((heads, m, n // heads), F32) if heads else ((m, n), F32) for n, heads, _ in splits]
    body = functools.partial(_norm_mod_matmul_body, splits=splits, n_tab=len(tables))
    periodic = [t for pair in tables for t in pair]
    return _rows_call(body, [x2], [g.reshape(1, -1), w.astype(BF16)], [sc, sh], outs, tm, rows_per_batch, name,
                      periodic=periodic)


def _out_proj_body(*refs, n_in):
    a_refs = refs[:n_in]
    x_ref = refs[n_in]
    w_refs = refs[n_in + 1: 2 * n_in + 1]
    g_ref = refs[2 * n_in + 1]
    o_ref = refs[2 * n_in + 2]
    acc = None
    for a_ref, w_ref in zip(a_refs, w_refs):
        if a_ref.ndim == 3:
            for h_i in range(a_ref.shape[0]):
                t = jnp.dot(a_ref[h_i].astype(BF16), w_ref[h_i], preferred_element_type=F32)
                acc = t if acc is None else acc + t
        else:
            t = jnp.dot(a_ref[...].astype(BF16), w_ref[...], preferred_element_type=F32)
            acc = t if acc is None else acc + t
    o_ref[...] = x_ref[...] + g_ref[...] * acc


def _out_proj(parts, ws, x2, gate, tm, rows_per_batch, name):
    m, d = x2.shape
    body = functools.partial(_out_proj_body, n_in=len(parts))
    ws = [w.astype(BF16) for w in ws]
    return _rows_call(body, list(parts) + [x2], ws, [gate], [((m, d), F32)], tm, rows_per_batch, name)[0]


def _final_norm_body(x_ref, g_ref, o_ref):
    x = x_ref[...]
    o_ref[...] = x * lax.rsqrt(jnp.mean(x * x, axis=-1, keepdims=True) + EPS) * g_ref[...]


def _hy_prep_body(u_ref, p_ref, n_ref, w_ref, b_ref, x0_ref, z_ref, *, width):
    i = pl.program_id(1)
    last = pl.num_programs(1) - 1
    u = u_ref[...]
    tt = u.shape[0]
    prev_row = jnp.where(i == 0, 0.0, p_ref[SUBLANES - 1:SUBLANES, :])
    next_row = jnp.where(i == last, 0.0, n_ref[0:1, :])
    rows = lax.broadcasted_iota(jnp.int32, (tt, 1), 0)
    up = jnp.where(rows == 0, prev_row, pltpu.roll(u, 1, axis=0))
    dn = jnp.where(rows == tt - 1, next_row, pltpu.roll(u, tt - 1, axis=0))
    y = w_ref[0:1, :] * up + w_ref[1:2, :] * u + w_ref[2:3, :] * dn + b_ref[...]
    x0_ref[...] = y[:, :width]
    z_ref[...] = y[:, 2 * width:] * y[:, width:2 * width]


def _hyena_prep(p, short_w, short_b, bsz, seq, tt):
    w3 = p.shape[1]
    width = w3 // 3
    p3 = p.reshape(bsz, seq, w3)
    nt = seq // tt
    sub = tt // SUBLANES
    nsub = seq // SUBLANES
    body = functools.partial(_hy_prep_body, width=width)
    x0, z = pl.pallas_call(
        body,
        grid=(bsz, nt),
        in_specs=[
            pl.BlockSpec((None, tt, w3), lambda b, i: (b, i, 0)),
            pl.BlockSpec((None, SUBLANES, w3), lambda b, i: (b, jnp.maximum(i * sub - 1, 0), 0)),
            pl.BlockSpec((None, SUBLANES, w3), lambda b, i: (b, jnp.minimum((i + 1) * sub, nsub - 1), 0)),
            pl.BlockSpec((HY_SHORT, w3), lambda b, i: (0, 0)),
            pl.BlockSpec((1, w3), lambda b, i: (0, 0)),
        ],
        out_specs=[pl.BlockSpec((None, tt, width), lambda b, i: (b, i, 0))] * 2,
        out_shape=[jax.ShapeDtypeStruct((bsz, seq, width), F32)] * 2,
        compiler_params=_cparams(("parallel", "parallel")),
        name="hyena_prep",
    )(p3, p3, p3, short_w, short_b.reshape(1, -1))
    return x0, z


def _hy_filter_body(t_ref, w_ref, lag_ref, bands_ref, deltas_ref, w1t_ref, w1c_ref, w1s_ref, b1_ref,
                    w2_ref, b2_ref, w3_ref, h_ref, s_ref):
    i = pl.program_id(0)
    arg = w_ref[...] * bands_ref[...]
    pre = (t_ref[...] * w1t_ref[...]
           + jnp.dot(jnp.cos(arg), w1c_ref[...], precision=HI, preferred_element_type=F32)
           + jnp.dot(-jnp.sin(arg), w1s_ref[...], precision=HI, preferred_element_type=F32)
           + b1_ref[...])
    h1 = jnp.sin(pre)
    h2 = jnp.sin(jnp.dot(h1, w2_ref[...], precision=HI, preferred_element_type=F32) + b2_ref[...])
    h3 = jnp.dot(h2, w3_ref[...], precision=HI, preferred_element_type=F32)
    h = h3 * (jnp.exp(-lag_ref[...] * deltas_ref[...]) + HY_SHIFT)
    h_ref[...] = h

    @pl.when(i == 0)
    def _():
        s_ref[...] = jnp.zeros_like(s_ref)

    s_ref[...] += jnp.sum(jnp.abs(h), axis=0, keepdims=True)


def _hyena_filter(seq, w1, b1, w2, b2, w3):
    width = w3.shape[1]
    pos = jnp.arange(seq, dtype=F32)
    t = (pos / seq)[:, None]
    w = (2.0 * math.pi * pos / seq)[:, None]
    lag = (jnp.abs(pos - seq // 2) / (seq / 2))[:, None]
    bands = jnp.linspace(1e-4, HY_BANDS - 1, HY_BANDS, dtype=F32)[None]
    deltas = jnp.abs(jnp.linspace(HY_FAST_DECAY, HY_SLOW_DECAY, width, dtype=F32))[None]
    tl = min(seq, 1024)
    col = pl.BlockSpec((tl, 1), lambda i: (i, 0))

    def whole(a):
        return pl.BlockSpec(a.shape, lambda i, n=a.ndim: (0,) * n)

    consts = [bands, deltas, w1[0:1], w1[1:1 + HY_BANDS], w1[1 + HY_BANDS:], b1.reshape(1, -1),
              w2, b2.reshape(1, -1), w3]
    h, s = pl.pallas_call(
        _hy_filter_body,
        grid=(seq // tl,),
        in_specs=[col, col, col] + [whole(a) for a in consts],
        out_specs=[pl.BlockSpec((tl, width), lambda i: (i, 0)), pl.BlockSpec((1, width), lambda i: (0, 0))],
        out_shape=[jax.ShapeDtypeStruct((seq, width), F32), jax.ShapeDtypeStruct((1, width), F32)],
        compiler_params=_cparams(("arbitrary",)),
        name="hyena_filter",
    )(t, w, lag, *consts)
    return h, s


FFT_G = SUBLANES
FFT_CW = 512


def _kron_eye(mat):
    return np.kron(mat, np.eye(FFT_G))


def _dft_tables_k(na, ka, a_lo, a_cnt):
    n = na * FFT_B
    a = np.arange(na)
    b = np.arange(FFT_B)
    ang1 = 2.0 * np.pi * np.outer(a, a) / na
    ang2 = 2.0 * np.pi * np.outer(b, b) / FFT_B
    angt = 2.0 * np.pi * np.outer(b, a) / n
    c1, s1 = np.cos(ang1), np.sin(ang1)
    c2, s2 = np.cos(ang2), np.sin(ang2)
    nbb = FFT_B // FFT_G
    tw1 = angt.reshape(nbb, FFT_G, na).transpose(0, 2, 1).reshape(nbb, na * FFT_G, 1)
    rows = slice(a_lo, a_lo + a_cnt)
    tabs = dict(
        m1=_kron_eye(np.concatenate([c1, -s1], axis=0)[:, :ka]),
        f2=np.block([[c2, s2], [-s2, c2]]),
        f2i=np.block([[c2, -s2], [s2, c2]]),
        m3r=_kron_eye(c1[rows] / n), m3i=_kron_eye(-s1[rows] / n),
        tw1c=np.cos(tw1), tw1s=np.sin(tw1),
        twc_c=np.cos(angt).T[:, :, None], tws_c=np.sin(angt).T[:, :, None],
    )
    return {k: jnp.asarray(v, F32) for k, v in tabs.items()}


def _fftk1_body(x_ref, m_ref, tc_ref, ts_ref, sc_ref, o_ref, *, na):
    ka, g, cw = x_ref.shape
    v = (x_ref[...].reshape(ka * g, cw) * sc_ref[...]).astype(BF16)
    r = jnp.dot(m_ref[...], v, preferred_element_type=F32)
    gr, gi = r[:na * g], r[na * g:]
    tc, ts = tc_ref[...], ts_ref[...]
    o_ref[...] = jnp.concatenate([gr * tc + gi * ts, gi * tc - gr * ts], axis=0).reshape(2 * na, g, cw)


def _fftk_stage1(x5, scale, tabs, na):
    bz, ka, nbb, g, ch = x5.shape
    cw = min(ch, FFT_CW)
    m1 = tabs["m1"].astype(BF16)
    return pl.pallas_call(
        functools.partial(_fftk1_body, na=na),
        grid=(bz, ch // cw, nbb),
        in_specs=[
            pl.BlockSpec((None, ka, None, g, cw), lambda z, q, b: (z, 0, b, 0, q)),
            pl.BlockSpec(m1.shape, lambda z, q, b: (0, 0)),
            pl.BlockSpec((None, na * g, 1), lambda z, q, b: (b, 0, 0)),
            pl.BlockSpec((None, na * g, 1), lambda z, q, b: (b, 0, 0)),
            pl.BlockSpec((1, cw), lambda z, q, b: (0, q)),
        ],
        out_specs=pl.BlockSpec((None, 2 * na, None, g, cw), lambda z, q, b: (z, 0, b, 0, q)),
        out_shape=jax.ShapeDtypeStruct((bz, 2 * na, nbb, g, ch), F32),
        compiler_params=_cparams(("parallel", "parallel", "parallel")),
        name="fft_stage1",
    )(x5, m1, tabs["tw1c"], tabs["tw1s"], scale)


def _fftk2_spec_body(ar_ref, ai_ref, f_ref, o_ref):
    f = f_ref[...].astype(BF16)
    cw = ar_ref.shape[-1]
    for j in range(FFT_G):
        v = jnp.concatenate([ar_ref[j].reshape(FFT_B, cw), ai_ref[j].reshape(FFT_B, cw)], axis=0).astype(BF16)
        o_ref[j] = jnp.dot(f, v, preferred_element_type=F32)


def _fftk2_conv_body(ar_ref, ai_ref, h_ref, f_ref, fi_ref, tc_ref, ts_ref, o_ref):
    f = f_ref[...].astype(BF16)
    fi = fi_ref[...].astype(BF16)
    cw = ar_ref.shape[-1]
    nbb = FFT_B // FFT_G
    for j in range(FFT_G):
        v = jnp.concatenate([ar_ref[j].reshape(FFT_B, cw), ai_ref[j].reshape(FFT_B, cw)], axis=0).astype(BF16)
        x = jnp.dot(f, v, preferred_element_type=F32)
        xr, xi = x[:FFT_B], x[FFT_B:]
        hr, hi = h_ref[j, :FFT_B, :], h_ref[j, FFT_B:, :]
        p = jnp.concatenate([xr * hr - xi * hi, xr * hi + xi * hr], axis=0).astype(BF16)
        q = jnp.dot(fi, p, preferred_element_type=F32)
        qr, qi = q[:FFT_B], q[FFT_B:]
        tc, ts = tc_ref[j], ts_ref[j]
        o_ref[j, 0] = (qr * tc - qi * ts).reshape(nbb, FFT_G, cw)
        o_ref[j, 1] = (qi * tc + qr * ts).reshape(nbb, FFT_G, cw)


def _fftk_stage2_spectrum(a5, tabs, na):
    _, _, nbb, g, ch = a5.shape
    cw = min(ch, FFT_CW)
    ng = na // FFT_G
    return pl.pallas_call(
        _fftk2_spec_body,
        grid=(ch // cw, ng),
        in_specs=[
            pl.BlockSpec((None, FFT_G, nbb, g, cw), lambda q, c: (0, c, 0, 0, q)),
            pl.BlockSpec((None, FFT_G, nbb, g, cw), lambda q, c: (0, ng + c, 0, 0, q)),
            pl.BlockSpec((2 * FFT_B, 2 * FFT_B), lambda q, c: (0, 0)),
        ],
        out_specs=pl.BlockSpec((FFT_G, 2 * FFT_B, cw), lambda q, c: (c, 0, q)),
        out_shape=jax.ShapeDtypeStruct((na, 2 * FFT_B, ch), F32),
        compiler_params=_cparams(("parallel", "parallel")),
        name="fft_stage2_spectrum",
    )(a5, a5, tabs["f2"])


def _fftk_stage2_conv(a5, hspec, tabs, na):
    bz, _, nbb, g, ch = a5.shape
    cw = min(ch, FFT_CW)
    ng = na // FFT_G
    return pl.pallas_call(
        _fftk2_conv_body,
        grid=(ch // cw, ng, bz),
        in_specs=[
            pl.BlockSpec((None, FFT_G, nbb, g, cw), lambda q, c, z: (z, c, 0, 0, q)),
            pl.BlockSpec((None, FFT_G, nbb, g, cw), lambda q, c, z: (z, ng + c, 0, 0, q)),
            pl.BlockSpec((FFT_G, 2 * FFT_B, cw), lambda q, c, z: (c, 0, q)),
            pl.BlockSpec((2 * FFT_B, 2 * FFT_B), lambda q, c, z: (0, 0)),
            pl.BlockSpec((2 * FFT_B, 2 * FFT_B), lambda q, c, z: (0, 0)),
            pl.BlockSpec((FFT_G, FFT_B, 1), lambda q, c, z: (c, 0, 0)),
            pl.BlockSpec((FFT_G, FFT_B, 1), lambda q, c, z: (c, 0, 0)),
        ],
        out_specs=pl.BlockSpec((None, None, FFT_G, 2, nbb, g, cw), lambda q, c, z: (z, c, 0, 0, 0, 0, q)),
        out_shape=jax.ShapeDtypeStruct((bz, ng, FFT_G, 2, nbb, g, ch), F32),
        compiler_params=_cparams(("parallel", "parallel", "parallel")),
        name="fft_stage2_conv",
    )(a5, a5, hspec, tabs["f2"], tabs["f2i"], tabs["twc_c"], tabs["tws_c"])


def _fftk3_body(*refs, gate):
    if gate:
        br_ref, bi_ref, mr_ref, mi_ref, x0_ref, z_ref, bias_ref, o_ref = refs
    else:
        br_ref, bi_ref, mr_ref, mi_ref, o_ref = refs
    ng, gc, g, cw = br_ref.shape
    vr = br_ref[...].reshape(ng * gc * g, cw).astype(BF16)
    vi = bi_ref[...].reshape(ng * gc * g, cw).astype(BF16)
    y = jnp.dot(mr_ref[...], vr, preferred_element_type=F32) + jnp.dot(mi_ref[...], vi, preferred_element_type=F32)
    a_cnt = o_ref.shape[0]
    if gate:
        x0 = x0_ref[...].reshape(a_cnt * g, cw)
        z = z_ref[...].reshape(a_cnt * g, cw)
        y = x0 * (y + bias_ref[...] * z)
    o_ref[...] = y.reshape(a_cnt, g, cw)


def _fftk_stage3(b7, tabs, na, a_cnt, gate_args=None):
    bz, ng, gc, _, nbb, g, ch = b7.shape
    cw = min(ch, FFT_CW)
    mr, mi = tabs["m3r"].astype(BF16), tabs["m3i"].astype(BF16)
    row_spec = pl.BlockSpec((None, a_cnt, None, g, cw), lambda z, q, b: (z, 0, b, 0, q))
    in_specs = [
        pl.BlockSpec((None, ng, gc, None, None, g, cw), lambda z, q, b: (z, 0, 0, 0, b, 0, q)),
        pl.BlockSpec((None, ng, gc, None, None, g, cw), lambda z, q, b: (z, 0, 0, 1, b, 0, q)),
        pl.BlockSpec(mr.shape, lambda z, q, b: (0, 0)),
        pl.BlockSpec(mi.shape, lambda z, q, b: (0, 0)),
    ]
    args = [b7, b7, mr, mi]
    if gate_args is not None:
        in_specs += [row_spec, row_spec, pl.BlockSpec((1, cw), lambda z, q, b: (0, q))]
        args += list(gate_args)
    return pl.pallas_call(
        functools.partial(_fftk3_body, gate=gate_args is not None),
        grid=(bz, ch // cw, nbb),
        in_specs=in_specs,
        out_specs=row_spec,
        out_shape=jax.ShapeDtypeStruct((bz, a_cnt, nbb, g, ch), F32),
        compiler_params=_cparams(("parallel", "parallel", "parallel")),
        name="fft_stage3",
    )(*args)


def _hy_gate_body(x0_ref, y_ref, z_ref, b_ref, o_ref):
    o_ref[...] = x0_ref[...] * (y_ref[...] + b_ref[...] * z_ref[...])


def _hyena(p, bsz, seq, short_w, short_b, w1, b1, w2, b2, w3, hy_bias, tt):
    width = w3.shape[1]
    x0, z = _hyena_prep(p, short_w, short_b, bsz, seq, tt)
    hu, hs = _hyena_filter(seq, w1, b1, w2, b2, w3)
    na = max(2 * seq // FFT_B, 16)
    ka = max(seq // FFT_B, 16)
    pad = ka * FFT_B - seq
    a_lo, a_cnt = (0, na) if pad else (seq // 2 // FFT_B, ka)
    tabs = _dft_tables_k(na, ka, a_lo, a_cnt)
    nbb = FFT_B // FFT_G

    def rows5(a, lead):
        if pad:
            a = jnp.pad(a, ((0, 0), (0, pad), (0, 0)))
        return a.reshape(lead, ka, nbb, FFT_G, width)

    ones = jnp.ones((1, width), F32)
    bias = hy_bias.reshape(1, -1)
    hspec = _fftk_stage2_spectrum(_fftk_stage1(rows5(hu[None], 1), 1.0 / hs, tabs, na), tabs, na)
    a5 = _fftk_stage1(rows5(z, bsz), ones, tabs, na)
    b7 = _fftk_stage2_conv(a5, hspec, tabs, na)
    m = bsz * seq
    if not pad:
        out = _fftk_stage3(b7, tabs, na, a_cnt, (rows5(x0, bsz), rows5(z, bsz), bias))
        return out.reshape(m, width)
    y = _fftk_stage3(b7, tabs, na, a_cnt).reshape(bsz, na * FFT_B, width)[:, seq // 2: seq // 2 + seq]
    return _rows_call(_hy_gate_body, [x0.reshape(m, width), y.reshape(m, width), z.reshape(m, width)],
                      [bias], [], [((m, width), F32)], tt, seq, "hyena_gate")[0]


def _s5_tables(lam_re, lam_im, log_dt, b_re, b_im, c_re, c_im, nsteps):
    t_len, hdim = S5_CHUNK, S5_GROUP
    lam = lax.complex(jnp.minimum(lam_re.astype(F32), -1e-4), lam_im.astype(F32))
    dt = jnp.exp(log_dt.astype(F32))[..., None]
    lam_dt = lam * dt
    lam_bar = jnp.exp(lam_dt)
    b_bar = ((lam_bar - 1.0) / lam)[..., None] * lax.complex(b_re.astype(F32), b_im.astype(F32))
    cm = lax.complex(c_re.astype(F32), c_im.astype(F32))
    ks = jnp.arange(t_len + 1, dtype=F32)
    pw = jnp.exp(ks[:, None, None, None] * lam_dt[None])
    g = lam.shape[1]
    tabs = {}
    pin = [pw[:t_len, 0][::-1], pw[:t_len, 1]]
    pout = [pw[1:, 0], pw[1:, 1][::-1]]
    for d in range(2):
        win = pin[d][:, :, :, None] * b_bar[d][None]
        win = jnp.transpose(win, (1, 0, 3, 2)).reshape(g, t_len * hdim, -1)
        tabs[f"win{d}"] = jnp.concatenate([jnp.real(win), jnp.imag(win)], axis=-1)
        wout = pout[d][:, :, None, :] * cm[d][None]
        wout = jnp.transpose(wout, (1, 3, 0, 2)).reshape(g, -1, t_len * hdim)
        tabs[f"wout{d}"] = jnp.concatenate([jnp.real(wout), -jnp.imag(wout)], axis=1)
        mu = jnp.exp((t_len * 2.0 ** jnp.arange(nsteps, dtype=F32))[:, None, None] * lam_dt[d][None])
        mr, mi = jnp.real(mu), jnp.imag(mu)
        tabs[f"m1{d}"] = jnp.concatenate([mr, mr], axis=-1)[:, :, None, :]
        tabs[f"m2{d}"] = jnp.concatenate([-mi, mi], axis=-1)[:, :, None, :]
        mu1 = jnp.exp(t_len * lam_dt[d])
        tabs[f"mu1{d}"] = jnp.concatenate([jnp.real(mu1), jnp.real(mu1)], axis=-1)[:, None, :]
        tabs[f"mu2{d}"] = jnp.concatenate([-jnp.imag(mu1), jnp.imag(mu1)], axis=-1)[:, None, :]
    kern = [jnp.real(jnp.einsum("ghp,tgp,gpk->tghk", cm[d], pw[:t_len, d], b_bar[d], precision=HI)) for d in range(2)]
    s_idx = jnp.arange(t_len)[:, None]
    t_idx = jnp.arange(t_len)[None, :]
    fwd = jnp.where((t_idx >= s_idx)[:, :, None, None, None], kern[0][jnp.maximum(t_idx - s_idx, 0)], 0.0)
    bwd = jnp.where((s_idx >= t_idx)[:, :, None, None, None], kern[1][jnp.maximum(s_idx - t_idx, 0)], 0.0)
    d0 = fwd + bwd
    tabs["d0"] = jnp.transpose(d0, (2, 0, 4, 1, 3)).reshape(g, t_len * hdim, t_len * hdim)
    return tabs


S5_GPB = LANES // S5_GROUP


def _s5_perm_table():
    p = np.zeros((S5_CHUNK, S5_GPB, LANES, LANES), np.float32)
    for t in range(S5_CHUNK):
        for g in range(S5_GPB):
            for h in range(S5_GROUP):
                p[t, g, S5_GROUP * g + h, S5_GROUP * t + h] = 1.0
    return jnp.asarray(p)


def _s5_body(u_ref, perm_ref, permt_ref, d0_ref, win0_ref, win1_ref, wout0_ref, wout1_ref, m10_ref, m20_ref, m11_ref,
             m21_ref, mu10_ref, mu20_ref, mu11_ref, mu21_ref, s0_ref, s1_ref, y_ref, f0_ref, f1_ref, *, nc, nsteps):
    half = S5_STATE
    t_len = S5_CHUNK
    j = lax.broadcasted_iota(jnp.int32, (nc, 1), 0)

    def swap(a):
        return pltpu.roll(a, half, axis=1)

    xs = [u_ref[pl.ds(t, nc, stride=t_len), :].astype(BF16) for t in range(t_len)]
    ys = []
    for g in range(S5_GPB):
        u = None
        for t in range(t_len):
            part = jnp.dot(xs[t], perm_ref[t, g].astype(BF16), preferred_element_type=F32)
            u = part if u is None else u + part
        u = u.astype(BF16)
        e0 = jnp.dot(u, win0_ref[g].astype(BF16), preferred_element_type=F32)
        e1 = jnp.dot(u, win1_ref[g].astype(BF16), preferred_element_type=F32)
        s_f = jnp.where(j == 0, s0_ref[g:g + 1, :], pltpu.roll(e0, 1, axis=0))
        s_b = jnp.where(j == nc - 1, s1_ref[g:g + 1, :], pltpu.roll(e1, nc - 1, axis=0))
        for k in range(nsteps):
            step = 2 ** k
            sh = jnp.where(j >= step, pltpu.roll(s_f, step, axis=0), 0.0)
            s_f = s_f + m10_ref[k, g] * sh + m20_ref[k, g] * swap(sh)
            sh = jnp.where(j < nc - step, pltpu.roll(s_b, nc - step, axis=0), 0.0)
            s_b = s_b + m11_ref[k, g] * sh + m21_ref[k, g] * swap(sh)
        y = jnp.dot(u, d0_ref[g].astype(BF16), preferred_element_type=F32)
        y = y + jnp.dot(s_f.astype(BF16), wout0_ref[g].astype(BF16), preferred_element_type=F32)
        y = y + jnp.dot(s_b.astype(BF16), wout1_ref[g].astype(BF16), preferred_element_type=F32)
        y_hi = y.astype(BF16)
        ys.append((y_hi, (y - y_hi.astype(F32)).astype(BF16)))
        t_f = mu10_ref[g] * s_f + mu20_ref[g] * swap(s_f) + e0
        t_b = mu11_ref[g] * s_b + mu21_ref[g] * swap(s_b) + e1
        f0_ref[g:g + 1, :] = t_f[nc - 1:nc, :]
        f1_ref[g:g + 1, :] = t_b[0:1, :]
    for t in range(t_len):
        out = None
        for g in range(S5_GPB):
            pt = permt_ref[t, g].astype(BF16)
            part = jnp.dot(ys[g][0], pt, preferred_element_type=F32) + jnp.dot(ys[g][1], pt, preferred_element_type=F32)
            out = part if out is None else out + part
        y_ref[pl.ds(t, nc, stride=t_len), :] = out


def _s5_core(u2, bsz, seq, tabs, init0, init1):
    m, width = u2.shape
    g = width // S5_GROUP
    nq = width // LANES
    t_len = S5_CHUNK
    nc = seq // t_len
    cols = t_len * S5_GROUP
    nsteps = max(1, math.ceil(math.log2(nc)))
    assert nsteps <= tabs["m10"].shape[0] and cols == LANES
    perm = _s5_perm_table()
    permt = jnp.swapaxes(perm, 2, 3)

    def whole(a):
        return pl.BlockSpec(a.shape, lambda q, b, nd=a.ndim: (0,) * nd)

    def per_q(shape):
        return pl.BlockSpec((S5_GPB,) + shape, lambda q, b: (q,) + (0,) * len(shape))

    p2 = 2 * S5_STATE
    steps_spec = pl.BlockSpec((nsteps, S5_GPB, 1, p2), lambda q, b: (0, q, 0, 0))
    state_spec = pl.BlockSpec((None, S5_GPB, p2), lambda q, b: (b, q, 0))
    seq_spec = pl.BlockSpec((seq, LANES), lambda q, b: (b, q))
    in_specs = [seq_spec, whole(perm), whole(permt), per_q((cols, cols)), per_q((cols, p2)), per_q((cols, p2)),
                per_q((p2, cols)), per_q((p2, cols))] + [steps_spec] * 4 + [per_q((1, p2))] * 4 + [state_spec] * 2
    state_shape = jax.ShapeDtypeStruct((bsz, g, p2), F32)
    y, f0, f1 = pl.pallas_call(
        functools.partial(_s5_body, nc=nc, nsteps=nsteps),
        grid=(nq, bsz),
        in_specs=in_specs,
        out_specs=[seq_spec, state_spec, state_spec],
        out_shape=[jax.ShapeDtypeStruct((m, width), F32), state_shape, state_shape],
        compiler_params=_cparams(("parallel", "parallel")),
        name="s5_scan",
    )(u2, perm, permt, tabs["d0"], tabs["win0"], tabs["win1"], tabs["wout0"], tabs["wout1"],
      tabs["m10"][:nsteps], tabs["m20"][:nsteps], tabs["m11"][:nsteps], tabs["m21"][:nsteps],
      tabs["mu10"], tabs["mu20"], tabs["mu11"], tabs["mu21"], init0, init1)
    return y, f0, f1


def _s5_glu_body(u_ref, y_ref, d_ref, w_ref, b_ref, o_ref):
    y = d_ref[...] * u_ref[...] + y_ref[...]
    g = jax.nn.gelu(y)
    o_ref[...] = g * jax.nn.sigmoid(jnp.dot(g.astype(BF16), w_ref[...], preferred_element_type=F32) + b_ref[...])


def _s5_glu(u2, y2, d_skip, glu_w, glu_b, tm, name):
    m, width = u2.shape
    return _rows_call(_s5_glu_body, [u2, y2], [d_skip.reshape(1, -1), glu_w.astype(BF16), glu_b.reshape(1, -1)], [],
                      [((m, width), F32)], tm, m, name)[0]


def _s5_mixer(u_lat, u_ctx, bsz, seq, seq_c, lam_re, lam_im, log_dt, b_re, b_im, c_re, c_im, d_skip, glu_w, glu_b,
              tm, tm_c):
    nsteps = max(1, math.ceil(math.log2(seq // S5_CHUNK)))
    tabs = _s5_tables(lam_re, lam_im, log_dt, b_re, b_im, c_re, c_im, nsteps)
    g = u_lat.shape[1] // S5_GROUP
    zero = jnp.zeros((bsz, g, 2 * S5_STATE), F32)
    y_ctx, f0, f1 = _s5_core(u_ctx, bsz, seq_c, tabs, zero, zero)
    y_lat, _, _ = _s5_core(u_lat, bsz, seq, tabs, f0, f1)
    out_lat = _s5_glu(u_lat, y_lat, d_skip, glu_w, glu_b, tm, "s5_glu")
    out_ctx = _s5_glu(u_ctx, y_ctx, d_skip, glu_w, glu_b, tm_c, "s5_glu_ctx")
    return out_lat, out_ctx


def _rope_tables_1d(seq, hd, heads):
    half = hd // 2
    inv = ROPE_BASE ** (-jnp.arange(half, dtype=F32) / half)
    ang = jnp.arange(seq, dtype=F32)[:, None] * inv[None]
    cos, sin = jnp.cos(ang), jnp.sin(ang)
    return jnp.tile(jnp.concatenate([cos, cos], -1), (1, heads)), jnp.tile(jnp.concatenate([-sin, sin], -1), (1, heads))


def _rope_tables_2d(seq, hd, heads):
    q = hd // 4
    inv = ROPE_BASE ** (-jnp.arange(q, dtype=F32) / q)
    n_rows = seq // GRID_W
    row = jnp.repeat(jnp.arange(n_rows, dtype=F32), GRID_W)
    col = jnp.tile(jnp.arange(GRID_W, dtype=F32), n_rows)
    ar, ac = row[:, None] * inv[None], col[:, None] * inv[None]
    cos = jnp.concatenate([jnp.cos(ar), jnp.cos(ar), jnp.cos(ac), jnp.cos(ac)], -1)
    sin = jnp.concatenate([-jnp.sin(ar), jnp.sin(ar), -jnp.sin(ac), jnp.sin(ac)], -1)
    return jnp.tile(cos, (1, heads)), jnp.tile(sin, (1, heads))


def _kv_update(k, kdec, v):
    kd = (k * kdec).astype(BF16)
    return lax.dot_general(kd, v.astype(BF16), (((0,), (0,)), ((), ())), preferred_element_type=F32)


def _ret_bwd_body(k_ref, v_ref, kdec_ref, gc_ref, init_ref, sprev_ref, fin_ref, s_ref):
    @pl.when(pl.program_id(1) == 0)
    def _():
        s_ref[...] = init_ref[...]

    for h in range(k_ref.shape[0]):
        s = s_ref[h]
        sprev_ref[h] = s
        s = gc_ref[h] * s + _kv_update(k_ref[h], kdec_ref[h], v_ref[h])
        s_ref[h] = s
        fin_ref[h] = s


def _ret_main_body(q_ref, k_ref, v_ref, g_ref, mask_ref, qdf_ref, kdf_ref, qdb_ref, gc_ref, init_ref, sb_ref,
                   o_ref, fin_ref, s_ref):
    @pl.when(pl.program_id(1) == 0)
    def _():
        s_ref[...] = init_ref[...]

    for h in range(q_ref.shape[0]):
        q, k, v = q_ref[h], k_ref[h], v_ref[h]
        vb = v.astype(BF16)
        sc = lax.dot_general(q.astype(BF16), k.astype(BF16), (((1,), (1,)), ((), ())), preferred_element_type=F32)
        o = jnp.dot((sc * mask_ref[h]).astype(BF16), vb, preferred_element_type=F32)
        s = s_ref[h]
        o = o + jnp.dot((q * qdf_ref[h]).astype(BF16), s.astype(BF16), preferred_element_type=F32)
        o = o + jnp.dot((q * qdb_ref[h]).astype(BF16), sb_ref[h].astype(BF16), preferred_element_type=F32)
        o = o * lax.rsqrt(jnp.mean(o * o, axis=-1, keepdims=True) + EPS)
        o_ref[h] = o * jax.nn.silu(g_ref[h])
        s = gc_ref[h] * s + _kv_update(k, kdf_ref[h], v)
        s_ref[h] = s
        fin_ref[h] = s


def _ret_tables(decay_logit):
    cl = RET_CHUNK
    log_g = jax.nn.log_sigmoid(decay_logit.astype(F32))
    idx = jnp.arange(cl, dtype=F32)
    diff = idx[:, None] - idx[None, :]
    mf = jnp.where(diff[None] >= 0, jnp.exp(jnp.maximum(diff, 0.0)[None] * log_g[0][:, None, None]), 0.0)
    mb = jnp.where(diff[None] <= 0, jnp.exp(jnp.maximum(-diff, 0.0)[None] * log_g[1][:, None, None]), 0.0)

    def col(e, d):
        return jnp.exp(e[None, :] * log_g[d][:, None])[:, :, None]

    return dict(mask=mf + mb, qdf=col(idx + 1.0, 0), kdf=col(cl - 1.0 - idx, 0), qdb=col(cl - idx, 1), kdb=col(idx, 1),
                gcf=jnp.exp(cl * log_g[0])[:, None, None], gcb=jnp.exp(cl * log_g[1])[:, None, None])


def _retention(q, k, v, g, bsz, seq, tabs, init_f, init_b):
    heads, _, dk = q.shape
    dv = v.shape[2]
    cl = RET_CHUNK
    n = seq // cl

    def whole(a):
        return pl.BlockSpec(a.shape, lambda b, i, nd=a.ndim: (0,) * nd)

    state_spec = pl.BlockSpec((None, heads, dk, dv), lambda b, i: (b, 0, 0, 0))
    state_shape = jax.ShapeDtypeStruct((bsz, heads, dk, dv), F32)
    rev = lambda b, i: (0, b * n + (n - 1 - i), 0)
    sprev_b, fin_b = pl.pallas_call(
        _ret_bwd_body,
        grid=(bsz, n),
        in_specs=[pl.BlockSpec((heads, cl, dk), rev), pl.BlockSpec((heads, cl, dv), rev),
                  whole(tabs["kdb"]), whole(tabs["gcb"]), state_spec],
        out_specs=[pl.BlockSpec((None, None, heads, dk, dv), lambda b, i: (b, n - 1 - i, 0, 0, 0)), state_spec],
        out_shape=[jax.ShapeDtypeStruct((bsz, n, heads, dk, dv), F32), state_shape],
        scratch_shapes=[pltpu.VMEM((heads, dk, dv), F32)],
        compiler_params=_cparams(("parallel", "arbitrary")),
        name="retention_backward_states",
    )(k, v, tabs["kdb"], tabs["gcb"], init_b)
    fwd = lambda b, i: (0, b * n + i, 0)
    o, fin_f = pl.pallas_call(
        _ret_main_body,
        grid=(bsz, n),
        in_specs=[pl.BlockSpec((heads, cl, dk), fwd), pl.BlockSpec((heads, cl, dk), fwd),
                  pl.BlockSpec((heads, cl, dv), fwd), pl.BlockSpec((heads, cl, dv), fwd),
                  whole(tabs["mask"]), whole(tabs["qdf"]), whole(tabs["kdf"]), whole(tabs["qdb"]), whole(tabs["gcf"]),
                  state_spec, pl.BlockSpec((None, None, heads, dk, dv), lambda b, i: (b, i, 0, 0, 0))],
        out_specs=[pl.BlockSpec((heads, cl, dv), fwd), state_spec],
        out_shape=[jax.ShapeDtypeStruct((heads, bsz * seq, dv), F32), state_shape],
        scratch_shapes=[pltpu.VMEM((heads, dk, dv), F32)],
        compiler_params=_cparams(("parallel", "arbitrary")),
        name="retention",
    )(q, k, v, g, tabs["mask"], tabs["qdf"], tabs["kdf"], tabs["qdb"], tabs["gcf"], init_f, sprev_b)
    return o, fin_f, fin_b


def _swa_body(*refs, local, seq, scale):
    if local:
        q_ref, kp_ref, kc_ref, kn_ref, vp_ref, vc_ref, vn_ref, kx_ref, vx_ref, sink_ref, o_ref = refs
    else:
        q_ref, kx_ref, vx_ref, sink_ref, o_ref = refs
    i = pl.program_id(1)
    bk = q_ref.shape[1]
    nkv = kx_ref.shape[0]
    grp = q_ref.shape[0] // nkv
    nt = (((1,), (1,)), ((), ()))
    hd = q_ref.shape[2]
    rows = grp * bk
    if local:
        row = lax.broadcasted_iota(jnp.int32, (rows, 3 * bk), 0) % bk
        col = lax.broadcasted_iota(jnp.int32, (rows, 3 * bk), 1)
        key_pos = (i - 1) * bk + col
        valid = (jnp.abs(col - (row + bk)) <= SWA_WINDOW) & (key_pos >= 0) & (key_pos < seq)
    for kv in range(nkv):
        kx = kx_ref[kv].astype(BF16)
        vx = vx_ref[kv].astype(BF16)
        q = q_ref[kv * grp:(kv + 1) * grp].reshape(rows, hd).astype(BF16)
        sink = jnp.concatenate([jnp.broadcast_to(sink_ref[kv * grp + gi], (bk, 1)) for gi in range(grp)], axis=0)
        s_x = lax.dot_general(q, kx, nt, preferred_element_type=F32) * scale
        m = jnp.maximum(jnp.max(s_x, axis=-1, keepdims=True), sink)
        if local:
            kl = jnp.concatenate([kp_ref[kv], kc_ref[kv], kn_ref[kv]], axis=0).astype(BF16)
            vl = jnp.concatenate([vp_ref[kv], vc_ref[kv], vn_ref[kv]], axis=0).astype(BF16)
            s_l = lax.dot_general(q, kl, nt, preferred_element_type=F32) * scale
            s_l = jnp.where(valid, s_l, NEG_INF)
            m = jnp.maximum(m, jnp.max(s_l, axis=-1, keepdims=True))
        p_x = jnp.exp(s_x - m)
        den = jnp.sum(p_x, axis=-1, keepdims=True) + jnp.exp(sink - m)
        o = jnp.dot(p_x.astype(BF16), vx, preferred_element_type=F32)
        if local:
            p_l = jnp.exp(s_l - m)
            den = den + jnp.sum(p_l, axis=-1, keepdims=True)
            o = o + jnp.dot(p_l.astype(BF16), vl, preferred_element_type=F32)
        o_ref[kv * grp:(kv + 1) * grp] = (o / den).reshape(grp, bk, hd)


def _swa(q, k, v, kx, vx, sink, bsz, seq, seq_c, local):
    hq, m, hd = q.shape
    hkv = kx.shape[0]
    bk = SWA_BLOCK
    nb = seq // bk
    scale = hd ** -0.5
    cur = lambda b, i: (0, b * nb + i, 0)
    prv = lambda b, i: (0, b * nb + jnp.maximum(i - 1, 0), 0)
    nxt = lambda b, i: (0, b * nb + jnp.minimum(i + 1, nb - 1), 0)
    ctx_spec = pl.BlockSpec((hkv, seq_c, hd), lambda b, i: (0, b, 0))
    sink_spec = pl.BlockSpec((hq, 1, 1), lambda b, i: (0, 0, 0))
    kvb = lambda f: pl.BlockSpec((hkv, bk, hd), f)
    in_specs = [pl.BlockSpec((hq, bk, hd), cur)]
    args = [q]
    if local:
        in_specs += [kvb(prv), kvb(cur), kvb(nxt), kvb(prv), kvb(cur), kvb(nxt)]
        args += [k, k, k, v, v, v]
    in_specs += [ctx_spec, ctx_spec, sink_spec]
    args += [kx, vx, sink.astype(F32).reshape(hq, 1, 1)]
    return pl.pallas_call(
        functools.partial(_swa_body, local=local, seq=seq, scale=scale),
        grid=(bsz, nb),
        in_specs=in_specs,
        out_specs=pl.BlockSpec((hq, bk, hd), cur),
        out_shape=jax.ShapeDtypeStruct((hq, m, hd), F32),
        compiler_params=_cparams(("parallel", "parallel")),
        name="swa" if local else "swa_ctx",
    )(*args)


def _mixer_cd_in(x2, g, sc, sh, w_in, d, seq, tm, rows_per_batch, name, rope):
    hd = d // 16
    qk, vw = RET_HEADS * hd, 2 * RET_HEADS * hd
    qw, kw = SWA_Q_HEADS * hd, SWA_KV_HEADS * hd
    if rope:
        t2 = _rope_tables_2d(seq, hd, 1)
        tables = [_rope_tables_1d(seq, hd, RET_HEADS), [jnp.tile(t, (1, SWA_Q_HEADS)) for t in t2],
                  [jnp.tile(t, (1, SWA_KV_HEADS)) for t in t2]]
        post = [(0, hd // 2, hd ** -0.5), (0, hd // 2, 1.0), (1, hd // 4, 1.0), (2, hd // 4, 1.0)]
    else:
        tables = []
        post = [(None, 0, hd ** -0.5), None, None, None]
    splits = [(qk, RET_HEADS, post[0]), (qk, RET_HEADS, post[1]), (vw, RET_HEADS, None), (vw, RET_HEADS, None),
              (qw, SWA_Q_HEADS, post[2]), (kw, SWA_KV_HEADS, post[3]), (kw, SWA_KV_HEADS, None)]
    names = ("rq", "rk", "rv", "rg", "sq", "sk", "sv")
    return dict(zip(names, _norm_mod_matmul(x2, g, sc, sh, w_in, splits, tm, rows_per_batch, name, tables)))


def _mixer_cd_core(lat, cx, bsz, seq, seq_c, decay_logit, sink, with_ctx):
    dk, dv = lat["rq"].shape[2], lat["rv"].shape[2]
    tabs = _ret_tables(decay_logit)
    zero = jnp.zeros((bsz, RET_HEADS, dk, dv), F32)
    ro_c, fin_f, fin_b = _retention(cx["rq"], cx["rk"], cx["rv"], cx["rg"], bsz, seq_c, tabs, zero, zero)
    ro_l, _, _ = _retention(lat["rq"], lat["rk"], lat["rv"], lat["rg"], bsz, seq, tabs, fin_f, fin_b)
    so_l = _swa(lat["sq"], lat["sk"], lat["sv"], cx["sk"], cx["sv"], sink, bsz, seq, seq_c, True)
    so_c = None
    if with_ctx:
        so_c = _swa(cx["sq"], None, None, cx["sk"], cx["sv"], sink, bsz, seq_c, seq_c, False)
    return ro_l, so_l, (ro_c if with_ctx else None), so_c


def _store_tile_rows(ref, val, base=0):
    r, d = val.shape
    sub = d // LANES
    for c in range(sub):
        ref[pl.ds(base + c, r, stride=sub), :] = val[:, c * LANES:(c + 1) * LANES]


def _load_tile_rows(ref, r, sub, base=0):
    return jnp.concatenate([ref[pl.ds(base + c, r, stride=sub), :] for c in range(sub)], axis=1)


def _router_body(x_ref, g_ref, rw_ref, rb_ref, tril_ref, sc_ref, sh_ref, hx_ref, ti_ref, gt_ref, pos_ref, cnt_ref):
    @pl.when(pl.program_id(0) == 0)
    def _():
        cnt_ref[...] = jnp.zeros_like(cnt_ref)

    hx = _norm_mod(x_ref[...], g_ref[...], sc_ref[...], sh_ref[...])
    _store_tile_rows(hx_ref, hx)
    logits = jnp.dot(hx, rw_ref[...], precision=HI, preferred_element_type=F32) + rb_ref[...]
    lane = lax.broadcasted_iota(jnp.int32, logits.shape, 1).astype(F32)
    rem = logits
    vals, hots = [], []
    for k in range(TOP_K):
        m = jnp.max(rem, axis=-1, keepdims=True)
        idx = jnp.min(jnp.where(rem == m, lane, float(N_EXPERTS)), axis=-1, keepdims=True)
        hot = lane == idx
        rem = jnp.where(hot, NEG_INF, rem)
        vals.append(m)
        hots.append(hot.astype(F32))
        ti_ref[:, k:k + 1] = idx.astype(jnp.int32)
    exps = [jnp.exp(v - vals[0]) for v in vals]
    den = exps[0] + exps[1] + exps[2] + exps[3]
    for k in range(TOP_K):
        gt_ref[:, k:k + 1] = exps[k] / den
    sel = hots[0] + hots[1] + hots[2] + hots[3]
    before = jnp.dot(tril_ref[...], sel.astype(BF16), preferred_element_type=F32) + cnt_ref[...]
    for k in range(TOP_K):
        pos_ref[:, k:k + 1] = jnp.sum(hots[k] * before, axis=-1, keepdims=True).astype(jnp.int32)
    cnt_ref[...] += jnp.sum(sel, axis=0, keepdims=True)


def _moe_route(x2, g, sc, sh, router_w, router_b, tr, rows_per_batch):
    n, d = x2.shape
    tiles_per_batch = rows_per_batch // tr
    tril = jnp.asarray(np.tril(np.ones((tr, tr), np.float32), -1)).astype(BF16)
    whole = lambda a: pl.BlockSpec(a.shape, lambda i, nd=a.ndim: (0,) * nd)
    bat = pl.BlockSpec((None, 1, d), lambda i: (i // tiles_per_batch, 0, 0))
    g2, rb2 = g.reshape(1, -1), router_b.reshape(1, -1)
    small = lambda dt: jax.ShapeDtypeStruct((n, TOP_K), dt)
    small_spec = pl.BlockSpec((tr, TOP_K), lambda i: (i, 0))
    return pl.pallas_call(
        _router_body,
        grid=(n // tr,),
        in_specs=[pl.BlockSpec((tr, d), lambda i: (i, 0)), whole(g2), whole(router_w), whole(rb2), whole(tril), bat, bat],
        out_specs=[pl.BlockSpec((tr * (d // LANES), LANES), lambda i: (i, 0)), small_spec, small_spec, small_spec,
                   pl.BlockSpec((1, N_EXPERTS), lambda i: (0, 0))],
        out_shape=[jax.ShapeDtypeStruct((n * (d // LANES), LANES), F32), small(jnp.int32), small(F32), small(jnp.int32),
                   jax.ShapeDtypeStruct((1, N_EXPERTS), F32)],
        compiler_params=_cparams(("arbitrary",)),
        name="moe_router",
    )(x2, g2, router_w, rb2, tril, sc, sh)


def _tile_row_copy(src, s_off, dst, d_off, sem, sub):
    return pltpu.make_async_copy(src.at[pl.ds(pl.multiple_of(s_off, sub), sub)],
                                 dst.at[pl.ds(pl.multiple_of(d_off, sub), sub)], sem)


def _dispatch_body(dest_ref, hx_ref, xs_in_ref, xs_ref, sem, *, td, sub):
    del xs_in_ref

    def issue(n, carry):
        for k in range(TOP_K):
            _tile_row_copy(hx_ref, n * sub, xs_ref, dest_ref[n * TOP_K + k], sem, sub).start()
        return carry

    def drain(n, carry):
        for k in range(TOP_K):
            _tile_row_copy(hx_ref, 0, xs_ref, 0, sem, sub).wait()
        return carry

    lax.fori_loop(0, td, issue, 0, unroll=DMA_UNROLL)
    lax.fori_loop(0, td, drain, 0, unroll=DMA_UNROLL)


def _moe_dispatch(hx, dest_off, n_slots, td, sub, xs_init=None):
    lanes = hx.shape[1]
    n = hx.shape[0] // sub
    zeros = jnp.zeros((n_slots * sub, lanes), F32) if xs_init is None else xs_init
    return pl.pallas_call(
        functools.partial(_dispatch_body, td=td, sub=sub),
        grid=(n // td,),
        in_specs=[pl.BlockSpec((td * TOP_K,), lambda i: (i,), memory_space=pltpu.SMEM),
                  pl.BlockSpec((td * sub, lanes), lambda i: (i, 0)), pl.BlockSpec(memory_space=pl.ANY)],
        out_specs=pl.BlockSpec(memory_space=pl.ANY),
        out_shape=jax.ShapeDtypeStruct((n_slots * sub, lanes), F32),
        scratch_shapes=[pltpu.SemaphoreType.DMA(())],
        input_output_aliases={2: 0},
        compiler_params=pltpu.CompilerParams(dimension_semantics=("arbitrary",), has_side_effects=True,
                                             vmem_limit_bytes=VMEM_LIMIT),
        name="moe_dispatch",
    )(dest_off, hx, zeros)


def _ffn_body(be_ref, nu_ref, x_ref, wgu_ref, bgu_ref, wdn_ref, bdn_ref, o_ref, wgu_bf, wdn_bf, *, tm):
    j = pl.program_id(0)
    e = be_ref[j]
    prev = be_ref[jnp.maximum(j - 1, 0)]

    @pl.when((j == 0) | (e != prev))
    def _():
        wgu_bf[...] = wgu_ref[...].astype(BF16)
        wdn_bf[...] = wdn_ref[...].astype(BF16)

    @pl.when(j < nu_ref[0])
    def _():
        f = wdn_ref.shape[0]
        x = _load_tile_rows(x_ref, tm, wgu_ref.shape[0] // LANES).astype(BF16)
        gu = jnp.dot(x, wgu_bf[...], preferred_element_type=F32) + bgu_ref[...]
        gate = jnp.minimum(gu[:, :f], SWIGLU_LIMIT)
        up = jnp.clip(gu[:, f:], -SWIGLU_LIMIT, SWIGLU_LIMIT)
        act = gate * jax.nn.sigmoid(SWIGLU_ALPHA * gate) * (up + 1.0)
        _store_tile_rows(o_ref, jnp.dot(act.astype(BF16), wdn_bf[...], preferred_element_type=F32) + bdn_ref[...])

    @pl.when(j >= nu_ref[0])
    def _():
        o_ref[...] = jnp.zeros_like(o_ref)


def _moe_ffn(xs, block_exp, n_used, layer, w_gu, b_gu, w_dn, b_dn, tm):
    depth, n_exp, d, f2 = w_gu.shape
    f = w_dn.shape[2]
    sub = d // LANES
    n_slots = xs.shape[0] // sub
    blk = lambda j, be, nu: (jnp.minimum(j, nu[0] - 1), 0)
    exp4 = lambda j, be, nu: (layer, be[j], 0, 0)
    grid_spec = pltpu.PrefetchScalarGridSpec(
        num_scalar_prefetch=2,
        grid=(n_slots // tm,),
        in_specs=[pl.BlockSpec((tm * sub, LANES), blk), pl.BlockSpec((None, None, d, f2), exp4),
                  pl.BlockSpec((None, None, 1, f2), exp4), pl.BlockSpec((None, None, f, d), exp4),
                  pl.BlockSpec((None, None, 1, d), exp4)],
        out_specs=pl.BlockSpec((tm * sub, LANES), lambda j, be, nu: (j, 0)),
        scratch_shapes=[pltpu.VMEM((d, f2), BF16), pltpu.VMEM((f, d), BF16)],
    )
    return pl.pallas_call(
        functools.partial(_ffn_body, tm=tm),
        grid_spec=grid_spec,
        out_shape=jax.ShapeDtypeStruct((n_slots * sub, LANES), F32),
        compiler_params=_cparams(("arbitrary",)),
        name="moe_ffn",
    )(block_exp, n_used, xs, w_gu, b_gu.reshape(depth, n_exp, 1, f2), w_dn, b_dn.reshape(depth, n_exp, 1, d))


def _combine_body(dest_ref, gt_ref, x_ref, ys_ref, g_ref, o_ref, buf, sem, *, tc, sub):
    def issue(n, carry):
        for k in range(TOP_K):
            _tile_row_copy(ys_ref, dest_ref[n * TOP_K + k], buf, (k * tc + n) * sub, sem, sub).start()
        return carry

    def drain(n, carry):
        for k in range(TOP_K):
            _tile_row_copy(ys_ref, 0, buf, 0, sem, sub).wait()
        return carry

    lax.fori_loop(0, tc, issue, 0, unroll=DMA_UNROLL)
    lax.fori_loop(0, tc, drain, 0, unroll=DMA_UNROLL)
    gates = [jnp.broadcast_to(gt_ref[:, k:k + 1], (tc, LANES)) for k in range(TOP_K)]
    for c in range(sub):
        cols = slice(c * LANES, (c + 1) * LANES)
        acc = gates[0] * buf[pl.ds(c, tc, stride=sub), :]
        for k in range(1, TOP_K):
            acc = acc + gates[k] * buf[pl.ds(k * tc * sub + c, tc, stride=sub), :]
        o_ref[:, cols] = x_ref[:, cols] + g_ref[:, cols] * acc


def _moe_combine(ys, dest_off, gates, x2, gate2, tc, rows_per_batch):
    n, d = x2.shape
    sub = d // LANES
    tiles_per_batch = rows_per_batch // tc
    return pl.pallas_call(
        functools.partial(_combine_body, tc=tc, sub=sub),
        grid=(n // tc,),
        in_specs=[pl.BlockSpec((tc * TOP_K,), lambda i: (i,), memory_space=pltpu.SMEM),
                  pl.BlockSpec((tc, TOP_K), lambda i: (i, 0)), pl.BlockSpec((tc, d), lambda i: (i, 0)),
                  pl.BlockSpec(memory_space=pl.ANY),
                  pl.BlockSpec((None, 1, d), lambda i: (i // tiles_per_batch, 0, 0))],
        out_specs=pl.BlockSpec((tc, d), lambda i: (i, 0)),
        out_shape=jax.ShapeDtypeStruct((n, d), F32),
        scratch_shapes=[pltpu.VMEM((TOP_K * tc * sub, LANES), F32), pltpu.SemaphoreType.DMA(())],
        compiler_params=_cparams(("arbitrary",)),
        name="moe_combine",
    )(dest_off, gates, x2, ys, gate2)


def _moe_layer(streams, g, router_w, router_b, layer, w_gu, b_gu, w_dn, b_dn):
    tm = MOE_TM
    sub = streams[0][0].shape[1] // LANES
    routed = [_moe_route(x2, g, sc, sh, router_w, router_b, tr, rpb) for x2, sc, sh, _, rpb, tr, _ in streams]
    counts = [r[4][0].astype(jnp.int32) for r in routed]
    total = functools.reduce(lambda a, b: a + b, counts)
    padded = (total + tm - 1) // tm * tm
    pad_end = jnp.cumsum(padded)
    n_tok = sum(s[0].shape[0] for s in streams)
    n_blocks = -(-n_tok * TOP_K // tm) + N_EXPERTS
    n_used = (pad_end[-1] // tm).astype(jnp.int32)
    blk_ids = jnp.arange(n_blocks, dtype=jnp.int32)
    last_row = jnp.minimum(blk_ids, n_used - 1) * tm
    block_exp = jnp.sum((pad_end[None, :] <= last_row[:, None]).astype(jnp.int32), axis=1)
    block_exp = jnp.minimum(block_exp, N_EXPERTS - 1)
    start = pad_end - padded
    xs, dests = None, []
    for (x2, _, _, _, _, tr, _), (hx, top_i, _, pos, _), cnt in zip(streams, routed, counts):
        dest_off = ((start[top_i] + pos) * sub).astype(jnp.int32).reshape(-1)
        xs = _moe_dispatch(hx, dest_off, n_blocks * tm, tr, sub, xs)
        dests.append(dest_off)
        start = start + cnt
    ys = _moe_ffn(xs, block_exp, n_used.reshape(1), layer, w_gu, b_gu, w_dn, b_dn, tm)
    return [_moe_combine(ys, dest_off, r[2], x2, gate2, tc, rpb)
            for (x2, _, _, gate2, rpb, _, tc), r, dest_off in zip(streams, routed, dests)]


def _mod_body(c_ref, w_ref, b_ref, o_ref):
    o_ref[...] = jnp.dot(jax.nn.silu(c_ref[...]), w_ref[...], precision=HI, preferred_element_type=F32) + b_ref[...]


def _modulation(cc, mod_w, mod_b):
    depth, d, d6 = mod_w.shape
    r = cc.shape[0]
    return pl.pallas_call(
        _mod_body,
        grid=(depth, d6 // d),
        in_specs=[pl.BlockSpec((r, d), lambda l, j: (0, 0)), pl.BlockSpec((None, d, d), lambda l, j: (l, 0, j)),
                  pl.BlockSpec((None, 1, d), lambda l, j: (l, 0, j))],
        out_specs=pl.BlockSpec((None, r, d), lambda l, j: (l, 0, j)),
        out_shape=jax.ShapeDtypeStruct((depth, r, d6), F32),
        compiler_params=_cparams(("parallel", "parallel")),
        name="modulation",
    )(cc, mod_w, mod_b.reshape(depth, 1, d6))


def kernel(x, c, ctx, c_ctx, mod_w, mod_b, norm1_g, norm2_g, ab_w_in, ab_w_out, hy_short_w, hy_short_b, hy_f_w1, hy_f_b1, hy_f_w2, hy_f_b2, hy_f_w3, hy_bias, s5_lambda_re, s5_lambda_im, s5_log_dt, s5_b_re, s5_b_im, s5_c_re, s5_c_im, s5_d, s5_glu_w, s5_glu_b, cd_w_in, cd_w_out, ret_decay_logit, swa_sink, router_w, router_b, exp_w_gu, exp_b_gu, exp_w_down, exp_b_down, final_g):
    bsz, seq, d = x.shape
    seq_c = ctx.shape[1]
    depth = mod_w.shape[0]
    m_l, m_c = bsz * seq, bsz * seq_c
    tm, tm_c, tt = 512, 256, 256
    xl = x.reshape(m_l, d)
    xc = ctx.reshape(m_c, d)
    cc = jnp.concatenate([c, c_ctx[None], jnp.zeros((SUBLANES - bsz - 1, d), F32)], axis=0)
    mods = _modulation(cc, mod_w, mod_b)
    hy_w = hy_f_w3.shape[2]
    for layer in range(depth):
        with_ctx = layer < depth - 1
        i = layer // 2
        sh1, sc1, g1, sh2, sc2, g2 = [t[:, None, :] for t in jnp.split(mods[layer, :bsz], 6, axis=-1)]
        csh1, csc1, cg1, csh2, csc2, cg2 = [t[:, None, :] for t in jnp.split(mods[layer, bsz:bsz + 1], 6, axis=-1)]
        if layer % 2 == 0:
            w_in, w_out = ab_w_in[i], ab_w_out[i]
            splits = [(3 * hy_w, 0, None), (w_in.shape[1] - 3 * hy_w, 0, None)]
            pa, pb = _norm_mod_matmul(xl, norm1_g[layer], sc1, sh1, w_in, splits, tm, seq, "ab_in")
            pac, pbc = _norm_mod_matmul(xc, norm1_g[layer], csc1, csh1, w_in, splits, tm_c, m_c, "ab_in_ctx")
            hy = (hy_short_w[i], hy_short_b[i], hy_f_w1[i], hy_f_b1[i], hy_f_w2[i], hy_f_b2[i], hy_f_w3[i], hy_bias[i])
            ya = _hyena(pa, bsz, seq, *hy, tt)
            yb, ybc = _s5_mixer(pb, pbc, bsz, seq, seq_c, s5_lambda_re[i], s5_lambda_im[i], s5_log_dt[i], s5_b_re[i],
                                s5_b_im[i], s5_c_re[i], s5_c_im[i], s5_d[i], s5_glu_w[i], s5_glu_b[i], tm, tm_c)
            ws = [w_out[:hy_w], w_out[hy_w:]]
            xl = _out_proj([ya, yb], ws, xl, g1, tm, seq, "ab_out")
            if with_ctx:
                yac = _hyena(pac, bsz, seq_c, *hy, tt)
                xc = _out_proj([yac, ybc], ws, xc, cg1, tm_c, m_c, "ab_out_ctx")
        else:
            w_in, w_out = cd_w_in[i], cd_w_out[i]
            vw = 2 * RET_HEADS * (d // 16)
            qw = SWA_Q_HEADS * (d // 16)
            lat = _mixer_cd_in(xl, norm1_g[layer], sc1, sh1, w_in, d, seq, tm, seq, "cd_in", True)
            cx = _mixer_cd_in(xc, norm1_g[layer], csc1, csh1, w_in, d, seq_c, tm_c, m_c, "cd_in_ctx", False)
            ro_l, so_l, ro_c, so_c = _mixer_cd_core(lat, cx, bsz, seq, seq_c, ret_decay_logit[i], swa_sink[i],
                                                    with_ctx)
            ws = [w_out[:vw].reshape(RET_HEADS, vw // RET_HEADS, d), w_out[vw:].reshape(SWA_Q_HEADS, qw // SWA_Q_HEADS, d)]
            xl = _out_proj([ro_l, so_l], ws, xl, g1, tm, seq, "cd_out")
            if with_ctx:
                xc = _out_proj([ro_c, so_c], ws, xc, cg1, tm_c, m_c, "cd_out_ctx")
        moe_w = (router_w[layer], router_b[layer], layer, exp_w_gu, exp_b_gu, exp_w_down, exp_b_down)
        streams = [(xl, sc2, sh2, g2, seq, tm, tm_c)]
        if with_ctx:
            streams.append((xc, csc2, csh2, cg2, m_c, tm_c, tm_c))
        outs = _moe_layer(streams, norm2_g[layer], *moe_w)
        xl = outs[0]
        if with_ctx:
            xc = outs[1]
    out = _rows_call(_final_norm_body, [xl], [final_g.reshape(1, -1)], [], [((m_l, d), F32)], tm, m_l, "final_norm")[0]
    return out.reshape(bsz, seq, d)
```

```python
import functools
import math

import numpy as np
import jax
import jax.numpy as jnp
from jax import lax
from jax.experimental import pallas as pl
from jax.experimental.pallas import tpu as pltpu

F32 = jnp.float32
BF16 = jnp.bfloat16
HI = lax.Precision.HIGHEST

EPS = 1e-6
NEG_INF = -1e30
ROPE_BASE = 10000.0
GRID_W = 64

HY_SHORT = 3
HY_BANDS = 16
HY_SHIFT = 0.05
HY_FAST_DECAY = math.log(1e-2) / 0.3
HY_SLOW_DECAY = math.log(1e-2) / 1.5
S5_GROUP = 16
S5_STATE = 64
S5_CHUNK = 8
RET_HEADS = 4
RET_CHUNK = 128
SWA_Q_HEADS = 8
SWA_KV_HEADS = 2
SWA_WINDOW = 128
SWA_BLOCK = 128
N_EXPERTS = 32
TOP_K = 4
SWIGLU_LIMIT = 7.0
SWIGLU_ALPHA = 1.702

LANES = 128
SUBLANES = 8
VMEM_LIMIT = 52 * 2**20
FFT_B = 128
MOE_TM = 512
DMA_UNROLL = 4


def _cparams(sem):
    return pltpu.CompilerParams(dimension_semantics=sem, vmem_limit_bytes=VMEM_LIMIT)


def _rows_call(body, rows, consts, batched, outs, tm, rows_per_batch, name, periodic=()):
    m = rows[0].shape[-2]
    assert m % tm == 0 and rows_per_batch % tm == 0
    tiles_per_batch = rows_per_batch // tm

    def row_spec(shape):
        if len(shape) == 2:
            return pl.BlockSpec((tm, shape[1]), lambda i: (i, 0))
        return pl.BlockSpec((shape[0], tm, shape[2]), lambda i: (0, i, 0))

    in_specs = [row_spec(a.shape) for a in rows]
    for a in periodic:
        in_specs.append(pl.BlockSpec((tm, a.shape[1]), lambda i: (i % tiles_per_batch, 0)))
    for a in consts:
        in_specs.append(pl.BlockSpec(a.shape, lambda i, n=a.ndim: (0,) * n))
    for a in batched:
        in_specs.append(pl.BlockSpec((None, 1, a.shape[2]), lambda i: (i // tiles_per_batch, 0, 0)))
    out_specs = [row_spec(s) for s, _ in outs]
    out_shape = [jax.ShapeDtypeStruct(s, d) for s, d in outs]
    res = pl.pallas_call(
        body,
        grid=(m // tm,),
        in_specs=in_specs,
        out_specs=out_specs,
        out_shape=out_shape,
        compiler_params=_cparams(("parallel",)),
        name=name,
    )(*rows, *periodic, *consts, *batched)
    return res


def _norm_mod(x, g, sc, sh):
    y = x * lax.rsqrt(jnp.mean(x * x, axis=-1, keepdims=True) + EPS) * g
    return y * (1.0 + sc) + sh


def _rotate_half(x, cos, sin, shift):
    w = x.shape[1]
    lane = lax.broadcasted_iota(jnp.int32, x.shape, 1)
    partner = jnp.where(lane % (2 * shift) < shift, pltpu.roll(x, w - shift, axis=1), pltpu.roll(x, shift, axis=1))
    return x * cos + partner * sin


def _norm_mod_matmul_body(x_ref, *refs, splits, n_tab):
    tabs = refs[:2 * n_tab]
    g_ref, w_ref, sc_ref, sh_ref = refs[2 * n_tab:2 * n_tab + 4]
    o_refs = refs[2 * n_tab + 4:]
    h = _norm_mod(x_ref[...], g_ref[...], sc_ref[...], sh_ref[...])
    r = jnp.dot(h.astype(BF16), w_ref[...], preferred_element_type=F32)
    off = 0
    for o_ref, (n, heads, post) in zip(o_refs, splits):
        x = r[:, off:off + n]
        if post is not None:
            table, shift, scale = post
            if table is not None:
                x = _rotate_half(x, tabs[2 * table][...], tabs[2 * table + 1][...], shift)
            if scale != 1.0:
                x = x * scale
        if heads:
            hd = n // heads
            for h_i in range(heads):
                o_ref[h_i] = x[:, h_i * hd:(h_i + 1) * hd]
        else:
            o_ref[...] = x
        off += n


def _norm_mod_matmul(x2, g, sc, sh, w, splits, tm, rows_per_batch, name, tables=()):
    m = x2.shape[0]
    outs = [((heads, m, n // heads), F32) if heads else ((m, n), F32) for n, heads, _ in splits]
    body = functools.partial(_norm_mod_matmul_body, splits=splits, n_tab=len(tables))
    periodic = [t for pair in tables for t in pair]
    return _rows_call(body, [x2], [g.reshape(1, -1), w.astype(BF16)], [sc, sh], outs, tm, rows_per_batch, name,
                      periodic=periodic)


def _out_proj_body(*refs, n_in):
    a_refs = refs[:n_in]
    x_ref = refs[n_in]
    w_refs = refs[n_in + 1: 2 * n_in + 1]
    g_ref = refs[2 * n_in + 1]
    o_ref = refs[2 * n_in + 2]
    acc = None
    for a_ref, w_ref in zip(a_refs, w_refs):
        if a_ref.ndim == 3:
            for h_i in range(a_ref.shape[0]):
                t = jnp.dot(a_ref[h_i].astype(BF16), w_ref[h_i], preferred_element_type=F32)
                acc = t if acc is None else acc + t
        else:
            t = jnp.dot(a_ref[...].astype(BF16), w_ref[...], preferred_element_type=F32)
            acc = t if acc is None else acc + t
    o_ref[...] = x_ref[...] + g_ref[...] * acc


def _out_proj(parts, ws, x2, gate, tm, rows_per_batch, name):
    m, d = x2.shape
    body = functools.partial(_out_proj_body, n_in=len(parts))
    ws = [w.astype(BF16) for w in ws]
    return _rows_call(body, list(parts) + [x2], ws, [gate], [((m, d), F32)], tm, rows_per_batch, name)[0]


def _final_norm_body(x_ref, g_ref, o_ref):
    x = x_ref[...]
    o_ref[...] = x * lax.rsqrt(jnp.mean(x * x, axis=-1, keepdims=True) + EPS) * g_ref[...]


def _hy_prep_body(u_ref, p_ref, n_ref, w_ref, b_ref, x0_ref, z_ref, *, width):
    i = pl.program_id(1)
    last = pl.num_programs(1) - 1
    u = u_ref[...]
    tt = u.shape[0]
    prev_row = jnp.where(i == 0, 0.0, p_ref[SUBLANES - 1:SUBLANES, :])
    next_row = jnp.where(i == last, 0.0, n_ref[0:1, :])
    rows = lax.broadcasted_iota(jnp.int32, (tt, 1), 0)
    up = jnp.where(rows == 0, prev_row, pltpu.roll(u, 1, axis=0))
    dn = jnp.where(rows == tt - 1, next_row, pltpu.roll(u, tt - 1, axis=0))
    y = w_ref[0:1, :] * up + w_ref[1:2, :] * u + w_ref[2:3, :] * dn + b_ref[...]
    x0_ref[...] = y[:, :width]
    z_ref[...] = y[:, 2 * width:] * y[:, width:2 * width]


def _hyena_prep(p, short_w, short_b, bsz, seq, tt):
    w3 = p.shape[1]
    width = w3 // 3
    p3 = p.reshape(bsz, seq, w3)
    nt = seq // tt
    sub = tt // SUBLANES
    nsub = seq // SUBLANES
    body = functools.partial(_hy_prep_body, width=width)
    x0, z = pl.pallas_call(
        body,
        grid=(bsz, nt),
        in_specs=[
            pl.BlockSpec((None, tt, w3), lambda b, i: (b, i, 0)),
            pl.BlockSpec((None, SUBLANES, w3), lambda b, i: (b, jnp.maximum(i * sub - 1, 0), 0)),
            pl.BlockSpec((None, SUBLANES, w3), lambda b, i: (b, jnp.minimum((i + 1) * sub, nsub - 1), 0)),
            pl.BlockSpec((HY_SHORT, w3), lambda b, i: (0, 0)),
            pl.BlockSpec((1, w3), lambda b, i: (0, 0)),
        ],
        out_specs=[pl.BlockSpec((None, tt, width), lambda b, i: (b, i, 0))] * 2,
        out_shape=[jax.ShapeDtypeStruct((bsz, seq, width), F32)] * 2,
        compiler_params=_cparams(("parallel", "parallel")),
        name="hyena_prep",
    )(p3, p3, p3, short_w, short_b.reshape(1, -1))
    return x0, z


def _hy_filter_body(t_ref, w_ref, lag_ref, bands_ref, deltas_ref, w1t_ref, w1c_ref, w1s_ref, b1_ref,
                    w2_ref, b2_ref, w3_ref, h_ref, s_ref):
    i = pl.program_id(0)
    arg = w_ref[...] * bands_ref[...]
    pre = (t_ref[...] * w1t_ref[...]
           + jnp.dot(jnp.cos(arg), w1c_ref[...], precision=HI, preferred_element_type=F32)
           + jnp.dot(-jnp.sin(arg), w1s_ref[...], precision=HI, preferred_element_type=F32)
           + b1_ref[...])
    h1 = jnp.sin(pre)
    h2 = jnp.sin(jnp.dot(h1, w2_ref[...], precision=HI, preferred_element_type=F32) + b2_ref[...])
    h3 = jnp.dot(h2, w3_ref[...], precision=HI, preferred_element_type=F32)
    h = h3 * (jnp.exp(-lag_ref[...] * deltas_ref[...]) + HY_SHIFT)
    h_ref[...] = h

    @pl.when(i == 0)
    def _():
        s_ref[...] = jnp.zeros_like(s_ref)

    s_ref[...] += jnp.sum(jnp.abs(h), axis=0, keepdims=True)


def _hyena_filter(seq, w1, b1, w2, b2, w3):
    width = w3.shape[1]
    pos = jnp.arange(seq, dtype=F32)
    t = (pos / seq)[:, None]
    w = (2.0 * math.pi * pos / seq)[:, None]
    lag = (jnp.abs(pos - seq // 2) / (seq / 2))[:, None]
    bands = jnp.linspace(1e-4, HY_BANDS - 1, HY_BANDS, dtype=F32)[None]
    deltas = jnp.abs(jnp.linspace(HY_FAST_DECAY, HY_SLOW_DECAY, width, dtype=F32))[None]
    tl = min(seq, 1024)
    col = pl.BlockSpec((tl, 1), lambda i: (i, 0))

    def whole(a):
        return pl.BlockSpec(a.shape, lambda i, n=a.ndim: (0,) * n)

    consts = [bands, deltas, w1[0:1], w1[1:1 + HY_BANDS], w1[1 + HY_BANDS:], b1.reshape(1, -1),
              w2, b2.reshape(1, -1), w3]
    h, s = pl.pallas_call(
        _hy_filter_body,
        grid=(seq // tl,),
        in_specs=[col, col, col] + [whole(a) for a in consts],
        out_specs=[pl.BlockSpec((tl, width), lambda i: (i, 0)), pl.BlockSpec((1, width), lambda i: (0, 0))],
        out_shape=[jax.ShapeDtypeStruct((seq, width), F32), jax.ShapeDtypeStruct((1, width), F32)],
        compiler_params=_cparams(("arbitrary",)),
        name="hyena_filter",
    )(t, w, lag, *consts)
    return h, s


FFT_G = SUBLANES
FFT_CW = 512


def _kron_eye(mat):
    return np.kron(mat, np.eye(FFT_G))


def _dft_tables_k(na, ka, a_lo, a_cnt):
    n = na * FFT_B
    a = np.arange(na)
    b = np.arange(FFT_B)
    ang1 = 2.0 * np.pi * np.outer(a, a) / na
    ang2 = 2.0 * np.pi * np.outer(b, b) / FFT_B
    angt = 2.0 * np.pi * np.outer(b, a) / n
    c1, s1 = np.cos(ang1), np.sin(ang1)
    c2, s2 = np.cos(ang2), np.sin(ang2)
    nbb = FFT_B // FFT_G
    tw1 = angt.reshape(nbb, FFT_G, na).transpose(0, 2, 1).reshape(nbb, na * FFT_G, 1)
    rows = slice(a_lo, a_lo + a_cnt)
    tabs = dict(
        m1=_kron_eye(np.concatenate([c1, -s1], axis=0)[:, :ka]),
        f2=np.block([[c2, s2], [-s2, c2]]),
        f2i=np.block([[c2, -s2], [s2, c2]]),
        m3r=_kron_eye(c1[rows] / n), m3i=_kron_eye(-s1[rows] / n),
        tw1c=np.cos(tw1), tw1s=np.sin(tw1),
        twc_c=np.cos(angt).T[:, :, None], tws_c=np.sin(angt).T[:, :, None],
    )
    return {k: jnp.asarray(v, F32) for k, v in tabs.items()}


def _fftk1_body(x_ref, m_ref, tc_ref, ts_ref, sc_ref, o_ref, *, na):
    ka, g, cw = x_ref.shape
    v = (x_ref[...].reshape(ka * g, cw) * sc_ref[...]).astype(BF16)
    r = jnp.dot(m_ref[...], v, preferred_element_type=F32)
    gr, gi = r[:na * g], r[na * g:]
    tc, ts = tc_ref[...], ts_ref[...]
    o_ref[...] = jnp.concatenate([gr * tc + gi * ts, gi * tc - gr * ts], axis=0).reshape(2 * na, g, cw)


def _fftk_stage1(x5, scale, tabs, na):
    bz, ka, nbb, g, ch = x5.shape
    cw = min(ch, FFT_CW)
    m1 = tabs["m1"].astype(BF16)
    return pl.pallas_call(
        functools.partial(_fftk1_body, na=na),
        grid=(bz, ch // cw, nbb),
        in_specs=[
            pl.BlockSpec((None, ka, None, g, cw), lambda z, q, b: (z, 0, b, 0, q)),
            pl.BlockSpec(m1.shape, lambda z, q, b: (0, 0)),
            pl.BlockSpec((None, na * g, 1), lambda z, q, b: (b, 0, 0)),
            pl.BlockSpec((None, na * g, 1), lambda z, q, b: (b, 0, 0)),
            pl.BlockSpec((1, cw), lambda z, q, b: (0, q)),
        ],
        out_specs=pl.BlockSpec((None, 2 * na, None, g, cw), lambda z, q, b: (z, 0, b, 0, q)),
        out_shape=jax.ShapeDtypeStruct((bz, 2 * na, nbb, g, ch), F32),
        compiler_params=_cparams(("parallel", "parallel", "parallel")),
        name="fft_stage1",
    )(x5, m1, tabs["tw1c"], tabs["tw1s"], scale)


def _fftk2_spec_body(ar_ref, ai_ref, f_ref, o_ref):
    f = f_ref[...].astype(BF16)
    cw = ar_ref.shape[-1]
    for j in range(FFT_G):
        v = jnp.concatenate([ar_ref[j].reshape(FFT_B, cw), ai_ref[j].reshape(FFT_B, cw)], axis=0).astype(BF16)
        o_ref[j] = jnp.dot(f, v, preferred_element_type=F32)


def _fftk2_conv_body(ar_ref, ai_ref, h_ref, f_ref, fi_ref, tc_ref, ts_ref, o_ref):
    f = f_ref[...].astype(BF16)
    fi = fi_ref[...].astype(BF16)
    cw = ar_ref.shape[-1]
    nbb = FFT_B // FFT_G
    for j in range(FFT_G):
        v = jnp.concatenate([ar_ref[j].reshape(FFT_B, cw), ai_ref[j].reshape(FFT_B, cw)], axis=0).astype(BF16)
        x = jnp.dot(f, v, preferred_element_type=F32)
        xr, xi = x[:FFT_B], x[FFT_B:]
        hr, hi = h_ref[j, :FFT_B, :], h_ref[j, FFT_B:, :]
        p = jnp.concatenate([xr * hr - xi * hi, xr * hi + xi * hr], axis=0).astype(BF16)
        q = jnp.dot(fi, p, preferred_element_type=F32)
        qr, qi = q[:FFT_B], q[FFT_B:]
        tc, ts = tc_ref[j], ts_ref[j]
        o_ref[j, 0] = (qr * tc - qi * ts).reshape(nbb, FFT_G, cw)
        o_ref[j, 1] = (qi * tc + qr * ts).reshape(nbb, FFT_G, cw)


def _fftk_stage2_spectrum(a5, tabs, na):
    _, _, nbb, g, ch = a5.shape
    cw = min(ch, FFT_CW)
    ng = na // FFT_G
    return pl.pallas_call(
        _fftk2_spec_body,
        grid=(ch // cw, ng),
        in_specs=[
            pl.BlockSpec((None, FFT_G, nbb, g, cw), lambda q, c: (0, c, 0, 0, q)),
            pl.BlockSpec((None, FFT_G, nbb, g, cw), lambda q, c: (0, ng + c, 0, 0, q)),
            pl.BlockSpec((2 * FFT_B, 2 * FFT_B), lambda q, c: (0, 0)),
        ],
        out_specs=pl.BlockSpec((FFT_G, 2 * FFT_B, cw), lambda q, c: (c, 0, q)),
        out_shape=jax.ShapeDtypeStruct((na, 2 * FFT_B, ch), F32),
        compiler_params=_cparams(("parallel", "parallel")),
        name="fft_stage2_spectrum",
    )(a5, a5, tabs["f2"])


def _fftk_stage2_conv(a5, hspec, tabs, na):
    bz, _, nbb, g, ch = a5.shape
    cw = min(ch, FFT_CW)
    ng = na // FFT_G
    return pl.pallas_call(
        _fftk2_conv_body,
        grid=(ch // cw, ng, bz),
        in_specs=[
            pl.BlockSpec((None, FFT_G, nbb, g, cw), lambda q, c, z: (z, c, 0, 0, q)),
            pl.BlockSpec((None, FFT_G, nbb, g, cw), lambda q, c, z: (z, ng + c, 0, 0, q)),
            pl.BlockSpec((FFT_G, 2 * FFT_B, cw), lambda q, c, z: (c, 0, q)),
            pl.BlockSpec((2 * FFT_B, 2 * FFT_B), lambda q, c, z: (0, 0)),
            pl.BlockSpec((2 * FFT_B, 2 * FFT_B), lambda q, c, z: (0, 0)),
            pl.BlockSpec((FFT_G, FFT_B, 1), lambda q, c, z: (c, 0, 0)),
            pl.BlockSpec((FFT_G, FFT_B, 1), lambda q, c, z: (c, 0, 0)),
        ],
        out_specs=pl.BlockSpec((None, None, FFT_G, 2, nbb, g, cw), lambda q, c, z: (z, c, 0, 0, 0, 0, q)),
        out_shape=jax.ShapeDtypeStruct((bz, ng, FFT_G, 2, nbb, g, ch), F32),
        compiler_params=_cparams(("parallel", "parallel", "parallel")),
        name="fft_stage2_conv",
    )(a5, a5, hspec, tabs["f2"], tabs["f2i"], tabs["twc_c"], tabs["tws_c"])


def _fftk3_body(*refs, gate):
    if gate:
        br_ref, bi_ref, mr_ref, mi_ref, x0_ref, z_ref, bias_ref, o_ref = refs
    else:
        br_ref, bi_ref, mr_ref, mi_ref, o_ref = refs
    ng, gc, g, cw = br_ref.shape
    vr = br_ref[...].reshape(ng * gc * g, cw).astype(BF16)
    vi = bi_ref[...].reshape(ng * gc * g, cw).astype(BF16)
    y = jnp.dot(mr_ref[...], vr, preferred_element_type=F32) + jnp.dot(mi_ref[...], vi, preferred_element_type=F32)
    a_cnt = o_ref.shape[0]
    if gate:
        x0 = x0_ref[...].reshape(a_cnt * g, cw)
        z = z_ref[...].reshape(a_cnt * g, cw)
        y = x0 * (y + bias_ref[...] * z)
    o_ref[...] = y.reshape(a_cnt, g, cw)


def _fftk_stage3(b7, tabs, na, a_cnt, gate_args=None):
    bz, ng, gc, _, nbb, g, ch = b7.shape
    cw = min(ch, FFT_CW)
    mr, mi = tabs["m3r"].astype(BF16), tabs["m3i"].astype(BF16)
    row_spec = pl.BlockSpec((None, a_cnt, None, g, cw), lambda z, q, b: (z, 0, b, 0, q))
    in_specs = [
        pl.BlockSpec((None, ng, gc, None, None, g, cw), lambda z, q, b: (z, 0, 0, 0, b, 0, q)),
        pl.BlockSpec((None, ng, gc, None, None, g, cw), lambda z, q, b: (z, 0, 0, 1, b, 0, q)),
        pl.BlockSpec(mr.shape, lambda z, q, b: (0, 0)),
        pl.BlockSpec(mi.shape, lambda z, q, b: (0, 0)),
    ]
    args = [b7, b7, mr, mi]
    if gate_args is not None:
        in_specs += [row_spec, row_spec, pl.BlockSpec((1, cw), lambda z, q, b: (0, q))]
        args += list(gate_args)
    return pl.pallas_call(
        functools.partial(_fftk3_body, gate=gate_args is not None),
        grid=(bz, ch // cw, nbb),
        in_specs=in_specs,
        out_specs=row_spec,
        out_shape=jax.ShapeDtypeStruct((bz, a_cnt, nbb, g, ch), F32),
        compiler_params=_cparams(("parallel", "parallel", "parallel")),
        name="fft_stage3",
    )(*args)


def _hy_gate_body(x0_ref, y_ref, z_ref, b_ref, o_ref):
    o_ref[...] = x0_ref[...] * (y_ref[...] + b_ref[...] * z_ref[...])


def _hyena(p, bsz, seq, short_w, short_b, w1, b1, w2, b2, w3, hy_bias, tt):
    width = w3.shape[1]
    x0, z = _hyena_prep(p, short_w, short_b, bsz, seq, tt)
    hu, hs = _hyena_filter(seq, w1, b1, w2, b2, w3)
    na = max(2 * seq // FFT_B, 16)
    ka = max(seq // FFT_B, 16)
    pad = ka * FFT_B - seq
    a_lo, a_cnt = (0, na) if pad else (seq // 2 // FFT_B, ka)
    tabs = _dft_tables_k(na, ka, a_lo, a_cnt)
    nbb = FFT_B // FFT_G

    def rows5(a, lead):
        if pad:
            a = jnp.pad(a, ((0, 0), (0, pad), (0, 0)))
        return a.reshape(lead, ka, nbb, FFT_G, width)

    ones = jnp.ones((1, width), F32)
    bias = hy_bias.reshape(1, -1)
    hspec = _fftk_stage2_spectrum(_fftk_stage1(rows5(hu[None], 1), 1.0 / hs, tabs, na), tabs, na)
    a5 = _fftk_stage1(rows5(z, bsz), ones, tabs, na)
    b7 = _fftk_stage2_conv(a5, hspec, tabs, na)
    m = bsz * seq
    if not pad:
        out = _fftk_stage3(b7, tabs, na, a_cnt, (rows5(x0, bsz), rows5(z, bsz), bias))
        return out.reshape(m, width)
    y = _fftk_stage3(b7, tabs, na, a_cnt).reshape(bsz, na * FFT_B, width)[:, seq // 2: seq // 2 + seq]
    return _rows_call(_hy_gate_body, [x0.reshape(m, width), y.reshape(m, width), z.reshape(m, width)],
                      [bias], [], [((m, width), F32)], tt, seq, "hyena_gate")[0]


def _s5_tables(lam_re, lam_im, log_dt, b_re, b_im, c_re, c_im, nsteps):
    t_len, hdim = S5_CHUNK, S5_GROUP
    lam = lax.complex(jnp.minimum(lam_re.astype(F32), -1e-4), lam_im.astype(F32))
    dt = jnp.exp(log_dt.astype(F32))[..., None]
    lam_dt = lam * dt
    lam_bar = jnp.exp(lam_dt)
    b_bar = ((lam_bar - 1.0) / lam)[..., None] * lax.complex(b_re.astype(F32), b_im.astype(F32))
    cm = lax.complex(c_re.astype(F32), c_im.astype(F32))
    ks = jnp.arange(t_len + 1, dtype=F32)
    pw = jnp.exp(ks[:, None, None, None] * lam_dt[None])
    g = lam.shape[1]
    tabs = {}
    pin = [pw[:t_len, 0][::-1], pw[:t_len, 1]]
    pout = [pw[1:, 0], pw[1:, 1][::-1]]
    for d in range(2):
        win = pin[d][:, :, :, None] * b_bar[d][None]
        win = jnp.transpose(win, (1, 0, 3, 2)).reshape(g, t_len * hdim, -1)
        tabs[f"win{d}"] = jnp.concatenate([jnp.real(win), jnp.imag(win)], axis=-1)
        wout = pout[d][:, :, None, :] * cm[d][None]
        wout = jnp.transpose(wout, (1, 3, 0, 2)).reshape(g, -1, t_len * hdim)
        tabs[f"wout{d}"] = jnp.concatenate([jnp.real(wout), -jnp.imag(wout)], axis=1)
        mu = jnp.exp((t_len * 2.0 ** jnp.arange(nsteps, dtype=F32))[:, None, None] * lam_dt[d][None])
        mr, mi = jnp.real(mu), jnp.imag(mu)
        tabs[f"m1{d}"] = jnp.concatenate([mr, mr], axis=-1)[:, :, None, :]
        tabs[f"m2{d}"] = jnp.concatenate([-mi, mi], axis=-1)[:, :, None, :]
        mu1 = jnp.exp(t_len * lam_dt[d])
        tabs[f"mu1{d}"] = jnp.concatenate([jnp.real(mu1), jnp.real(mu1)], axis=-1)[:, None, :]
        tabs[f"mu2{d}"] = jnp.concatenate([-jnp.imag(mu1), jnp.imag(mu1)], axis=-1)[:, None, :]
    kern = [jnp.real(jnp.einsum("ghp,tgp,gpk->tghk", cm[d], pw[:t_len, d], b_bar[d], precision=HI)) for d in range(2)]
    s_idx = jnp.arange(t_len)[:, None]
    t_idx = jnp.arange(t_len)[None, :]
    fwd = jnp.where((t_idx >= s_idx)[:, :, None, None, None], kern[0][jnp.maximum(t_idx - s_idx, 0)], 0.0)
    bwd = jnp.where((s_idx >= t_idx)[:, :, None, None, None], kern[1][jnp.maximum(s_idx - t_idx, 0)], 0.0)
    d0 = fwd + bwd
    tabs["d0"] = jnp.transpose(d0, (2, 0, 4, 1, 3)).reshape(g, t_len * hdim, t_len * hdim)
    return tabs


S5_GPB = LANES // S5_GROUP


def _s5_perm_table():
    p = np.zeros((S5_CHUNK, S5_GPB, LANES, LANES), np.float32)
    for t in range(S5_CHUNK):
        for g in range(S5_GPB):
            for h in range(S5_GROUP):
                p[t, g, S5_GROUP * g + h, S5_GROUP * t + h] = 1.0
    return jnp.asarray(p)


def _s5_body(u_ref, perm_ref, permt_ref, d0_ref, win0_ref, win1_ref, wout0_ref, wout1_ref, m10_ref, m20_ref, m11_ref,
             m21_ref, mu10_ref, mu20_ref, mu11_ref, mu21_ref, s0_ref, s1_ref, y_ref, f0_ref, f1_ref, *, nc, nsteps):
    half = S5_STATE
    t_len = S5_CHUNK
    j = lax.broadcasted_iota(jnp.int32, (nc, 1), 0)

    def swap(a):
        return pltpu.roll(a, half, axis=1)

    xs = [u_ref[pl.ds(t, nc, stride=t_len), :].astype(BF16) for t in range(t_len)]
    ys = []
    for g in range(S5_GPB):
        u = None
        for t in range(t_len):
            part = jnp.dot(xs[t], perm_ref[t, g].astype(BF16), preferred_element_type=F32)
            u = part if u is None else u + part
        u = u.astype(BF16)
        e0 = jnp.dot(u, win0_ref[g].astype(BF16), preferred_element_type=F32)
        e1 = jnp.dot(u, win1_ref[g].astype(BF16), preferred_element_type=F32)
        s_f = jnp.where(j == 0, s0_ref[g:g + 1, :], pltpu.roll(e0, 1, axis=0))
        s_b = jnp.where(j == nc - 1, s1_ref[g:g + 1, :], pltpu.roll(e1, nc - 1, axis=0))
        for k in range(nsteps):
            step = 2 ** k
            sh = jnp.where(j >= step, pltpu.roll(s_f, step, axis=0), 0.0)
            s_f = s_f + m10_ref[k, g] * sh + m20_ref[k, g] * swap(sh)
            sh = jnp.where(j < nc - step, pltpu.roll(s_b, nc - step, axis=0), 0.0)
            s_b = s_b + m11_ref[k, g] * sh + m21_ref[k, g] * swap(sh)
        y = jnp.dot(u, d0_ref[g].astype(BF16), preferred_element_type=F32)
        y = y + jnp.dot(s_f.astype(BF16), wout0_ref[g].astype(BF16), preferred_element_type=F32)
        y = y + jnp.dot(s_b.astype(BF16), wout1_ref[g].astype(BF16), preferred_element_type=F32)
        y_hi = y.astype(BF16)
        ys.append((y_hi, (y - y_hi.astype(F32)).astype(BF16)))
        t_f = mu10_ref[g] * s_f + mu20_ref[g] * swap(s_f) + e0
        t_b = mu11_ref[g] * s_b + mu21_ref[g] * swap(s_b) + e1
        f0_ref[g:g + 1, :] = t_f[nc - 1:nc, :]
        f1_ref[g:g + 1, :] = t_b[0:1, :]
    for t in range(t_len):
        out = None
        for g in range(S5_GPB):
            pt = permt_ref[t, g].astype(BF16)
            part = jnp.dot(ys[g][0], pt, preferred_element_type=F32) + jnp.dot(ys[g][1], pt, preferred_element_type=F32)
            out = part if out is None else out + part
        y_ref[pl.ds(t, nc, stride=t_len), :] = out


def _s5_core(u2, bsz, seq, tabs, init0, init1):
    m, width = u2.shape
    g = width // S5_GROUP
    nq = width // LANES
    t_len = S5_CHUNK
    nc = seq // t_len
    cols = t_len * S5_GROUP
    nsteps = max(1, math.ceil(math.log2(nc)))
    assert nsteps <= tabs["m10"].shape[0] and cols == LANES
    perm = _s5_perm_table()
    permt = jnp.swapaxes(perm, 2, 3)

    def whole(a):
        return pl.BlockSpec(a.shape, lambda q, b, nd=a.ndim: (0,) * nd)

    def per_q(shape):
        return pl.BlockSpec((S5_GPB,) + shape, lambda q, b: (q,) + (0,) * len(shape))

    p2 = 2 * S5_STATE
    steps_spec = pl.BlockSpec((nsteps, S5_GPB, 1, p2), lambda q, b: (0, q, 0, 0))
    state_spec = pl.BlockSpec((None, S5_GPB, p2), lambda q, b: (b, q, 0))
    seq_spec = pl.BlockSpec((seq, LANES), lambda q, b: (b, q))
    in_specs = [seq_spec, whole(perm), whole(permt), per_q((cols, cols)), per_q((cols, p2)), per_q((cols, p2)),
                per_q((p2, cols)), per_q((p2, cols))] + [steps_spec] * 4 + [per_q((1, p2))] * 4 + [state_spec] * 2
    state_shape = jax.ShapeDtypeStruct((bsz, g, p2), F32)
    y, f0, f1 = pl.pallas_call(
        functools.partial(_s5_body, nc=nc, nsteps=nsteps),
        grid=(nq, bsz),
        in_specs=in_specs,
        out_specs=[seq_spec, state_spec, state_spec],
        out_shape=[jax.ShapeDtypeStruct((m, width), F32), state_shape, state_shape],
        compiler_params=_cparams(("parallel", "parallel")),
        name="s5_scan",
    )(u2, perm, permt, tabs["d0"], tabs["win0"], tabs["win1"], tabs["wout0"], tabs["wout1"],
      tabs["m10"][:nsteps], tabs["m20"][:nsteps], tabs["m11"][:nsteps], tabs["m21"][:nsteps],
      tabs["mu10"], tabs["mu20"], tabs["mu11"], tabs["mu21"], init0, init1)
    return y, f0, f1


def _s5_glu_body(u_ref, y_ref, d_ref, w_ref, b_ref, o_ref):
    y = d_ref[...] * u_ref[...] + y_ref[...]
    g = jax.nn.gelu(y)
    o_ref[...] = g * jax.nn.sigmoid(jnp.dot(g.astype(BF16), w_ref[...], preferred_element_type=F32) + b_ref[...])


def _s5_glu(u2, y2, d_skip, glu_w, glu_b, tm, name):
    m, width = u2.shape
    return _rows_call(_s5_glu_body, [u2, y2], [d_skip.reshape(1, -1), glu_w.astype(BF16), glu_b.reshape(1, -1)], [],
                      [((m, width), F32)], tm, m, name)[0]


def _s5_mixer(u_lat, u_ctx, bsz, seq, seq_c, lam_re, lam_im, log_dt, b_re, b_im, c_re, c_im, d_skip, glu_w, glu_b,
              tm, tm_c):
    nsteps = max(1, math.ceil(math.log2(seq // S5_CHUNK)))
    tabs = _s5_tables(lam_re, lam_im, log_dt, b_re, b_im, c_re, c_im, nsteps)
    g = u_lat.shape[1] // S5_GROUP
    zero = jnp.zeros((bsz, g, 2 * S5_STATE), F32)
    y_ctx, f0, f1 = _s5_core(u_ctx, bsz, seq_c, tabs, zero, zero)
    y_lat, _, _ = _s5_core(u_lat, bsz, seq, tabs, f0, f1)
    out_lat = _s5_glu(u_lat, y_lat, d_skip, glu_w, glu_b, tm, "s5_glu")
    out_ctx = _s5_glu(u_ctx, y_ctx, d_skip, glu_w, glu_b, tm_c, "s5_glu_ctx")
    return out_lat, out_ctx


def _rope_tables_1d(seq, hd, heads):
    half = hd // 2
    inv = ROPE_BASE ** (-jnp.arange(half, dtype=F32) / half)
    ang = jnp.arange(seq, dtype=F32)[:, None] * inv[None]
    cos, sin = jnp.cos(ang), jnp.sin(ang)
    return jnp.tile(jnp.concatenate([cos, cos], -1), (1, heads)), jnp.tile(jnp.concatenate([-sin, sin], -1), (1, heads))


def _rope_tables_2d(seq, hd, heads):
    q = hd // 4
    inv = ROPE_BASE ** (-jnp.arange(q, dtype=F32) / q)
    n_rows = seq // GRID_W
    row = jnp.repeat(jnp.arange(n_rows, dtype=F32), GRID_W)
    col = jnp.tile(jnp.arange(GRID_W, dtype=F32), n_rows)
    ar, ac = row[:, None] * inv[None], col[:, None] * inv[None]
    cos = jnp.concatenate([jnp.cos(ar), jnp.cos(ar), jnp.cos(ac), jnp.cos(ac)], -1)
    sin = jnp.concatenate([-jnp.sin(ar), jnp.sin(ar), -jnp.sin(ac), jnp.sin(ac)], -1)
    return jnp.tile(cos, (1, heads)), jnp.tile(sin, (1, heads))


def _kv_update(k, kdec, v):
    kd = (k * kdec).astype(BF16)
    return lax.dot_general(kd, v.astype(BF16), (((0,), (0,)), ((), ())), preferred_element_type=F32)


def _ret_bwd_body(k_ref, v_ref, kdec_ref, gc_ref, init_ref, sprev_ref, fin_ref, s_ref):
    @pl.when(pl.program_id(1) == 0)
    def _():
        s_ref[...] = init_ref[...]

    for h in range(k_ref.shape[0]):
        s = s_ref[h]
        sprev_ref[h] = s
        s = gc_ref[h] * s + _kv_update(k_ref[h], kdec_ref[h], v_ref[h])
        s_ref[h] = s
        fin_ref[h] = s


def _ret_main_body(q_ref, k_ref, v_ref, g_ref, mask_ref, qdf_ref, kdf_ref, qdb_ref, gc_ref, init_ref, sb_ref,
                   o_ref, fin_ref, s_ref):
    @pl.when(pl.program_id(1) == 0)
    def _():
        s_ref[...] = init_ref[...]

    for h in range(q_ref.shape[0]):
        q, k, v = q_ref[h], k_ref[h], v_ref[h]
        vb = v.astype(BF16)
        sc = lax.dot_general(q.astype(BF16), k.astype(BF16), (((1,), (1,)), ((), ())), preferred_element_type=F32)
        o = jnp.dot((sc * mask_ref[h]).astype(BF16), vb, preferred_element_type=F32)
        s = s_ref[h]
        o = o + jnp.dot((q * qdf_ref[h]).astype(BF16), s.astype(BF16), preferred_element_type=F32)
        o = o + jnp.dot((q * qdb_ref[h]).astype(BF16), sb_ref[h].astype(BF16), preferred_element_type=F32)
        o = o * lax.rsqrt(jnp.mean(o * o, axis=-1, keepdims=True) + EPS)
        o_ref[h] = o * jax.nn.silu(g_ref[h])
        s = gc_ref[h] * s + _kv_update(k, kdf_ref[h], v)
        s_ref[h] = s
        fin_ref[h] = s


def _ret_tables(decay_logit):
    cl = RET_CHUNK
    log_g = jax.nn.log_sigmoid(decay_logit.astype(F32))
    idx = jnp.arange(cl, dtype=F32)
    diff = idx[:, None] - idx[None, :]
    mf = jnp.where(diff[None] >= 0, jnp.exp(jnp.maximum(diff, 0.0)[None] * log_g[0][:, None, None]), 0.0)
    mb = jnp.where(diff[None] <= 0, jnp.exp(jnp.maximum(-diff, 0.0)[None] * log_g[1][:, None, None]), 0.0)

    def col(e, d):
        return jnp.exp(e[None, :] * log_g[d][:, None])[:, :, None]

    return dict(mask=mf + mb, qdf=col(idx + 1.0, 0), kdf=col(cl - 1.0 - idx, 0), qdb=col(cl - idx, 1), kdb=col(idx, 1),
                gcf=jnp.exp(cl * log_g[0])[:, None, None], gcb=jnp.exp(cl * log_g[1])[:, None, None])


def _retention(q, k, v, g, bsz, seq, tabs, init_f, init_b):
    heads, _, dk = q.shape
    dv = v.shape[2]
    cl = RET_CHUNK
    n = seq // cl

    def whole(a):
        return pl.BlockSpec(a.shape, lambda b, i, nd=a.ndim: (0,) * nd)

    state_spec = pl.BlockSpec((None, heads, dk, dv), lambda b, i: (b, 0, 0, 0))
    state_shape = jax.ShapeDtypeStruct((bsz, heads, dk, dv), F32)
    rev = lambda b, i: (0, b * n + (n - 1 - i), 0)
    sprev_b, fin_b = pl.pallas_call(
        _ret_bwd_body,
        grid=(bsz, n),
        in_specs=[pl.BlockSpec((heads, cl, dk), rev), pl.BlockSpec((heads, cl, dv), rev),
                  whole(tabs["kdb"]), whole(tabs["gcb"]), state_spec],
        out_specs=[pl.BlockSpec((None, None, heads, dk, dv), lambda b, i: (b, n - 1 - i, 0, 0, 0)), state_spec],
        out_shape=[jax.ShapeDtypeStruct((bsz, n, heads, dk, dv), F32), state_shape],
        scratch_shapes=[pltpu.VMEM((heads, dk, dv), F32)],
        compiler_params=_cparams(("parallel", "arbitrary")),
        name="retention_backward_states",
    )(k, v, tabs["kdb"], tabs["gcb"], init_b)
    fwd = lambda b, i: (0, b * n + i, 0)
    o, fin_f = pl.pallas_call(
        _ret_main_body,
        grid=(bsz, n),
        in_specs=[pl.BlockSpec((heads, cl, dk), fwd), pl.BlockSpec((heads, cl, dk), fwd),
                  pl.BlockSpec((heads, cl, dv), fwd), pl.BlockSpec((heads, cl, dv), fwd),
                  whole(tabs["mask"]), whole(tabs["qdf"]), whole(tabs["kdf"]), whole(tabs["qdb"]), whole(tabs["gcf"]),
                  state_spec, pl.BlockSpec((None, None, heads, dk, dv), lambda b, i: (b, i, 0, 0, 0))],
        out_specs=[pl.BlockSpec((heads, cl, dv), fwd), state_spec],
        out_shape=[jax.ShapeDtypeStruct((heads, bsz * seq, dv), F32), state_shape],
        scratch_shapes=[pltpu.VMEM((heads, dk, dv), F32)],
        compiler_params=_cparams(("parallel", "arbitrary")),
        name="retention",
    )(q, k, v, g, tabs["mask"], tabs["qdf"], tabs["kdf"], tabs["qdb"], tabs["gcf"], init_f, sprev_b)
    return o, fin_f, fin_b


def _swa_body(*refs, local, seq, scale):
    if local:
        q_ref, kp_ref, kc_ref, kn_ref, vp_ref, vc_ref, vn_ref, kx_ref, vx_ref, sink_ref, o_ref = refs
    else:
        q_ref, kx_ref, vx_ref, sink_ref, o_ref = refs
    i = pl.program_id(1)
    bk = q_ref.shape[1]
    nkv = kx_ref.shape[0]
    grp = q_ref.shape[0] // nkv
    nt = (((1,), (1,)), ((), ()))
    hd = q_ref.shape[2]
    rows = grp * bk
    if local:
        row = lax.broadcasted_iota(jnp.int32, (rows, 3 * bk), 0) % bk
        col = lax.broadcasted_iota(jnp.int32, (rows, 3 * bk), 1)
        key_pos = (i - 1) * bk + col
        valid = (jnp.abs(col - (row + bk)) <= SWA_WINDOW) & (key_pos >= 0) & (key_pos < seq)
    for kv in range(nkv):
        kx = kx_ref[kv].astype(BF16)
        vx = vx_ref[kv].astype(BF16)
        q = q_ref[kv * grp:(kv + 1) * grp].reshape(rows, hd).astype(BF16)
        sink = jnp.concatenate([jnp.broadcast_to(sink_ref[kv * grp + gi], (bk, 1)) for gi in range(grp)], axis=0)
        s_x = lax.dot_general(q, kx, nt, preferred_element_type=F32) * scale
        m = jnp.maximum(jnp.max(s_x, axis=-1, keepdims=True), sink)
        if local:
            kl = jnp.concatenate([kp_ref[kv], kc_ref[kv], kn_ref[kv]], axis=0).astype(BF16)
            vl = jnp.concatenate([vp_ref[kv], vc_ref[kv], vn_ref[kv]], axis=0).astype(BF16)
            s_l = lax.dot_general(q, kl, nt, preferred_element_type=F32) * scale
            s_l = jnp.where(valid, s_l, NEG_INF)
            m = jnp.maximum(m, jnp.max(s_l, axis=-1, keepdims=True))
        p_x = jnp.exp(s_x - m)
        den = jnp.sum(p_x, axis=-1, keepdims=True) + jnp.exp(sink - m)
        o = jnp.dot(p_x.astype(BF16), vx, preferred_element_type=F32)
        if local:
            p_l = jnp.exp(s_l - m)
            den = den + jnp.sum(p_l, axis=-1, keepdims=True)
            o = o + jnp.dot(p_l.astype(BF16), vl, preferred_element_type=F32)
        o_ref[kv * grp:(kv + 1) * grp] = (o / den).reshape(grp, bk, hd)


def _swa(q, k, v, kx, vx, sink, bsz, seq, seq_c, local):
    hq, m, hd = q.shape
    hkv = kx.shape[0]
    bk = SWA_BLOCK
    nb = seq // bk
    scale = hd ** -0.5
    cur = lambda b, i: (0, b * nb + i, 0)
    prv = lambda b, i: (0, b * nb + jnp.maximum(i - 1, 0), 0)
    nxt = lambda b, i: (0, b * nb + jnp.minimum(i + 1, nb - 1), 0)
    ctx_spec = pl.BlockSpec((hkv, seq_c, hd), lambda b, i: (0, b, 0))
    sink_spec = pl.BlockSpec((hq, 1, 1), lambda b, i: (0, 0, 0))
    kvb = lambda f: pl.BlockSpec((hkv, bk, hd), f)
    in_specs = [pl.BlockSpec((hq, bk, hd), cur)]
    args = [q]
    if local:
        in_specs += [kvb(prv), kvb(cur), kvb(nxt), kvb(prv), kvb(cur), kvb(nxt)]
        args += [k, k, k, v, v, v]
    in_specs += [ctx_spec, ctx_spec, sink_spec]
    args += [kx, vx, sink.astype(F32).reshape(hq, 1, 1)]
    return pl.pallas_call(
        functools.partial(_swa_body, local=local, seq=seq, scale=scale),
        grid=(bsz, nb),
        in_specs=in_specs,
        out_specs=pl.BlockSpec((hq, bk, hd), cur),
        out_shape=jax.ShapeDtypeStruct((hq, m, hd), F32),
        compiler_params=_cparams(("parallel", "parallel")),
        name="swa" if local else "swa_ctx",
    )(*args)


def _mixer_cd_in(x2, g, sc, sh, w_in, d, seq, tm, rows_per_batch, name, rope):
    hd = d // 16
    qk, vw = RET_HEADS * hd, 2 * RET_HEADS * hd
    qw, kw = SWA_Q_HEADS * hd, SWA_KV_HEADS * hd
    if rope:
        t2 = _rope_tables_2d(seq, hd, 1)
        tables = [_rope_tables_1d(seq, hd, RET_HEADS), [jnp.tile(t, (1, SWA_Q_HEADS)) for t in t2],
                  [jnp.tile(t, (1, SWA_KV_HEADS)) for t in t2]]
        post = [(0, hd // 2, hd ** -0.5), (0, hd // 2, 1.0), (1, hd // 4, 1.0), (2, hd // 4, 1.0)]
    else:
        tables = []
        post = [(None, 0, hd ** -0.5), None, None, None]
    splits = [(qk, RET_HEADS, post[0]), (qk, RET_HEADS, post[1]), (vw, RET_HEADS, None), (vw, RET_HEADS, None),
              (qw, SWA_Q_HEADS, post[2]), (kw, SWA_KV_HEADS, post[3]), (kw, SWA_KV_HEADS, None)]
    names = ("rq", "rk", "rv", "rg", "sq", "sk", "sv")
    return dict(zip(names, _norm_mod_matmul(x2, g, sc, sh, w_in, splits, tm, rows_per_batch, name, tables)))


def _mixer_cd_core(lat, cx, bsz, seq, seq_c, decay_logit, sink, with_ctx):
    dk, dv = lat["rq"].shape[2], lat["rv"].shape[2]
    tabs = _ret_tables(decay_logit)
    zero = jnp.zeros((bsz, RET_HEADS, dk, dv), F32)
    ro_c, fin_f, fin_b = _retention(cx["rq"], cx["rk"], cx["rv"], cx["rg"], bsz, seq_c, tabs, zero, zero)
    ro_l, _, _ = _retention(lat["rq"], lat["rk"], lat["rv"], lat["rg"], bsz, seq, tabs, fin_f, fin_b)
    so_l = _swa(lat["sq"], lat["sk"], lat["sv"], cx["sk"], cx["sv"], sink, bsz, seq, seq_c, True)
    so_c = None
    if with_ctx:
        so_c = _swa(cx["sq"], None, None, cx["sk"], cx["sv"], sink, bsz, seq_c, seq_c, False)
    return ro_l, so_l, (ro_c if with_ctx else None), so_c


def _store_tile_rows(ref, val, base=0):
    r, d = val.shape
    sub = d // LANES
    for c in range(sub):
        ref[pl.ds(base + c, r, stride=sub), :] = val[:, c * LANES:(c + 1) * LANES]


def _load_tile_rows(ref, r, sub, base=0):
    return jnp.concatenate([ref[pl.ds(base + c, r, stride=sub), :] for c in range(sub)], axis=1)


def _router_body(x_ref, g_ref, rw_ref, rb_ref, tril_ref, sc_ref, sh_ref, hx_ref, ti_ref, gt_ref, pos_ref, cnt_ref):
    @pl.when(pl.program_id(0) == 0)
    def _():
        cnt_ref[...] = jnp.zeros_like(cnt_ref)

    hx = _norm_mod(x_ref[...], g_ref[...], sc_ref[...], sh_ref[...])
    _store_tile_rows(hx_ref, hx)
    logits = jnp.dot(hx, rw_ref[...], precision=HI, preferred_element_type=F32) + rb_ref[...]
    lane = lax.broadcasted_iota(jnp.int32, logits.shape, 1).astype(F32)
    rem = logits
    vals, hots = [], []
    for k in range(TOP_K):
        m = jnp.max(rem, axis=-1, keepdims=True)
        idx = jnp.min(jnp.where(rem == m, lane, float(N_EXPERTS)), axis=-1, keepdims=True)
        hot = lane == idx
        rem = jnp.where(hot, NEG_INF, rem)
        vals.append(m)
        hots.append(hot.astype(F32))
        ti_ref[:, k:k + 1] = idx.astype(jnp.int32)
    exps = [jnp.exp(v - vals[0]) for v in vals]
    den = exps[0] + exps[1] + exps[2] + exps[3]
    for k in range(TOP_K):
        gt_ref[:, k:k + 1] = exps[k] / den
    sel = hots[0] + hots[1] + hots[2] + hots[3]
    before = jnp.dot(tril_ref[...], sel.astype(BF16), preferred_element_type=F32) + cnt_ref[...]
    for k in range(TOP_K):
        pos_ref[:, k:k + 1] = jnp.sum(hots[k] * before, axis=-1, keepdims=True).astype(jnp.int32)
    cnt_ref[...] += jnp.sum(sel, axis=0, keepdims=True)


def _moe_route(x2, g, sc, sh, router_w, router_b, tr, rows_per_batch):
    n, d = x2.shape
    tiles_per_batch = rows_per_batch // tr
    tril = jnp.asarray(np.tril(np.ones((tr, tr), np.float32), -1)).astype(BF16)
    whole = lambda a: pl.BlockSpec(a.shape, lambda i, nd=a.ndim: (0,) * nd)
    bat = pl.BlockSpec((None, 1, d), lambda i: (i // tiles_per_batch, 0, 0))
    g2, rb2 = g.reshape(1, -1), router_b.reshape(1, -1)
    small = lambda dt: jax.ShapeDtypeStruct((n, TOP_K), dt)
    small_spec = pl.BlockSpec((tr, TOP_K), lambda i: (i, 0))
    return pl.pallas_call(
        _router_body,
        grid=(n // tr,),
        in_specs=[pl.BlockSpec((tr, d), lambda i: (i, 0)), whole(g2), whole(router_w), whole(rb2), whole(tril), bat, bat],
        out_specs=[pl.BlockSpec((tr * (d // LANES), LANES), lambda i: (i, 0)), small_spec, small_spec, small_spec,
                   pl.BlockSpec((1, N_EXPERTS), lambda i: (0, 0))],
        out_shape=[jax.ShapeDtypeStruct((n * (d // LANES), LANES), F32), small(jnp.int32), small(F32), small(jnp.int32),
                   jax.ShapeDtypeStruct((1, N_EXPERTS), F32)],
        compiler_params=_cparams(("arbitrary",)),
        name="moe_router",
    )(x2, g2, router_w, rb2, tril, sc, sh)


def _tile_row_copy(src, s_off, dst, d_off, sem, sub):
    return pltpu.make_async_copy(src.at[pl.ds(pl.multiple_of(s_off, sub), sub)],
                                 dst.at[pl.ds(pl.multiple_of(d_off, sub), sub)], sem)


def _dispatch_body(dest_ref, hx_ref, xs_in_ref, xs_ref, sem, *, td, sub):
    del xs_in_ref

    def issue(n, carry):
        for k in range(TOP_K):
            _tile_row_copy(hx_ref, n * sub, xs_ref, dest_ref[n * TOP_K + k], sem, sub).start(priority=k % 2)
        return carry

    def drain(n, carry):
        for k in range(TOP_K):
            _tile_row_copy(hx_ref, 0, xs_ref, 0, sem, sub).wait()
        return carry

    lax.fori_loop(0, td, issue, 0, unroll=DMA_UNROLL)
    lax.fori_loop(0, td, drain, 0, unroll=DMA_UNROLL)


def _moe_dispatch(hx, dest_off, n_slots, td, sub, xs_init=None):
    lanes = hx.shape[1]
    n = hx.shape[0] // sub
    zeros = jnp.zeros((n_slots * sub, lanes), F32) if xs_init is None else xs_init
    return pl.pallas_call(
        functools.partial(_dispatch_body, td=td, sub=sub),
        grid=(n // td,),
        in_specs=[pl.BlockSpec((td * TOP_K,), lambda i: (i,), memory_space=pltpu.SMEM),
                  pl.BlockSpec((td * sub, lanes), lambda i: (i, 0)), pl.BlockSpec(memory_space=pl.ANY)],
        out_specs=pl.BlockSpec(memory_space=pl.ANY),
        out_shape=jax.ShapeDtypeStruct((n_slots * sub, lanes), F32),
        scratch_shapes=[pltpu.SemaphoreType.DMA(())],
        input_output_aliases={2: 0},
        compiler_params=pltpu.CompilerParams(dimension_semantics=("arbitrary",), has_side_effects=True,
                                             vmem_limit_bytes=VMEM_LIMIT),
        name="moe_dispatch",
    )(dest_off, hx, zeros)


def _ffn_body(be_ref, nu_ref, x_ref, wgu_ref, bgu_ref, wdn_ref, bdn_ref, o_ref, wgu_bf, wdn_bf, *, tm):
    j = pl.program_id(0)
    e = be_ref[j]
    prev = be_ref[jnp.maximum(j - 1, 0)]

    @pl.when((j == 0) | (e != prev))
    def _():
        wgu_bf[...] = wgu_ref[...].astype(BF16)
        wdn_bf[...] = wdn_ref[...].astype(BF16)

    @pl.when(j < nu_ref[0])
    def _():
        f = wdn_ref.shape[0]
        x = _load_tile_rows(x_ref, tm, wgu_ref.shape[0] // LANES).astype(BF16)
        gu = jnp.dot(x, wgu_bf[...], preferred_element_type=F32) + bgu_ref[...]
        gate = jnp.minimum(gu[:, :f], SWIGLU_LIMIT)
        up = jnp.clip(gu[:, f:], -SWIGLU_LIMIT, SWIGLU_LIMIT)
        act = gate * jax.nn.sigmoid(SWIGLU_ALPHA * gate) * (up + 1.0)
        _store_tile_rows(o_ref, jnp.dot(act.astype(BF16), wdn_bf[...], preferred_element_type=F32) + bdn_ref[...])

    @pl.when(j >= nu_ref[0])
    def _():
        o_ref[...] = jnp.zeros_like(o_ref)


def _moe_ffn(xs, block_exp, n_used, layer, w_gu, b_gu, w_dn, b_dn, tm):
    depth, n_exp, d, f2 = w_gu.shape
    f = w_dn.shape[2]
    sub = d // LANES
    n_slots = xs.shape[0] // sub
    blk = lambda j, be, nu: (jnp.minimum(j, nu[0] - 1), 0)
    exp4 = lambda j, be, nu: (layer, be[j], 0, 0)
    grid_spec = pltpu.PrefetchScalarGridSpec(
        num_scalar_prefetch=2,
        grid=(n_slots // tm,),
        in_specs=[pl.BlockSpec((tm * sub, LANES), blk), pl.BlockSpec((None, None, d, f2), exp4),
                  pl.BlockSpec((None, None, 1, f2), exp4), pl.BlockSpec((None, None, f, d), exp4),
                  pl.BlockSpec((None, None, 1, d), exp4)],
        out_specs=pl.BlockSpec((tm * sub, LANES), lambda j, be, nu: (j, 0)),
        scratch_shapes=[pltpu.VMEM((d, f2), BF16), pltpu.VMEM((f, d), BF16)],
    )
    return pl.pallas_call(
        functools.partial(_ffn_body, tm=tm),
        grid_spec=grid_spec,
        out_shape=jax.ShapeDtypeStruct((n_slots * sub, LANES), F32),
        compiler_params=_cparams(("arbitrary",)),
        name="moe_ffn",
    )(block_exp, n_used, xs, w_gu, b_gu.reshape(depth, n_exp, 1, f2), w_dn, b_dn.reshape(depth, n_exp, 1, d))


def _combine_body(dest_ref, gt_ref, x_ref, ys_ref, g_ref, o_ref, buf, sem, *, tc, sub):
    def issue(n, carry):
        for k in range(TOP_K):
            _tile_row_copy(ys_ref, dest_ref[n * TOP_K + k], buf, (k * tc + n) * sub, sem, sub).start(priority=k % 2)
        return carry

    def drain(n, carry):
        for k in range(TOP_K):
            _tile_row_copy(ys_ref, 0, buf, 0, sem, sub).wait()
        return carry

    lax.fori_loop(0, tc, issue, 0, unroll=DMA_UNROLL)
    lax.fori_loop(0, tc, drain, 0, unroll=DMA_UNROLL)
    gates = [jnp.broadcast_to(gt_ref[:, k:k + 1], (tc, LANES)) for k in range(TOP_K)]
    for c in range(sub):
        cols = slice(c * LANES, (c + 1) * LANES)
        acc = gates[0] * buf[pl.ds(c, tc, stride=sub), :]
        for k in range(1, TOP_K):
            acc = acc + gates[k] * buf[pl.ds(k * tc * sub + c, tc, stride=sub), :]
        o_ref[:, cols] = x_ref[:, cols] + g_ref[:, cols] * acc


def _moe_combine(ys, dest_off, gates, x2, gate2, tc, rows_per_batch):
    n, d = x2.shape
    sub = d // LANES
    tiles_per_batch = rows_per_batch // tc
    return pl.pallas_call(
        functools.partial(_combine_body, tc=tc, sub=sub),
        grid=(n // tc,),
        in_specs=[pl.BlockSpec((tc * TOP_K,), lambda i: (i,), memory_space=pltpu.SMEM),
                  pl.BlockSpec((tc, TOP_K), lambda i: (i, 0)), pl.BlockSpec((tc, d), lambda i: (i, 0)),
                  pl.BlockSpec(memory_space=pl.ANY),
                  pl.BlockSpec((None, 1, d), lambda i: (i // tiles_per_batch, 0, 0))],
        out_specs=pl.BlockSpec((tc, d), lambda i: (i, 0)),
        out_shape=jax.ShapeDtypeStruct((n, d), F32),
        scratch_shapes=[pltpu.VMEM((TOP_K * tc * sub, LANES), F32), pltpu.SemaphoreType.DMA(())],
        compiler_params=_cparams(("arbitrary",)),
        name="moe_combine",
    )(dest_off, gates, x2, ys, gate2)


def _moe_slot_blocks(n_tok):
    return -(-n_tok * TOP_K // MOE_TM) + N_EXPERTS


def _moe_layer(streams, g, router_w, router_b, layer, w_gu, b_gu, w_dn, b_dn, n_blocks, slots=None):
    tm = MOE_TM
    sub = streams[0][0].shape[1] // LANES
    routed = [_moe_route(x2, g, sc, sh, router_w, router_b, tr, rpb) for x2, sc, sh, _, rpb, tr, _ in streams]
    counts = [r[4][0].astype(jnp.int32) for r in routed]
    total = functools.reduce(lambda a, b: a + b, counts)
    padded = (total + tm - 1) // tm * tm
    pad_end = jnp.cumsum(padded)
    assert n_blocks >= _moe_slot_blocks(sum(s[0].shape[0] for s in streams))
    n_used = (pad_end[-1] // tm).astype(jnp.int32)
    blk_ids = jnp.arange(n_blocks, dtype=jnp.int32)
    last_row = jnp.minimum(blk_ids, n_used - 1) * tm
    block_exp = jnp.sum((pad_end[None, :] <= last_row[:, None]).astype(jnp.int32), axis=1)
    block_exp = jnp.minimum(block_exp, N_EXPERTS - 1)
    start = pad_end - padded
    xs, dests = slots, []
    for (x2, _, _, _, _, tr, _), (hx, top_i, _, pos, _), cnt in zip(streams, routed, counts):
        dest_off = ((start[top_i] + pos) * sub).astype(jnp.int32).reshape(-1)
        xs = _moe_dispatch(hx, dest_off, n_blocks * tm, tr, sub, xs)
        dests.append(dest_off)
        start = start + cnt
    ys = _moe_ffn(xs, block_exp, n_used.reshape(1), layer, w_gu, b_gu, w_dn, b_dn, tm)
    outs = [_moe_combine(ys, dest_off, r[2], x2, gate2, tc, rpb)
            for (x2, _, _, gate2, rpb, _, tc), r, dest_off in zip(streams, routed, dests)]
    return outs, ys


def _mod_body(c_ref, w_ref, b_ref, o_ref):
    o_ref[...] = jnp.dot(jax.nn.silu(c_ref[...]), w_ref[...], precision=HI, preferred_element_type=F32) + b_ref[...]


def _modulation(cc, mod_w, mod_b):
    depth, d, d6 = mod_w.shape
    r = cc.shape[0]
    return pl.pallas_call(
        _mod_body,
        grid=(depth, d6 // d),
        in_specs=[pl.BlockSpec((r, d), lambda l, j: (0, 0)), pl.BlockSpec((None, d, d), lambda l, j: (l, 0, j)),
                  pl.BlockSpec((None, 1, d), lambda l, j: (l, 0, j))],
        out_specs=pl.BlockSpec((None, r, d), lambda l, j: (l, 0, j)),
        out_shape=jax.ShapeDtypeStruct((depth, r, d6), F32),
        compiler_params=_cparams(("parallel", "parallel")),
        name="modulation",
    )(cc, mod_w, mod_b.reshape(depth, 1, d6))


def kernel(x, c, ctx, c_ctx, mod_w, mod_b, norm1_g, norm2_g, ab_w_in, ab_w_out, hy_short_w, hy_short_b, hy_f_w1, hy_f_b1, hy_f_w2, hy_f_b2, hy_f_w3, hy_bias, s5_lambda_re, s5_lambda_im, s5_log_dt, s5_b_re, s5_b_im, s5_c_re, s5_c_im, s5_d, s5_glu_w, s5_glu_b, cd_w_in, cd_w_out, ret_decay_logit, swa_sink, router_w, router_b, exp_w_gu, exp_b_gu, exp_w_down, exp_b_down, final_g):
    bsz, seq, d = x.shape
    seq_c = ctx.shape[1]
    depth = mod_w.shape[0]
    m_l, m_c = bsz * seq, bsz * seq_c
    tm, tm_c, tt = 512, 256, 256
    xl = x.reshape(m_l, d)
    xc = ctx.reshape(m_c, d)
    cc = jnp.concatenate([c, c_ctx[None], jnp.zeros((SUBLANES - bsz - 1, d), F32)], axis=0)
    mods = _modulation(cc, mod_w, mod_b)
    hy_w = hy_f_w3.shape[2]
    n_blocks = _moe_slot_blocks(m_l + m_c)
    slots = None
    for layer in range(depth):
        with_ctx = layer < depth - 1
        i = layer // 2
        sh1, sc1, g1, sh2, sc2, g2 = [t[:, None, :] for t in jnp.split(mods[layer, :bsz], 6, axis=-1)]
        csh1, csc1, cg1, csh2, csc2, cg2 = [t[:, None, :] for t in jnp.split(mods[layer, bsz:bsz + 1], 6, axis=-1)]
        if layer % 2 == 0:
            w_in, w_out = ab_w_in[i], ab_w_out[i]
            splits = [(3 * hy_w, 0, None), (w_in.shape[1] - 3 * hy_w, 0, None)]
            pa, pb = _norm_mod_matmul(xl, norm1_g[layer], sc1, sh1, w_in, splits, tm, seq, "ab_in")
            pac, pbc = _norm_mod_matmul(xc, norm1_g[layer], csc1, csh1, w_in, splits, tm_c, m_c, "ab_in_ctx")
            hy = (hy_short_w[i], hy_short_b[i], hy_f_w1[i], hy_f_b1[i], hy_f_w2[i], hy_f_b2[i], hy_f_w3[i], hy_bias[i])
            ya = _hyena(pa, bsz, seq, *hy, tt)
            yb, ybc = _s5_mixer(pb, pbc, bsz, seq, seq_c, s5_lambda_re[i], s5_lambda_im[i], s5_log_dt[i], s5_b_re[i],
                                s5_b_im[i], s5_c_re[i], s5_c_im[i], s5_d[i], s5_glu_w[i], s5_glu_b[i], tm, tm_c)
            ws = [w_out[:hy_w], w_out[hy_w:]]
            xl = _out_proj([ya, yb], ws, xl, g1, tm, seq, "ab_out")
            if with_ctx:
                yac = _hyena(pac, bsz, seq_c, *hy, tt)
                xc = _out_proj([yac, ybc], ws, xc, cg1, tm_c, m_c, "ab_out_ctx")
        else:
            w_in, w_out = cd_w_in[i], cd_w_out[i]
            vw = 2 * RET_HEADS * (d // 16)
            qw = SWA_Q_HEADS * (d // 16)
            lat = _mixer_cd_in(xl, norm1_g[layer], sc1, sh1, w_in, d, seq, tm, seq, "cd_in", True)
            cx = _mixer_cd_in(xc, norm1_g[layer], csc1, csh1, w_in, d, seq_c, tm_c, m_c, "cd_in_ctx", False)
            ro_l, so_l, ro_c, so_c = _mixer_cd_core(lat, cx, bsz, seq, seq_c, ret_decay_logit[i], swa_sink[i],
                                                    with_ctx)
            ws = [w_out[:vw].reshape(RET_HEADS, vw // RET_HEADS, d), w_out[vw:].reshape(SWA_Q_HEADS, qw // SWA_Q_HEADS, d)]
            xl = _out_proj([ro_l, so_l], ws, xl, g1, tm, seq, "cd_out")
            if with_ctx:
                xc = _out_proj([ro_c, so_c], ws, xc, cg1, tm_c, m_c, "cd_out_ctx")
        moe_w = (router_w[layer], router_b[layer], layer, exp_w_gu, exp_b_gu, exp_w_down, exp_b_down)
        streams = [(xl, sc2, sh2, g2, seq, tm, tm_c)]
        if with_ctx:
            streams.append((xc, csc2, csh2, cg2, m_c, tm_c, tm_c))
        outs, slots = _moe_layer(streams, norm2_g[layer], *moe_w, n_blocks, slots)
        xl = outs[0]
        if with_ctx:
            xc = outs[1]
    out = _rows_call(_final_norm_body, [xl], [final_g.reshape(1, -1)], [], [((m_l, d), F32)], tm, m_l, "final_norm")[0]
    return out.reshape(bsz, seq, d)
```

```python
import functools
import math

import numpy as np
import jax
import jax.numpy as jnp
from jax import lax
from jax.experimental import pallas as pl
from jax.experimental.pallas import tpu as pltpu

F32 = jnp.float32
BF16 = jnp.bfloat16
HI = lax.Precision.HIGHEST

EPS = 1e-6
NEG_INF = -1e30
ROPE_BASE = 10000.0
GRID_W = 64

HY_SHORT = 3
HY_BANDS = 16
HY_SHIFT = 0.05
HY_FAST_DECAY = math.log(1e-2) / 0.3
HY_SLOW_DECAY = math.log(1e-2) / 1.5
S5_GROUP = 16
S5_STATE = 64
S5_CHUNK = 8
RET_HEADS = 4
RET_CHUNK = 128
SWA_Q_HEADS = 8
SWA_KV_HEADS = 2
SWA_WINDOW = 128
SWA_BLOCK = 128
N_EXPERTS = 32
TOP_K = 4
SWIGLU_LIMIT = 7.0
SWIGLU_ALPHA = 1.702

LANES = 128
SUBLANES = 8
VMEM_LIMIT = 52 * 2**20
FFT_B = 128
MOE_TM = 512
DMA_UNROLL = 4


def _cparams(sem):
    return pltpu.CompilerParams(dimension_semantics=sem, vmem_limit_bytes=VMEM_LIMIT)


def _rows_call(body, rows, consts, batched, outs, tm, rows_per_batch, name, periodic=()):
    m = rows[0].shape[-2]
    assert m % tm == 0 and rows_per_batch % tm == 0
    tiles_per_batch = rows_per_batch // tm

    def row_spec(shape):
        if len(shape) == 2:
            return pl.BlockSpec((tm, shape[1]), lambda i: (i, 0))
        return pl.BlockSpec((shape[0], tm, shape[2]), lambda i: (0, i, 0))

    in_specs = [row_spec(a.shape) for a in rows]
    for a in periodic:
        in_specs.append(pl.BlockSpec((tm, a.shape[1]), lambda i: (i % tiles_per_batch, 0)))
    for a in consts:
        in_specs.append(pl.BlockSpec(a.shape, lambda i, n=a.ndim: (0,) * n))
    for a in batched:
        in_specs.append(pl.BlockSpec((None, 1, a.shape[2]), lambda i: (i // tiles_per_batch, 0, 0)))
    out_specs = [row_spec(s) for s, _ in outs]
    out_shape = [jax.ShapeDtypeStruct(s, d) for s, d in outs]
    res = pl.pallas_call(
        body,
        grid=(m // tm,),
        in_specs=in_specs,
        out_specs=out_specs,
        out_shape=out_shape,
        compiler_params=_cparams(("parallel",)),
        name=name,
    )(*rows, *periodic, *consts, *batched)
    return res


def _norm_mod(x, g, sc, sh):
    y = x * lax.rsqrt(jnp.mean(x * x, axis=-1, keepdims=True) + EPS) * g
    return y * (1.0 + sc) + sh


def _rotate_half(x, cos, sin, shift):
    w = x.shape[1]
    lane = lax.broadcasted_iota(jnp.int32, x.shape, 1)
    partner = jnp.where(lane % (2 * shift) < shift, pltpu.roll(x, w - shift, axis=1), pltpu.roll(x, shift, axis=1))
    return x * cos + partner * sin


def _norm_mod_matmul_body(x_ref, *refs, splits, n_tab):
    tabs = refs[:2 * n_tab]
    g_ref, w_ref, sc_ref, sh_ref = refs[2 * n_tab:2 * n_tab + 4]
    o_refs = refs[2 * n_tab + 4:]
    h = _norm_mod(x_ref[...], g_ref[...], sc_ref[...], sh_ref[...])
    r = jnp.dot(h.astype(BF16), w_ref[...], preferred_element_type=F32)
    off = 0
    for o_ref, (n, heads, post) in zip(o_refs, splits):
        x = r[:, off:off + n]
        if post is not None:
            table, shift, scale = post
            if table is not None:
                x = _rotate_half(x, tabs[2 * table][...], tabs[2 * table + 1][...], shift)
            if scale != 1.0:
                x = x * scale
        if heads:
            hd = n // heads
            for h_i in range(heads):
                o_ref[h_i] = x[:, h_i * hd:(h_i + 1) * hd]
        else:
            o_ref[...] = x
        off += n


def _norm_mod_matmul(x2, g, sc, sh, w, splits, tm, rows_per_batch, name, tables=()):
    m = x2.shape[0]
    outs = [((heads, m, n // heads), F32) if heads else ((m, n), F32) for n, heads, _ in splits]
    body = functools.partial(_norm_mod_matmul_body, splits=splits, n_tab=len(tables))
    periodic = [t for pair in tables for t in pair]
    return _rows_call(body, [x2], [g.reshape(1, -1), w.astype(BF16)], [sc, sh], outs, tm, rows_per_batch, name,
                      periodic=periodic)


def _out_proj_body(*refs, n_in):
    a_refs = refs[:n_in]
    x_ref = refs[n_in]
    w_refs = refs[n_in + 1: 2 * n_in + 1]
    g_ref = refs[2 * n_in + 1]
    o_ref = refs[2 * n_in + 2]
    acc = None
    for a_ref, w_ref in zip(a_refs, w_refs):
        if a_ref.ndim == 3:
            for h_i in range(a_ref.shape[0]):
                t = jnp.dot(a_ref[h_i].astype(BF16), w_ref[h_i], preferred_element_type=F32)
                acc = t if acc is None else acc + t
        else:
            t = jnp.dot(a_ref[...].astype(BF16), w_ref[...], preferred_element_type=F32)
            acc = t if acc is None else acc + t
    o_ref[...] = x_ref[...] + g_ref[...] * acc


def _out_proj(parts, ws, x2, gate, tm, rows_per_batch, name):
    m, d = x2.shape
    body = functools.partial(_out_proj_body, n_in=len(parts))
    ws = [w.astype(BF16) for w in ws]
    return _rows_call(body, list(parts) + [x2], ws, [gate], [((m, d), F32)], tm, rows_per_batch, name)[0]


def _final_norm_body(x_ref, g_ref, o_ref):
    x = x_ref[...]
    o_ref[...] = x * lax.rsqrt(jnp.mean(x * x, axis=-1, keepdims=True) + EPS) * g_ref[...]


def _hy_prep_body(u_ref, p_ref, n_ref, w_ref, b_ref, x0_ref, z_ref, *, width):
    i = pl.program_id(1)
    last = pl.num_programs(1) - 1
    u = u_ref[...]
    tt = u.shape[0]
    prev_row = jnp.where(i == 0, 0.0, p_ref[SUBLANES - 1:SUBLANES, :])
    next_row = jnp.where(i == last, 0.0, n_ref[0:1, :])
    rows = lax.broadcasted_iota(jnp.int32, (tt, 1), 0)
    up = jnp.where(rows == 0, prev_row, pltpu.roll(u, 1, axis=0))
    dn = jnp.where(rows == tt - 1, next_row, pltpu.roll(u, tt - 1, axis=0))
    y = w_ref[0:1, :] * up + w_ref[1:2, :] * u + w_ref[2:3, :] * dn + b_ref[...]
    x0_ref[...] = y[:, :width]
    z_ref[...] = y[:, 2 * width:] * y[:, width:2 * width]


def _hyena_prep(p, short_w, short_b, bsz, seq, tt):
    w3 = p.shape[1]
    width = w3 // 3
    p3 = p.reshape(bsz, seq, w3)
    nt = seq // tt
    sub = tt // SUBLANES
    nsub = seq // SUBLANES
    body = functools.partial(_hy_prep_body, width=width)
    x0, z = pl.pallas_call(
        body,
        grid=(bsz, nt),
        in_specs=[
            pl.BlockSpec((None, tt, w3), lambda b, i: (b, i, 0)),
            pl.BlockSpec((None, SUBLANES, w3), lambda b, i: (b, jnp.maximum(i * sub - 1, 0), 0)),
            pl.BlockSpec((None, SUBLANES, w3), lambda b, i: (b, jnp.minimum((i + 1) * sub, nsub - 1), 0)),
            pl.BlockSpec((HY_SHORT, w3), lambda b, i: (0, 0)),
            pl.BlockSpec((1, w3), lambda b, i: (0, 0)),
        ],
        out_specs=[pl.BlockSpec((None, tt, width), lambda b, i: (b, i, 0))] * 2,
        out_shape=[jax.ShapeDtypeStruct((bsz, seq, width), F32)] * 2,
        compiler_params=_cparams(("parallel", "parallel")),
        name="hyena_prep",
    )(p3, p3, p3, short_w, short_b.reshape(1, -1))
    return x0, z


def _hy_filter_body(t_ref, w_ref, lag_ref, bands_ref, deltas_ref, w1t_ref, w1c_ref, w1s_ref, b1_ref,
                    w2_ref, b2_ref, w3_ref, h_ref, s_ref):
    i = pl.program_id(0)
    arg = w_ref[...] * bands_ref[...]
    pre = (t_ref[...] * w1t_ref[...]
           + jnp.dot(jnp.cos(arg), w1c_ref[...], precision=HI, preferred_element_type=F32)
           + jnp.dot(-jnp.sin(arg), w1s_ref[...], precision=HI, preferred_element_type=F32)
           + b1_ref[...])
    h1 = jnp.sin(pre)
    h2 = jnp.sin(jnp.dot(h1, w2_ref[...], precision=HI, preferred_element_type=F32) + b2_ref[...])
    h3 = jnp.dot(h2, w3_ref[...], precision=HI, preferred_element_type=F32)
    h = h3 * (jnp.exp(-lag_ref[...] * deltas_ref[...]) + HY_SHIFT)
    h_ref[...] = h

    @pl.when(i == 0)
    def _():
        s_ref[...] = jnp.zeros_like(s_ref)

    s_ref[...] += jnp.sum(jnp.abs(h), axis=0, keepdims=True)


def _hyena_filter(seq, w1, b1, w2, b2, w3):
    width = w3.shape[1]
    pos = jnp.arange(seq, dtype=F32)
    t = (pos / seq)[:, None]
    w = (2.0 * math.pi * pos / seq)[:, None]
    lag = (jnp.abs(pos - seq // 2) / (seq / 2))[:, None]
    bands = jnp.linspace(1e-4, HY_BANDS - 1, HY_BANDS, dtype=F32)[None]
    deltas = jnp.abs(jnp.linspace(HY_FAST_DECAY, HY_SLOW_DECAY, width, dtype=F32))[None]
    tl = min(seq, 1024)
    col = pl.BlockSpec((tl, 1), lambda i: (i, 0))

    def whole(a):
        return pl.BlockSpec(a.shape, lambda i, n=a.ndim: (0,) * n)

    consts = [bands, deltas, w1[0:1], w1[1:1 + HY_BANDS], w1[1 + HY_BANDS:], b1.reshape(1, -1),
              w2, b2.reshape(1, -1), w3]
    h, s = pl.pallas_call(
        _hy_filter_body,
        grid=(seq // tl,),
        in_specs=[col, col, col] + [whole(a) for a in consts],
        out_specs=[pl.BlockSpec((tl, width), lambda i: (i, 0)), pl.BlockSpec((1, width), lambda i: (0, 0))],
        out_shape=[jax.ShapeDtypeStruct((seq, width), F32), jax.ShapeDtypeStruct((1, width), F32)],
        compiler_params=_cparams(("arbitrary",)),
        name="hyena_filter",
    )(t, w, lag, *consts)
    return h, s


FFT_G = SUBLANES
FFT_CW = 512


def _kron_eye(mat):
    return np.kron(mat, np.eye(FFT_G))


def _dft_tables_k(na, ka, a_lo, a_cnt):
    n = na * FFT_B
    a = np.arange(na)
    b = np.arange(FFT_B)
    ang1 = 2.0 * np.pi * np.outer(a, a) / na
    ang2 = 2.0 * np.pi * np.outer(b, b) / FFT_B
    angt = 2.0 * np.pi * np.outer(b, a) / n
    c1, s1 = np.cos(ang1), np.sin(ang1)
    c2, s2 = np.cos(ang2), np.sin(ang2)
    nbb = FFT_B // FFT_G
    tw1 = angt.reshape(nbb, FFT_G, na).transpose(0, 2, 1).reshape(nbb, na * FFT_G, 1)
    rows = slice(a_lo, a_lo + a_cnt)
    tabs = dict(
        m1=_kron_eye(np.concatenate([c1, -s1], axis=0)[:, :ka]),
        f2=np.block([[c2, s2], [-s2, c2]]),
        f2i=np.block([[c2, -s2], [s2, c2]]),
        m3r=_kron_eye(c1[rows] / n), m3i=_kron_eye(-s1[rows] / n),
        tw1c=np.cos(tw1), tw1s=np.sin(tw1),
        twc_c=np.cos(angt).T[:, :, None], tws_c=np.sin(angt).T[:, :, None],
    )
    return {k: jnp.asarray(v, F32) for k, v in tabs.items()}


def _fftk1_body(x_ref, m_ref, tc_ref, ts_ref, sc_ref, o_ref, *, na):
    ka, g, cw = x_ref.shape
    v = (x_ref[...].reshape(ka * g, cw) * sc_ref[...]).astype(BF16)
    r = jnp.dot(m_ref[...], v, preferred_element_type=F32)
    gr, gi = r[:na * g], r[na * g:]
    tc, ts = tc_ref[...], ts_ref[...]
    o_ref[...] = jnp.concatenate([gr * tc + gi * ts, gi * tc - gr * ts], axis=0).reshape(2 * na, g, cw)


def _fftk_stage1(x5, scale, tabs, na):
    bz, ka, nbb, g, ch = x5.shape
    cw = min(ch, FFT_CW)
    m1 = tabs["m1"].astype(BF16)
    return pl.pallas_call(
        functools.partial(_fftk1_body, na=na),
        grid=(bz, ch // cw, nbb),
        in_specs=[
            pl.BlockSpec((None, ka, None, g, cw), lambda z, q, b: (z, 0, b, 0, q)),
            pl.BlockSpec(m1.shape, lambda z, q, b: (0, 0)),
            pl.BlockSpec((None, na * g, 1), lambda z, q, b: (b, 0, 0)),
            pl.BlockSpec((None, na * g, 1), lambda z, q, b: (b, 0, 0)),
            pl.BlockSpec((1, cw), lambda z, q, b: (0, q)),
        ],
        out_specs=pl.BlockSpec((None, 2 * na, None, g, cw), lambda z, q, b: (z, 0, b, 0, q)),
        out_shape=jax.ShapeDtypeStruct((bz, 2 * na, nbb, g, ch), F32),
        compiler_params=_cparams(("parallel", "parallel", "parallel")),
        name="fft_stage1",
    )(x5, m1, tabs["tw1c"], tabs["tw1s"], scale)


def _fftk2_spec_body(ar_ref, ai_ref, f_ref, o_ref):
    f = f_ref[...].astype(BF16)
    cw = ar_ref.shape[-1]
    for j in range(FFT_G):
        v = jnp.concatenate([ar_ref[j].reshape(FFT_B, cw), ai_ref[j].reshape(FFT_B, cw)], axis=0).astype(BF16)
        o_ref[j] = jnp.dot(f, v, preferred_element_type=F32)


def _fftk2_conv_body(ar_ref, ai_ref, h_ref, f_ref, fi_ref, tc_ref, ts_ref, o_ref):
    f = f_ref[...].astype(BF16)
    fi = fi_ref[...].astype(BF16)
    cw = ar_ref.shape[-1]
    nbb = FFT_B // FFT_G
    for j in range(FFT_G):
        v = jnp.concatenate([ar_ref[j].reshape(FFT_B, cw), ai_ref[j].reshape(FFT_B, cw)], axis=0).astype(BF16)
        x = jnp.dot(f, v, preferred_element_type=F32)
        xr, xi = x[:FFT_B], x[FFT_B:]
        hr, hi = h_ref[j, :FFT_B, :], h_ref[j, FFT_B:, :]
        p = jnp.concatenate([xr * hr - xi * hi, xr * hi + xi * hr], axis=0).astype(BF16)
        q = jnp.dot(fi, p, preferred_element_type=F32)
        qr, qi = q[:FFT_B], q[FFT_B:]
        tc, ts = tc_ref[j], ts_ref[j]
        o_ref[j, 0] = (qr * tc - qi * ts).reshape(nbb, FFT_G, cw)
        o_ref[j, 1] = (qi * tc + qr * ts).reshape(nbb, FFT_G, cw)


def _fftk_stage2_spectrum(a5, tabs, na):
    _, _, nbb, g, ch = a5.shape
    cw = min(ch, FFT_CW)
    ng = na // FFT_G
    return pl.pallas_call(
        _fftk2_spec_body,
        grid=(ch // cw, ng),
        in_specs=[
            pl.BlockSpec((None, FFT_G, nbb, g, cw), lambda q, c: (0, c, 0, 0, q)),
            pl.BlockSpec((None, FFT_G, nbb, g, cw), lambda q, c: (0, ng + c, 0, 0, q)),
            pl.BlockSpec((2 * FFT_B, 2 * FFT_B), lambda q, c: (0, 0)),
        ],
        out_specs=pl.BlockSpec((FFT_G, 2 * FFT_B, cw), lambda q, c: (c, 0, q)),
        out_shape=jax.ShapeDtypeStruct((na, 2 * FFT_B, ch), F32),
        compiler_params=_cparams(("parallel", "parallel")),
        name="fft_stage2_spectrum",
    )(a5, a5, tabs["f2"])


def _fftk_stage2_conv(a5, hspec, tabs, na):
    bz, _, nbb, g, ch = a5.shape
    cw = min(ch, FFT_CW)
    ng = na // FFT_G
    return pl.pallas_call(
        _fftk2_conv_body,
        grid=(ch // cw, ng, bz),
        in_specs=[
            pl.BlockSpec((None, FFT_G, nbb, g, cw), lambda q, c, z: (z, c, 0, 0, q)),
            pl.BlockSpec((None, FFT_G, nbb, g, cw), lambda q, c, z: (z, ng + c, 0, 0, q)),
            pl.BlockSpec((FFT_G, 2 * FFT_B, cw), lambda q, c, z: (c, 0, q)),
            pl.BlockSpec((2 * FFT_B, 2 * FFT_B), lambda q, c, z: (0, 0)),
            pl.BlockSpec((2 * FFT_B, 2 * FFT_B), lambda q, c, z: (0, 0)),
            pl.BlockSpec((FFT_G, FFT_B, 1), lambda q, c, z: (c, 0, 0)),
            pl.BlockSpec((FFT_G, FFT_B, 1), lambda q, c, z: (c, 0, 0)),
        ],
        out_specs=pl.BlockSpec((None, None, FFT_G, 2, nbb, g, cw), lambda q, c, z: (z, c, 0, 0, 0, 0, q)),
        out_shape=jax.ShapeDtypeStruct((bz, ng, FFT_G, 2, nbb, g, ch), F32),
        compiler_params=_cparams(("parallel", "parallel", "parallel")),
        name="fft_stage2_conv",
    )(a5, a5, hspec, tabs["f2"], tabs["f2i"], tabs["twc_c"], tabs["tws_c"])


def _fftk3_body(*refs, gate):
    if gate:
        br_ref, bi_ref, mr_ref, mi_ref, x0_ref, z_ref, bias_ref, o_ref = refs
    else:
        br_ref, bi_ref, mr_ref, mi_ref, o_ref = refs
    ng, gc, g, cw = br_ref.shape
    vr = br_ref[...].reshape(ng * gc * g, cw).astype(BF16)
    vi = bi_ref[...].reshape(ng * gc * g, cw).astype(BF16)
    y = jnp.dot(mr_ref[...], vr, preferred_element_type=F32) + jnp.dot(mi_ref[...], vi, preferred_element_type=F32)
    a_cnt = o_ref.shape[0]
    if gate:
        x0 = x0_ref[...].reshape(a_cnt * g, cw)
        z = z_ref[...].reshape(a_cnt * g, cw)
        y = x0 * (y + bias_ref[...] * z)
    o_ref[...] = y.reshape(a_cnt, g, cw)


def _fftk_stage3(b7, tabs, na, a_cnt, gate_args=None):
    bz, ng, gc, _, nbb, g, ch = b7.shape
    cw = min(ch, FFT_CW)
    mr, mi = tabs["m3r"].astype(BF16), tabs["m3i"].astype(BF16)
    row_spec = pl.BlockSpec((None, a_cnt, None, g, cw), lambda z, q, b: (z, 0, b, 0, q))
    in_specs = [
        pl.BlockSpec((None, ng, gc, None, None, g, cw), lambda z, q, b: (z, 0, 0, 0, b, 0, q)),
        pl.BlockSpec((None, ng, gc, None, None, g, cw), lambda z, q, b: (z, 0, 0, 1, b, 0, q)),
        pl.BlockSpec(mr.shape, lambda z, q, b: (0, 0)),
        pl.BlockSpec(mi.shape, lambda z, q, b: (0, 0)),
    ]
    args = [b7, b7, mr, mi]
    if gate_args is not None:
        in_specs += [row_spec, row_spec, pl.BlockSpec((1, cw), lambda z, q, b: (0, q))]
        args += list(gate_args)
    return pl.pallas_call(
        functools.partial(_fftk3_body, gate=gate_args is not None),
        grid=(bz, ch // cw, nbb),
        in_specs=in_specs,
        out_specs=row_spec,
        out_shape=jax.ShapeDtypeStruct((bz, a_cnt, nbb, g, ch), F32),
        compiler_params=_cparams(("parallel", "parallel", "parallel")),
        name="fft_stage3",
    )(*args)


def _hy_gate_body(x0_ref, y_ref, z_ref, b_ref, o_ref):
    o_ref[...] = x0_ref[...] * (y_ref[...] + b_ref[...] * z_ref[...])


def _hyena(p, bsz, seq, short_w, short_b, w1, b1, w2, b2, w3, hy_bias, tt):
    width = w3.shape[1]
    x0, z = _hyena_prep(p, short_w, short_b, bsz, seq, tt)
    hu, hs = _hyena_filter(seq, w1, b1, w2, b2, w3)
    na = max(2 * seq // FFT_B, 16)
    ka = max(seq // FFT_B, 16)
    pad = ka * FFT_B - seq
    a_lo, a_cnt = (0, na) if pad else (seq // 2 // FFT_B, ka)
    tabs = _dft_tables_k(na, ka, a_lo, a_cnt)
    nbb = FFT_B // FFT_G

    def rows5(a, lead):
        if pad:
            a = jnp.pad(a, ((0, 0), (0, pad), (0, 0)))
        return a.reshape(lead, ka, nbb, FFT_G, width)

    ones = jnp.ones((1, width), F32)
    bias = hy_bias.reshape(1, -1)
    hspec = _fftk_stage2_spectrum(_fftk_stage1(rows5(hu[None], 1), 1.0 / hs, tabs, na), tabs, na)
    a5 = _fftk_stage1(rows5(z, bsz), ones, tabs, na)
    b7 = _fftk_stage2_conv(a5, hspec, tabs, na)
    m = bsz * seq
    if not pad:
        out = _fftk_stage3(b7, tabs, na, a_cnt, (rows5(x0, bsz), rows5(z, bsz), bias))
        return out.reshape(m, width)
    y = _fftk_stage3(b7, tabs, na, a_cnt).reshape(bsz, na * FFT_B, width)[:, seq // 2: seq // 2 + seq]
    return _rows_call(_hy_gate_body, [x0.reshape(m, width), y.reshape(m, width), z.reshape(m, width)],
                      [bias], [], [((m, width), F32)], tt, seq, "hyena_gate")[0]


def _s5_tables(lam_re, lam_im, log_dt, b_re, b_im, c_re, c_im, nsteps):
    t_len, hdim = S5_CHUNK, S5_GROUP
    lam = lax.complex(jnp.minimum(lam_re.astype(F32), -1e-4), lam_im.astype(F32))
    dt = jnp.exp(log_dt.astype(F32))[..., None]
    lam_dt = lam * dt
    lam_bar = jnp.exp(lam_dt)
    b_bar = ((lam_bar - 1.0) / lam)[..., None] * lax.complex(b_re.astype(F32), b_im.astype(F32))
    cm = lax.complex(c_re.astype(F32), c_im.astype(F32))
    ks = jnp.arange(t_len + 1, dtype=F32)
    pw = jnp.exp(ks[:, None, None, None] * lam_dt[None])
    g = lam.shape[1]
    tabs = {}
    pin = [pw[:t_len, 0][::-1], pw[:t_len, 1]]
    pout = [pw[1:, 0], pw[1:, 1][::-1]]
    for d in range(2):
        win = pin[d][:, :, :, None] * b_bar[d][None]
        win = jnp.transpose(win, (1, 0, 3, 2)).reshape(g, t_len * hdim, -1)
        tabs[f"win{d}"] = jnp.concatenate([jnp.real(win), jnp.imag(win)], axis=-1)
        wout = pout[d][:, :, None, :] * cm[d][None]
        wout = jnp.transpose(wout, (1, 3, 0, 2)).reshape(g, -1, t_len * hdim)
        tabs[f"wout{d}"] = jnp.concatenate([jnp.real(wout), -jnp.imag(wout)], axis=1)
        mu = jnp.exp((t_len * 2.0 ** jnp.arange(nsteps, dtype=F32))[:, None, None] * lam_dt[d][None])
        mr, mi = jnp.real(mu), jnp.imag(mu)
        tabs[f"m1{d}"] = jnp.concatenate([mr, mr], axis=-1)[:, :, None, :]
        tabs[f"m2{d}"] = jnp.concatenate([-mi, mi], axis=-1)[:, :, None, :]
        mu1 = jnp.exp(t_len * lam_dt[d])
        tabs[f"mu1{d}"] = jnp.concatenate([jnp.real(mu1), jnp.real(mu1)], axis=-1)[:, None, :]
        tabs[f"mu2{d}"] = jnp.concatenate([-jnp.imag(mu1), jnp.imag(mu1)], axis=-1)[:, None, :]
    kern = [jnp.real(jnp.einsum("ghp,tgp,gpk->tghk", cm[d], pw[:t_len, d], b_bar[d], precision=HI)) for d in range(2)]
    s_idx = jnp.arange(t_len)[:, None]
    t_idx = jnp.arange(t_len)[None, :]
    fwd = jnp.where((t_idx >= s_idx)[:, :, None, None, None], kern[0][jnp.maximum(t_idx - s_idx, 0)], 0.0)
    bwd = jnp.where((s_idx >= t_idx)[:, :, None, None, None], kern[1][jnp.maximum(s_idx - t_idx, 0)], 0.0)
    d0 = fwd + bwd
    tabs["d0"] = jnp.transpose(d0, (2, 0, 4, 1, 3)).reshape(g, t_len * hdim, t_len * hdim)
    return _s5_pair_tables(tabs)


def _s5_pair_tables(tabs):
    half = S5_STATE

    def block_diag(a):
        g, r, c = a.shape
        a = a.reshape(g // 2, 2, r, c)
        z = jnp.zeros_like(a[:, 0])
        return jnp.concatenate([jnp.concatenate([a[:, 0], z], -1), jnp.concatenate([z, a[:, 1]], -1)], axis=-2)

    def lanes(a):
        return jnp.concatenate([a[..., 0::2, :, :], a[..., 1::2, :, :]], axis=-1)

    out = {"d0": block_diag(tabs["d0"])}
    for d in range(2):
        win, wout = tabs[f"win{d}"], tabs[f"wout{d}"]
        out[f"winr{d}"], out[f"wini{d}"] = block_diag(win[..., :half]), block_diag(win[..., half:])
        out[f"woutr{d}"], out[f"wouti{d}"] = block_diag(wout[:, :half, :]), block_diag(wout[:, half:, :])
        out[f"mr{d}"], out[f"mi{d}"] = lanes(tabs[f"m1{d}"][..., :half]), lanes(tabs[f"m2{d}"][..., half:])
        out[f"mur{d}"], out[f"mui{d}"] = lanes(tabs[f"mu1{d}"][..., :half]), lanes(tabs[f"mu2{d}"][..., half:])
    return out


S5_GPB = LANES // S5_GROUP


def _s5_perm_table():
    p = np.zeros((S5_CHUNK, S5_GPB, LANES, LANES), np.float32)
    for t in range(S5_CHUNK):
        for g in range(S5_GPB):
            for h in range(S5_GROUP):
                p[t, g, S5_GROUP * g + h, S5_GROUP * t + h] = 1.0
    return jnp.asarray(p)


S5_TABLES = ("d0", "winr0", "wini0", "winr1", "wini1", "woutr0", "wouti0", "woutr1", "wouti1")
S5_STEP_TABLES = ("mr0", "mi0", "mr1", "mi1")
S5_LAST_TABLES = ("mur0", "mui0", "mur1", "mui1")


def _s5_body(u_ref, perm_ref, permt_ref, *refs, nc, nsteps):
    nt, ns = len(S5_TABLES), len(S5_STEP_TABLES)
    tab = dict(zip(S5_TABLES + S5_STEP_TABLES + S5_LAST_TABLES, refs))
    s0_ref, s1_ref, y_ref, f0_ref, f1_ref = refs[nt + ns + len(S5_LAST_TABLES):]
    t_len = S5_CHUNK
    j = lax.broadcasted_iota(jnp.int32, (nc, 1), 0)

    def mm(a, name, p):
        return jnp.dot(a.astype(BF16), tab[name][p].astype(BF16), preferred_element_type=F32)

    def scan(e_re, e_im, init_ref, p, d):
        first = (j == 0) if d == 0 else (j == nc - 1)
        one = 1 if d == 0 else nc - 1
        s_re = jnp.where(first, init_ref[p, 0:1, :], pltpu.roll(e_re, one, axis=0))
        s_im = jnp.where(first, init_ref[p, 1:2, :], pltpu.roll(e_im, one, axis=0))
        for k in range(nsteps):
            step = 2 ** k
            keep = (j >= step) if d == 0 else (j < nc - step)
            shift = step if d == 0 else nc - step
            sh_re = jnp.where(keep, pltpu.roll(s_re, shift, axis=0), 0.0)
            sh_im = jnp.where(keep, pltpu.roll(s_im, shift, axis=0), 0.0)
            mr, mi = tab[f"mr{d}"][k, p], tab[f"mi{d}"][k, p]
            s_re, s_im = s_re + mr * sh_re - mi * sh_im, s_im + mr * sh_im + mi * sh_re
        return s_re, s_im

    xs = [u_ref[pl.ds(t, nc, stride=t_len), :].astype(BF16) for t in range(t_len)]
    ys = []
    for p in range(S5_GPB // 2):
        us = []
        for g in (2 * p, 2 * p + 1):
            u = None
            for t in range(t_len):
                part = jnp.dot(xs[t], perm_ref[t, g].astype(BF16), preferred_element_type=F32)
                u = part if u is None else u + part
            us.append(u.astype(BF16))
        u = jnp.concatenate(us, axis=1)
        y = mm(u, "d0", p)
        for d, (init_ref, fin_ref) in enumerate(((s0_ref, f0_ref), (s1_ref, f1_ref))):
            e_re, e_im = mm(u, f"winr{d}", p), mm(u, f"wini{d}", p)
            s_re, s_im = scan(e_re, e_im, init_ref, p, d)
            y = y + mm(s_re, f"woutr{d}", p) + mm(s_im, f"wouti{d}", p)
            mur, mui = tab[f"mur{d}"][p], tab[f"mui{d}"][p]
            last = slice(nc - 1, nc) if d == 0 else slice(0, 1)
            fin_ref[p, 0:1, :] = (mur * s_re - mui * s_im + e_re)[last, :]
            fin_ref[p, 1:2, :] = (mur * s_im + mui * s_re + e_im)[last, :]
        for y_g in (y[:, :LANES], y[:, LANES:]):
            y_hi = y_g.astype(BF16)
            ys.append((y_hi, (y_g - y_hi.astype(F32)).astype(BF16)))
    for t in range(t_len):
        out = None
        for g in range(S5_GPB):
            pt = permt_ref[t, g].astype(BF16)
            part = jnp.dot(ys[g][0], pt, preferred_element_type=F32) + jnp.dot(ys[g][1], pt, preferred_element_type=F32)
            out = part if out is None else out + part
        y_ref[pl.ds(t, nc, stride=t_len), :] = out


def _s5_core(u2, bsz, seq, tabs, init0, init1):
    m, width = u2.shape
    pairs = width // S5_GROUP // 2
    ppb = S5_GPB // 2
    nq = width // LANES
    t_len = S5_CHUNK
    nc = seq // t_len
    nsteps = max(1, math.ceil(math.log2(nc)))
    assert nsteps <= tabs["mr0"].shape[0] and t_len * S5_GROUP == LANES
    perm = _s5_perm_table()
    permt = jnp.swapaxes(perm, 2, 3)

    def whole(a):
        return pl.BlockSpec(a.shape, lambda q, b, nd=a.ndim: (0,) * nd)

    def per_q(a):
        return pl.BlockSpec((ppb,) + a.shape[1:], lambda q, b, nd=a.ndim: (q,) + (0,) * (nd - 1))

    def per_q_steps(a):
        return pl.BlockSpec((nsteps, ppb) + a.shape[2:], lambda q, b: (0, q, 0, 0))

    p2 = 2 * S5_STATE
    state_spec = pl.BlockSpec((None, ppb, 2, p2), lambda q, b: (b, q, 0, 0))
    seq_spec = pl.BlockSpec((seq, LANES), lambda q, b: (b, q))
    mats = [tabs[k] for k in S5_TABLES]
    steps = [tabs[k][:nsteps] for k in S5_STEP_TABLES]
    lasts = [tabs[k] for k in S5_LAST_TABLES]
    in_specs = [seq_spec, whole(perm), whole(permt)] + [per_q(a) for a in mats] + [per_q_steps(a) for a in steps] \
        + [per_q(a) for a in lasts] + [state_spec] * 2
    state_shape = jax.ShapeDtypeStruct((bsz, pairs, 2, p2), F32)
    y, f0, f1 = pl.pallas_call(
        functools.partial(_s5_body, nc=nc, nsteps=nsteps),
        grid=(nq, bsz),
        in_specs=in_specs,
        out_specs=[seq_spec, state_spec, state_spec],
        out_shape=[jax.ShapeDtypeStruct((m, width), F32), state_shape, state_shape],
        compiler_params=_cparams(("parallel", "parallel")),
        name="s5_scan",
    )(u2, perm, permt, *mats, *steps, *lasts, init0, init1)
    return y, f0, f1


def _s5_glu_body(u_ref, y_ref, d_ref, w_ref, b_ref, o_ref):
    y = d_ref[...] * u_ref[...] + y_ref[...]
    g = jax.nn.gelu(y)
    o_ref[...] = g * jax.nn.sigmoid(jnp.dot(g.astype(BF16), w_ref[...], preferred_element_type=F32) + b_ref[...])


def _s5_glu(u2, y2, d_skip, glu_w, glu_b, tm, name):
    m, width = u2.shape
    return _rows_call(_s5_glu_body, [u2, y2], [d_skip.reshape(1, -1), glu_w.astype(BF16), glu_b.reshape(1, -1)], [],
                      [((m, width), F32)], tm, m, name)[0]


def _s5_mixer(u_lat, u_ctx, bsz, seq, seq_c, lam_re, lam_im, log_dt, b_re, b_im, c_re, c_im, d_skip, glu_w, glu_b,
              tm, tm_c):
    nsteps = max(1, math.ceil(math.log2(seq // S5_CHUNK)))
    tabs = _s5_tables(lam_re, lam_im, log_dt, b_re, b_im, c_re, c_im, nsteps)
    g = u_lat.shape[1] // S5_GROUP
    zero = jnp.zeros((bsz, g // 2, 2, 2 * S5_STATE), F32)
    y_ctx, f0, f1 = _s5_core(u_ctx, bsz, seq_c, tabs, zero, zero)
    y_lat, _, _ = _s5_core(u_lat, bsz, seq, tabs, f0, f1)
    out_lat = _s5_glu(u_lat, y_lat, d_skip, glu_w, glu_b, tm, "s5_glu")
    out_ctx = _s5_glu(u_ctx, y_ctx, d_skip, glu_w, glu_b, tm_c, "s5_glu_ctx")
    return out_lat, out_ctx


def _rope_tables_1d(seq, hd, heads):
    half = hd // 2
    inv = ROPE_BASE ** (-jnp.arange(half, dtype=F32) / half)
    ang = jnp.arange(seq, dtype=F32)[:, None] * inv[None]
    cos, sin = jnp.cos(ang), jnp.sin(ang)
    return jnp.tile(jnp.concatenate([cos, cos], -1), (1, heads)), jnp.tile(jnp.concatenate([-sin, sin], -1), (1, heads))


def _rope_tables_2d(seq, hd, heads):
    q = hd // 4
    inv = ROPE_BASE ** (-jnp.arange(q, dtype=F32) / q)
    n_rows = seq // GRID_W
    row = jnp.repeat(jnp.arange(n_rows, dtype=F32), GRID_W)
    col = jnp.tile(jnp.arange(GRID_W, dtype=F32), n_rows)
    ar, ac = row[:, None] * inv[None], col[:, None] * inv[None]
    cos = jnp.concatenate([jnp.cos(ar), jnp.cos(ar), jnp.cos(ac), jnp.cos(ac)], -1)
    sin = jnp.concatenate([-jnp.sin(ar), jnp.sin(ar), -jnp.sin(ac), jnp.sin(ac)], -1)
    return jnp.tile(cos, (1, heads)), jnp.tile(sin, (1, heads))


def _kv_update(k, kdec, v):
    kd = (k * kdec).astype(BF16)
    return lax.dot_general(kd, v.astype(BF16), (((0,), (0,)), ((), ())), preferred_element_type=F32)


def _ret_bwd_body(k_ref, v_ref, kdec_ref, gc_ref, init_ref, sprev_ref, fin_ref, s_ref):
    @pl.when(pl.program_id(1) == 0)
    def _():
        s_ref[...] = init_ref[...]

    for h in range(k_ref.shape[0]):
        s = s_ref[h]
        sprev_ref[h] = s
        s = gc_ref[h] * s + _kv_update(k_ref[h], kdec_ref[h], v_ref[h])
        s_ref[h] = s
        fin_ref[h] = s


def _ret_main_body(q_ref, k_ref, v_ref, g_ref, mask_ref, qdf_ref, kdf_ref, qdb_ref, gc_ref, init_ref, sb_ref,
                   o_ref, fin_ref, s_ref):
    @pl.when(pl.program_id(1) == 0)
    def _():
        s_ref[...] = init_ref[...]

    for h in range(q_ref.shape[0]):
        q, k, v = q_ref[h], k_ref[h], v_ref[h]
        vb = v.astype(BF16)
        sc = lax.dot_general(q.astype(BF16), k.astype(BF16), (((1,), (1,)), ((), ())), preferred_element_type=F32)
        o = jnp.dot((sc * mask_ref[h]).astype(BF16), vb, preferred_element_type=F32)
        s = s_ref[h]
        o = o + jnp.dot((q * qdf_ref[h]).astype(BF16), s.astype(BF16), preferred_element_type=F32)
        o = o + jnp.dot((q * qdb_ref[h]).astype(BF16), sb_ref[h].astype(BF16), preferred_element_type=F32)
        o = o * lax.rsqrt(jnp.mean(o * o, axis=-1, keepdims=True) + EPS)
        o_ref[h] = o * jax.nn.silu(g_ref[h])
        s = gc_ref[h] * s + _kv_update(k, kdf_ref[h], v)
        s_ref[h] = s
        fin_ref[h] = s


def _ret_tables(decay_logit):
    cl = RET_CHUNK
    log_g = jax.nn.log_sigmoid(decay_logit.astype(F32))
    idx = jnp.arange(cl, dtype=F32)
    diff = idx[:, None] - idx[None, :]
    mf = jnp.where(diff[None] >= 0, jnp.exp(jnp.maximum(diff, 0.0)[None] * log_g[0][:, None, None]), 0.0)
    mb = jnp.where(diff[None] <= 0, jnp.exp(jnp.maximum(-diff, 0.0)[None] * log_g[1][:, None, None]), 0.0)

    def col(e, d):
        return jnp.exp(e[None, :] * log_g[d][:, None])[:, :, None]

    return dict(mask=mf + mb, qdf=col(idx + 1.0, 0), kdf=col(cl - 1.0 - idx, 0), qdb=col(cl - idx, 1), kdb=col(idx, 1),
                gcf=jnp.exp(cl * log_g[0])[:, None, None], gcb=jnp.exp(cl * log_g[1])[:, None, None])


def _retention(q, k, v, g, bsz, seq, tabs, init_f, init_b):
    heads, _, dk = q.shape
    dv = v.shape[2]
    cl = RET_CHUNK
    n = seq // cl

    def whole(a):
        return pl.BlockSpec(a.shape, lambda b, i, nd=a.ndim: (0,) * nd)

    state_spec = pl.BlockSpec((None, heads, dk, dv), lambda b, i: (b, 0, 0, 0))
    state_shape = jax.ShapeDtypeStruct((bsz, heads, dk, dv), F32)
    rev = lambda b, i: (0, b * n + (n - 1 - i), 0)
    sprev_b, fin_b = pl.pallas_call(
        _ret_bwd_body,
        grid=(bsz, n),
        in_specs=[pl.BlockSpec((heads, cl, dk), rev), pl.BlockSpec((heads, cl, dv), rev),
                  whole(tabs["kdb"]), whole(tabs["gcb"]), state_spec],
        out_specs=[pl.BlockSpec((None, None, heads, dk, dv), lambda b, i: (b, n - 1 - i, 0, 0, 0)), state_spec],
        out_shape=[jax.ShapeDtypeStruct((bsz, n, heads, dk, dv), F32), state_shape],
        scratch_shapes=[pltpu.VMEM((heads, dk, dv), F32)],
        compiler_params=_cparams(("parallel", "arbitrary")),
        name="retention_backward_states",
    )(k, v, tabs["kdb"], tabs["gcb"], init_b)
    fwd = lambda b, i: (0, b * n + i, 0)
    o, fin_f = pl.pallas_call(
        _ret_main_body,
        grid=(bsz, n),
        in_specs=[pl.BlockSpec((heads, cl, dk), fwd), pl.BlockSpec((heads, cl, dk), fwd),
                  pl.BlockSpec((heads, cl, dv), fwd), pl.BlockSpec((heads, cl, dv), fwd),
                  whole(tabs["mask"]), whole(tabs["qdf"]), whole(tabs["kdf"]), whole(tabs["qdb"]), whole(tabs["gcf"]),
                  state_spec, pl.BlockSpec((None, None, heads, dk, dv), lambda b, i: (b, i, 0, 0, 0))],
        out_specs=[pl.BlockSpec((heads, cl, dv), fwd), state_spec],
        out_shape=[jax.ShapeDtypeStruct((heads, bsz * seq, dv), F32), state_shape],
        scratch_shapes=[pltpu.VMEM((heads, dk, dv), F32)],
        compiler_params=_cparams(("parallel", "arbitrary")),
        name="retention",
    )(q, k, v, g, tabs["mask"], tabs["qdf"], tabs["kdf"], tabs["qdb"], tabs["gcf"], init_f, sprev_b)
    return o, fin_f, fin_b


def _swa_body(*refs, local, seq, scale):
    if local:
        q_ref, kp_ref, kc_ref, kn_ref, vp_ref, vc_ref, vn_ref, kx_ref, vx_ref, sink_ref, o_ref = refs
    else:
        q_ref, kx_ref, vx_ref, sink_ref, o_ref = refs
    i = pl.program_id(1)
    bk = q_ref.shape[1]
    nkv = kx_ref.shape[0]
    grp = q_ref.shape[0] // nkv
    nt = (((1,), (1,)), ((), ()))
    hd = q_ref.shape[2]
    rows = grp * bk
    if local:
        row = lax.broadcasted_iota(jnp.int32, (rows, 3 * bk), 0) % bk
        col = lax.broadcasted_iota(jnp.int32, (rows, 3 * bk), 1)
        key_pos = (i - 1) * bk + col
        valid = (jnp.abs(col - (row + bk)) <= SWA_WINDOW) & (key_pos >= 0) & (key_pos < seq)
    for kv in range(nkv):
        kx = kx_ref[kv].astype(BF16)
        vx = vx_ref[kv].astype(BF16)
        q = q_ref[kv * grp:(kv + 1) * grp].reshape(rows, hd).astype(BF16)
        sink = jnp.concatenate([jnp.broadcast_to(sink_ref[kv * grp + gi], (bk, 1)) for gi in range(grp)], axis=0)
        s_x = lax.dot_general(q, kx, nt, preferred_element_type=F32) * scale
        m = jnp.maximum(jnp.max(s_x, axis=-1, keepdims=True), sink)
        if local:
            kl = jnp.concatenate([kp_ref[kv], kc_ref[kv], kn_ref[kv]], axis=0).astype(BF16)
            vl = jnp.concatenate([vp_ref[kv], vc_ref[kv], vn_ref[kv]], axis=0).astype(BF16)
            s_l = lax.dot_general(q, kl, nt, preferred_element_type=F32) * scale
            s_l = jnp.where(valid, s_l, NEG_INF)
            m = jnp.maximum(m, jnp.max(s_l, axis=-1, keepdims=True))
        p_x = jnp.exp(s_x - m)
        den = jnp.sum(p_x, axis=-1, keepdims=True) + jnp.exp(sink - m)
        o = jnp.dot(p_x.astype(BF16), vx, preferred_element_type=F32)
        if local:
            p_l = jnp.exp(s_l - m)
            den = den + jnp.sum(p_l, axis=-1, keepdims=True)
            o = o + jnp.dot(p_l.astype(BF16), vl, preferred_element_type=F32)
        o_ref[kv * grp:(kv + 1) * grp] = (o / den).reshape(grp, bk, hd)


def _swa(q, k, v, kx, vx, sink, bsz, seq, seq_c, local):
    hq, m, hd = q.shape
    hkv = kx.shape[0]
    bk = SWA_BLOCK
    nb = seq // bk
    scale = hd ** -0.5
    cur = lambda b, i: (0, b * nb + i, 0)
    prv = lambda b, i: (0, b * nb + jnp.maximum(i - 1, 0), 0)
    nxt = lambda b, i: (0, b * nb + jnp.minimum(i + 1, nb - 1), 0)
    ctx_spec = pl.BlockSpec((hkv, seq_c, hd), lambda b, i: (0, b, 0))
    sink_spec = pl.BlockSpec((hq, 1, 1), lambda b, i: (0, 0, 0))
    kvb = lambda f: pl.BlockSpec((hkv, bk, hd), f)
    in_specs = [pl.BlockSpec((hq, bk, hd), cur)]
    args = [q]
    if local:
        in_specs += [kvb(prv), kvb(cur), kvb(nxt), kvb(prv), kvb(cur), kvb(nxt)]
        args += [k, k, k, v, v, v]
    in_specs += [ctx_spec, ctx_spec, sink_spec]
    args += [kx, vx, sink.astype(F32).reshape(hq, 1, 1)]
    return pl.pallas_call(
        functools.partial(_swa_body, local=local, seq=seq, scale=scale),
        grid=(bsz, nb),
        in_specs=in_specs,
        out_specs=pl.BlockSpec((hq, bk, hd), cur),
        out_shape=jax.ShapeDtypeStruct((hq, m, hd), F32),
        compiler_params=_cparams(("parallel", "parallel")),
        name="swa" if local else "swa_ctx",
    )(*args)


def _mixer_cd_in(x2, g, sc, sh, w_in, d, seq, tm, rows_per_batch, name, rope):
    hd = d // 16
    qk, vw = RET_HEADS * hd, 2 * RET_HEADS * hd
    qw, kw = SWA_Q_HEADS * hd, SWA_KV_HEADS * hd
    if rope:
        t2 = _rope_tables_2d(seq, hd, 1)
        tables = [_rope_tables_1d(seq, hd, RET_HEADS), [jnp.tile(t, (1, SWA_Q_HEADS)) for t in t2],
                  [jnp.tile(t, (1, SWA_KV_HEADS)) for t in t2]]
        post = [(0, hd // 2, hd ** -0.5), (0, hd // 2, 1.0), (1, hd // 4, 1.0), (2, hd // 4, 1.0)]
    else:
        tables = []
        post = [(None, 0, hd ** -0.5), None, None, None]
    splits = [(qk, RET_HEADS, post[0]), (qk, RET_HEADS, post[1]), (vw, RET_HEADS, None), (vw, RET_HEADS, None),
              (qw, SWA_Q_HEADS, post[2]), (kw, SWA_KV_HEADS, post[3]), (kw, SWA_KV_HEADS, None)]
    names = ("rq", "rk", "rv", "rg", "sq", "sk", "sv")
    return dict(zip(names, _norm_mod_matmul(x2, g, sc, sh, w_in, splits, tm, rows_per_batch, name, tables)))


def _mixer_cd_core(lat, cx, bsz, seq, seq_c, decay_logit, sink, with_ctx):
    dk, dv = lat["rq"].shape[2], lat["rv"].shape[2]
    tabs = _ret_tables(decay_logit)
    zero = jnp.zeros((bsz, RET_HEADS, dk, dv), F32)
    ro_c, fin_f, fin_b = _retention(cx["rq"], cx["rk"], cx["rv"], cx["rg"], bsz, seq_c, tabs, zero, zero)
    ro_l, _, _ = _retention(lat["rq"], lat["rk"], lat["rv"], lat["rg"], bsz, seq, tabs, fin_f, fin_b)
    so_l = _swa(lat["sq"], lat["sk"], lat["sv"], cx["sk"], cx["sv"], sink, bsz, seq, seq_c, True)
    so_c = None
    if with_ctx:
        so_c = _swa(cx["sq"], None, None, cx["sk"], cx["sv"], sink, bsz, seq_c, seq_c, False)
    return ro_l, so_l, (ro_c if with_ctx else None), so_c


def _store_tile_rows(ref, val, base=0):
    r, d = val.shape
    sub = d // LANES
    for c in range(sub):
        ref[pl.ds(base + c, r, stride=sub), :] = val[:, c * LANES:(c + 1) * LANES]


def _load_tile_rows(ref, r, sub, base=0):
    return jnp.concatenate([ref[pl.ds(base + c, r, stride=sub), :] for c in range(sub)], axis=1)


def _router_body(x_ref, g_ref, rw_ref, rb_ref, tril_ref, sc_ref, sh_ref, hx_ref, ti_ref, gt_ref, pos_ref, cnt_ref):
    @pl.when(pl.program_id(0) == 0)
    def _():
        cnt_ref[...] = jnp.zeros_like(cnt_ref)

    hx = _norm_mod(x_ref[...], g_ref[...], sc_ref[...], sh_ref[...])
    _store_tile_rows(hx_ref, hx)
    logits = jnp.dot(hx, rw_ref[...], precision=HI, preferred_element_type=F32) + rb_ref[...]
    lane = lax.broadcasted_iota(jnp.int32, logits.shape, 1).astype(F32)
    rem = logits
    vals, hots = [], []
    for k in range(TOP_K):
        m = jnp.max(rem, axis=-1, keepdims=True)
        idx = jnp.min(jnp.where(rem == m, lane, float(N_EXPERTS)), axis=-1, keepdims=True)
        hot = lane == idx
        rem = jnp.where(hot, NEG_INF, rem)
        vals.append(m)
        hots.append(hot.astype(F32))
        ti_ref[:, k:k + 1] = idx.astype(jnp.int32)
    exps = [jnp.exp(v - vals[0]) for v in vals]
    den = exps[0] + exps[1] + exps[2] + exps[3]
    for k in range(TOP_K):
        gt_ref[:, k:k + 1] = exps[k] / den
    sel = hots[0] + hots[1] + hots[2] + hots[3]
    before = jnp.dot(tril_ref[...], sel.astype(BF16), preferred_element_type=F32) + cnt_ref[...]
    for k in range(TOP_K):
        pos_ref[:, k:k + 1] = jnp.sum(hots[k] * before, axis=-1, keepdims=True).astype(jnp.int32)
    cnt_ref[...] += jnp.sum(sel, axis=0, keepdims=True)


def _moe_route(x2, g, sc, sh, router_w, router_b, tr, rows_per_batch):
    n, d = x2.shape
    tiles_per_batch = rows_per_batch // tr
    tril = jnp.asarray(np.tril(np.ones((tr, tr), np.float32), -1)).astype(BF16)
    whole = lambda a: pl.BlockSpec(a.shape, lambda i, nd=a.ndim: (0,) * nd)
    bat = pl.BlockSpec((None, 1, d), lambda i: (i // tiles_per_batch, 0, 0))
    g2, rb2 = g.reshape(1, -1), router_b.reshape(1, -1)
    small = lambda dt: jax.ShapeDtypeStruct((n, TOP_K), dt)
    small_spec = pl.BlockSpec((tr, TOP_K), lambda i: (i, 0))
    return pl.pallas_call(
        _router_body,
        grid=(n // tr,),
        in_specs=[pl.BlockSpec((tr, d), lambda i: (i, 0)), whole(g2), whole(router_w), whole(rb2), whole(tril), bat, bat],
        out_specs=[pl.BlockSpec((tr * (d // LANES), LANES), lambda i: (i, 0)), small_spec, small_spec, small_spec,
                   pl.BlockSpec((1, N_EXPERTS), lambda i: (0, 0))],
        out_shape=[jax.ShapeDtypeStruct((n * (d // LANES), LANES), F32), small(jnp.int32), small(F32), small(jnp.int32),
                   jax.ShapeDtypeStruct((1, N_EXPERTS), F32)],
        compiler_params=_cparams(("arbitrary",)),
        name="moe_router",
    )(x2, g2, router_w, rb2, tril, sc, sh)


def _tile_row_copy(src, s_off, dst, d_off, sem, sub):
    return pltpu.make_async_copy(src.at[pl.ds(pl.multiple_of(s_off, sub), sub)],
                                 dst.at[pl.ds(pl.multiple_of(d_off, sub), sub)], sem)


def _dispatch_body(dest_ref, hx_ref, xs_in_ref, xs_ref, sem, *, td, sub):
    del xs_in_ref

    def issue(n, carry):
        for k in range(TOP_K):
            _tile_row_copy(hx_ref, n * sub, xs_ref, dest_ref[n * TOP_K + k], sem, sub).start(priority=k % 2)
        return carry

    def drain(n, carry):
        for k in range(TOP_K):
            _tile_row_copy(hx_ref, 0, xs_ref, 0, sem, sub).wait()
        return carry

    lax.fori_loop(0, td, issue, 0, unroll=DMA_UNROLL)
    lax.fori_loop(0, td, drain, 0, unroll=DMA_UNROLL)


def _moe_dispatch(hx, dest_off, n_slots, td, sub, xs_init=None):
    lanes = hx.shape[1]
    n = hx.shape[0] // sub
    zeros = jnp.zeros((n_slots * sub, lanes), F32) if xs_init is None else xs_init
    return pl.pallas_call(
        functools.partial(_dispatch_body, td=td, sub=sub),
        grid=(n // td,),
        in_specs=[pl.BlockSpec((td * TOP_K,), lambda i: (i,), memory_space=pltpu.SMEM),
                  pl.BlockSpec((td * sub, lanes), lambda i: (i, 0)), pl.BlockSpec(memory_space=pl.ANY)],
        out_specs=pl.BlockSpec(memory_space=pl.ANY),
        out_shape=jax.ShapeDtypeStruct((n_slots * sub, lanes), F32),
        scratch_shapes=[pltpu.SemaphoreType.DMA(())],
        input_output_aliases={2: 0},
        compiler_params=pltpu.CompilerParams(dimension_semantics=("arbitrary",), has_side_effects=True,
                                             vmem_limit_bytes=VMEM_LIMIT),
        name="moe_dispatch",
    )(dest_off, hx, zeros)


def _ffn_body(be_ref, nu_ref, x_ref, wgu_ref, bgu_ref, wdn_ref, bdn_ref, o_ref, wgu_bf, wdn_bf, *, tm):
    j = pl.program_id(0)
    e = be_ref[j]
    prev = be_ref[jnp.maximum(j - 1, 0)]

    @pl.when((j == 0) | (e != prev))
    def _():
        wgu_bf[...] = wgu_ref[...].astype(BF16)
        wdn_bf[...] = wdn_ref[...].astype(BF16)

    @pl.when(j < nu_ref[0])
    def _():
        f = wdn_ref.shape[0]
        x = _load_tile_rows(x_ref, tm, wgu_ref.shape[0] // LANES).astype(BF16)
        gu = jnp.dot(x, wgu_bf[...], preferred_element_type=F32) + bgu_ref[...]
        gate = jnp.minimum(gu[:, :f], SWIGLU_LIMIT)
        up = jnp.clip(gu[:, f:], -SWIGLU_LIMIT, SWIGLU_LIMIT)
        act = gate * jax.nn.sigmoid(SWIGLU_ALPHA * gate) * (up + 1.0)
        _store_tile_rows(o_ref, jnp.dot(act.astype(BF16), wdn_bf[...], preferred_element_type=F32) + bdn_ref[...])

    @pl.when(j >= nu_ref[0])
    def _():
        o_ref[...] = jnp.zeros_like(o_ref)


def _moe_ffn(xs, block_exp, n_used, layer, w_gu, b_gu, w_dn, b_dn, tm):
    depth, n_exp, d, f2 = w_gu.shape
    f = w_dn.shape[2]
    sub = d // LANES
    n_slots = xs.shape[0] // sub
    blk = lambda j, be, nu: (jnp.minimum(j, nu[0] - 1), 0)
    exp4 = lambda j, be, nu: (layer, be[j], 0, 0)
    grid_spec = pltpu.PrefetchScalarGridSpec(
        num_scalar_prefetch=2,
        grid=(n_slots // tm,),
        in_specs=[pl.BlockSpec((tm * sub, LANES), blk), pl.BlockSpec((None, None, d, f2), exp4),
                  pl.BlockSpec((None, None, 1, f2), exp4), pl.BlockSpec((None, None, f, d), exp4),
                  pl.BlockSpec((None, None, 1, d), exp4)],
        out_specs=pl.BlockSpec((tm * sub, LANES), lambda j, be, nu: (j, 0)),
        scratch_shapes=[pltpu.VMEM((d, f2), BF16), pltpu.VMEM((f, d), BF16)],
    )
    return pl.pallas_call(
        functools.partial(_ffn_body, tm=tm),
        grid_spec=grid_spec,
        out_shape=jax.ShapeDtypeStruct((n_slots * sub, LANES), F32),
        compiler_params=_cparams(("arbitrary",)),
        name="moe_ffn",
    )(block_exp, n_used, xs, w_gu, b_gu.reshape(depth, n_exp, 1, f2), w_dn, b_dn.reshape(depth, n_exp, 1, d))


def _combine_body(dest_ref, gt_ref, x_ref, ys_ref, g_ref, o_ref, buf, sem, *, tc, sub):
    def issue(n, carry):
        for k in range(TOP_K):
            _tile_row_copy(ys_ref, dest_ref[n * TOP_K + k], buf, (k * tc + n) * sub, sem, sub).start(priority=k % 2)
        return carry

    def drain(n, carry):
        for k in range(TOP_K):
            _tile_row_copy(ys_ref, 0, buf, 0, sem, sub).wait()
        return carry

    lax.fori_loop(0, tc, issue, 0, unroll=DMA_UNROLL)
    lax.fori_loop(0, tc, drain, 0, unroll=DMA_UNROLL)
    gates = [jnp.broadcast_to(gt_ref[:, k:k + 1], (tc, LANES)) for k in range(TOP_K)]
    for c in range(sub):
        cols = slice(c * LANES, (c + 1) * LANES)
        acc = gates[0] * buf[pl.ds(c, tc, stride=sub), :]
        for k in range(1, TOP_K):
            acc = acc + gates[k] * buf[pl.ds(k * tc * sub + c, tc, stride=sub), :]
        o_ref[:, cols] = x_ref[:, cols] + g_ref[:, cols] * acc


def _moe_combine(ys, dest_off, gates, x2, gate2, tc, rows_per_batch):
    n, d = x2.shape
    sub = d // LANES
    tiles_per_batch = rows_per_batch // tc
    return pl.pallas_call(
        functools.partial(_combine_body, tc=tc, sub=sub),
        grid=(n // tc,),
        in_specs=[pl.BlockSpec((tc * TOP_K,), lambda i: (i,), memory_space=pltpu.SMEM),
                  pl.BlockSpec((tc, TOP_K), lambda i: (i, 0)), pl.BlockSpec((tc, d), lambda i: (i, 0)),
                  pl.BlockSpec(memory_space=pl.ANY),
                  pl.BlockSpec((None, 1, d), lambda i: (i // tiles_per_batch, 0, 0))],
        out_specs=pl.BlockSpec((tc, d), lambda i: (i, 0)),
        out_shape=jax.ShapeDtypeStruct((n, d), F32),
        scratch_shapes=[pltpu.VMEM((TOP_K * tc * sub, LANES), F32), pltpu.SemaphoreType.DMA(())],
        compiler_params=_cparams(("arbitrary",)),
        name="moe_combine",
    )(dest_off, gates, x2, ys, gate2)


def _moe_slot_blocks(n_tok):
    return -(-n_tok * TOP_K // MOE_TM) + N_EXPERTS


def _moe_layer(streams, g, router_w, router_b, layer, w_gu, b_gu, w_dn, b_dn, n_blocks, slots=None):
    tm = MOE_TM
    sub = streams[0][0].shape[1] // LANES
    routed = [_moe_route(x2, g, sc, sh, router_w, router_b, tr, rpb) for x2, sc, sh, _, rpb, tr, _ in streams]
    counts = [r[4][0].astype(jnp.int32) for r in routed]
    total = functools.reduce(lambda a, b: a + b, counts)
    padded = (total + tm - 1) // tm * tm
    pad_end = jnp.cumsum(padded)
    assert n_blocks >= _moe_slot_blocks(sum(s[0].shape[0] for s in streams))
    n_used = (pad_end[-1] // tm).astype(jnp.int32)
    blk_ids = jnp.arange(n_blocks, dtype=jnp.int32)
    last_row = jnp.minimum(blk_ids, n_used - 1) * tm
    block_exp = jnp.sum((pad_end[None, :] <= last_row[:, None]).astype(jnp.int32), axis=1)
    block_exp = jnp.minimum(block_exp, N_EXPERTS - 1)
    start = pad_end - padded
    xs, dests = slots, []
    for (x2, _, _, _, _, tr, _), (hx, top_i, _, pos, _), cnt in zip(streams, routed, counts):
        dest_off = ((start[top_i] + pos) * sub).astype(jnp.int32).reshape(-1)
        xs = _moe_dispatch(hx, dest_off, n_blocks * tm, tr, sub, xs)
        dests.append(dest_off)
        start = start + cnt
    ys = _moe_ffn(xs, block_exp, n_used.reshape(1), layer, w_gu, b_gu, w_dn, b_dn, tm)
    outs = [_moe_combine(ys, dest_off, r[2], x2, gate2, tc, rpb)
            for (x2, _, _, gate2, rpb, _, tc), r, dest_off in zip(streams, routed, dests)]
    return outs, ys


def _mod_body(c_ref, w_ref, b_ref, o_ref):
    o_ref[...] = jnp.dot(jax.nn.silu(c_ref[...]), w_ref[...], precision=HI, preferred_element_type=F32) + b_ref[...]


def _modulation(cc, mod_w, mod_b):
    depth, d, d6 = mod_w.shape
    r = cc.shape[0]
    return pl.pallas_call(
        _mod_body,
        grid=(depth, d6 // d),
        in_specs=[pl.BlockSpec((r, d), lambda l, j: (0, 0)), pl.BlockSpec((None, d, d), lambda l, j: (l, 0, j)),
                  pl.BlockSpec((None, 1, d), lambda l, j: (l, 0, j))],
        out_specs=pl.BlockSpec((None, r, d), lambda l, j: (l, 0, j)),
        out_shape=jax.ShapeDtypeStruct((depth, r, d6), F32),
        compiler_params=_cparams(("parallel", "parallel")),
        name="modulation",
    )(cc, mod_w, mod_b.reshape(depth, 1, d6))


def kernel(x, c, ctx, c_ctx, mod_w, mod_b, norm1_g, norm2_g, ab_w_in, ab_w_out, hy_short_w, hy_short_b, hy_f_w1, hy_f_b1, hy_f_w2, hy_f_b2, hy_f_w3, hy_bias, s5_lambda_re, s5_lambda_im, s5_log_dt, s5_b_re, s5_b_im, s5_c_re, s5_c_im, s5_d, s5_glu_w, s5_glu_b, cd_w_in, cd_w_out, ret_decay_logit, swa_sink, router_w, router_b, exp_w_gu, exp_b_gu, exp_w_down, exp_b_down, final_g):
    bsz, seq, d = x.shape
    seq_c = ctx.shape[1]
    depth = mod_w.shape[0]
    m_l, m_c = bsz * seq, bsz * seq_c
    tm, tm_c, tt = 512, 256, 256
    xl = x.reshape(m_l, d)
    xc = ctx.reshape(m_c, d)
    cc = jnp.concatenate([c, c_ctx[None], jnp.zeros((SUBLANES - bsz - 1, d), F32)], axis=0)
    mods = _modulation(cc, mod_w, mod_b)
    hy_w = hy_f_w3.shape[2]
    n_blocks = _moe_slot_blocks(m_l + m_c)
    slots = None
    for layer in range(depth):
        with_ctx = layer < depth - 1
        i = layer // 2
        sh1, sc1, g1, sh2, sc2, g2 = [t[:, None, :] for t in jnp.split(mods[layer, :bsz], 6, axis=-1)]
        csh1, csc1, cg1, csh2, csc2, cg2 = [t[:, None, :] for t in jnp.split(mods[layer, bsz:bsz + 1], 6, axis=-1)]
        if layer % 2 == 0:
            w_in, w_out = ab_w_in[i], ab_w_out[i]
            splits = [(3 * hy_w, 0, None), (w_in.shape[1] - 3 * hy_w, 0, None)]
            pa, pb = _norm_mod_matmul(xl, norm1_g[layer], sc1, sh1, w_in, splits, tm, seq, "ab_in")
            pac, pbc = _norm_mod_matmul(xc, norm1_g[layer], csc1, csh1, w_in, splits, tm_c, m_c, "ab_in_ctx")
            hy = (hy_short_w[i], hy_short_b[i], hy_f_w1[i], hy_f_b1[i], hy_f_w2[i], hy_f_b2[i], hy_f_w3[i], hy_bias[i])
            ya = _hyena(pa, bsz, seq, *hy, tt)
            yb, ybc = _s5_mixer(pb, pbc, bsz, seq, seq_c, s5_lambda_re[i], s5_lambda_im[i], s5_log_dt[i], s5_b_re[i],
                                s5_b_im[i], s5_c_re[i], s5_c_im[i], s5_d[i], s5_glu_w[i], s5_glu_b[i], tm, tm_c)
            ws = [w_out[:hy_w], w_out[hy_w:]]
            xl = _out_proj([ya, yb], ws, xl, g1, tm, seq, "ab_out")
            if with_ctx:
                yac = _hyena(pac, bsz, seq_c, *hy, tt)
                xc = _out_proj([yac, ybc], ws, xc, cg1, tm_c, m_c, "ab_out_ctx")
        else:
            w_in, w_out = cd_w_in[i], cd_w_out[i]
            vw = 2 * RET_HEADS * (d // 16)
            qw = SWA_Q_HEADS * (d // 16)
            lat = _mixer_cd_in(xl, norm1_g[layer], sc1, sh1, w_in, d, seq, tm, seq, "cd_in", True)
            cx = _mixer_cd_in(xc, norm1_g[layer], csc1, csh1, w_in, d, seq_c, tm_c, m_c, "cd_in_ctx", False)
            ro_l, so_l, ro_c, so_c = _mixer_cd_core(lat, cx, bsz, seq, seq_c, ret_decay_logit[i], swa_sink[i],
                                                    with_ctx)
            ws = [w_out[:vw].reshape(RET_HEADS, vw // RET_HEADS, d), w_out[vw:].reshape(SWA_Q_HEADS, qw // SWA_Q_HEADS, d)]
            xl = _out_proj([ro_l, so_l], ws, xl, g1, tm, seq, "cd_out")
            if with_ctx:
                xc = _out_proj([ro_c, so_c], ws, xc, cg1, tm_c, m_c, "cd_out_ctx")
        moe_w = (router_w[layer], router_b[layer], layer, exp_w_gu, exp_b_gu, exp_w_down, exp_b_down)
        streams = [(xl, sc2, sh2, g2, seq, tm, tm_c)]
        if with_ctx:
            streams.append((xc, csc2, csh2, cg2, m_c, tm_c, tm_c))
        outs, slots = _moe_layer(streams, norm2_g[layer], *moe_w, n_blocks, slots)
        xl = outs[0]
        if with_ctx:
            xc = outs[1]
    out = _rows_call(_final_norm_body, [xl], [final_g.reshape(1, -1)], [], [((m_l, d), F32)], tm, m_l, "final_norm")[0]
    return out.reshape(bsz, seq, d)
```

```python
import functools
import math

import numpy as np
import jax
import jax.numpy as jnp
from jax import lax
from jax.experimental import pallas as pl
from jax.experimental.pallas import tpu as pltpu

F32 = jnp.float32
BF16 = jnp.bfloat16
HI = lax.Precision.HIGHEST

EPS = 1e-6
NEG_INF = -1e30
ROPE_BASE = 10000.0
GRID_W = 64

HY_SHORT = 3
HY_BANDS = 16
HY_SHIFT = 0.05
HY_FAST_DECAY = math.log(1e-2) / 0.3
HY_SLOW_DECAY = math.log(1e-2) / 1.5
S5_GROUP = 16
S5_STATE = 64
S5_CHUNK = 8
RET_HEADS = 4
RET_CHUNK = 128
SWA_Q_HEADS = 8
SWA_KV_HEADS = 2
SWA_WINDOW = 128
SWA_BLOCK = 128
N_EXPERTS = 32
TOP_K = 4
SWIGLU_LIMIT = 7.0
SWIGLU_ALPHA = 1.702

LANES = 128
SUBLANES = 8
VMEM_LIMIT = 52 * 2**20
FFT_B = 128
MOE_TM = 512
DMA_UNROLL = 4


def _cparams(sem):
    return pltpu.CompilerParams(dimension_semantics=sem, vmem_limit_bytes=VMEM_LIMIT)


def _rows_call(body, rows, consts, batched, outs, tm, rows_per_batch, name, periodic=()):
    m = rows[0].shape[-2]
    assert m % tm == 0 and rows_per_batch % tm == 0
    tiles_per_batch = rows_per_batch // tm

    def row_spec(shape):
        if len(shape) == 2:
            return pl.BlockSpec((tm, shape[1]), lambda i: (i, 0))
        return pl.BlockSpec((shape[0], tm, shape[2]), lambda i: (0, i, 0))

    in_specs = [row_spec(a.shape) for a in rows]
    for a in periodic:
        in_specs.append(pl.BlockSpec((tm, a.shape[1]), lambda i: (i % tiles_per_batch, 0)))
    for a in consts:
        in_specs.append(pl.BlockSpec(a.shape, lambda i, n=a.ndim: (0,) * n))
    for a in batched:
        in_specs.append(pl.BlockSpec((None, 1, a.shape[2]), lambda i: (i // tiles_per_batch, 0, 0)))
    out_specs = [row_spec(s) for s, _ in outs]
    out_shape = [jax.ShapeDtypeStruct(s, d) for s, d in outs]
    res = pl.pallas_call(
        body,
        grid=(m // tm,),
        in_specs=in_specs,
        out_specs=out_specs,
        out_shape=out_shape,
        compiler_params=_cparams(("parallel",)),
        name=name,
    )(*rows, *periodic, *consts, *batched)
    return res


def _norm_mod(x, g, sc, sh):
    y = x * lax.rsqrt(jnp.mean(x * x, axis=-1, keepdims=True) + EPS) * g
    return y * (1.0 + sc) + sh


def _rotate_half(x, cos, sin, shift):
    w = x.shape[1]
    lane = lax.broadcasted_iota(jnp.int32, x.shape, 1)
    partner = jnp.where(lane % (2 * shift) < shift, pltpu.roll(x, w - shift, axis=1), pltpu.roll(x, shift, axis=1))
    return x * cos + partner * sin


def _norm_mod_matmul_body(x_ref, *refs, splits, n_tab):
    tabs = refs[:2 * n_tab]
    g_ref, w_ref, sc_ref, sh_ref = refs[2 * n_tab:2 * n_tab + 4]
    o_refs = refs[2 * n_tab + 4:]
    h = _norm_mod(x_ref[...], g_ref[...], sc_ref[...], sh_ref[...])
    r = jnp.dot(h.astype(BF16), w_ref[...], preferred_element_type=F32)
    off = 0
    for o_ref, (n, heads, post) in zip(o_refs, splits):
        x = r[:, off:off + n]
        if post is not None:
            table, shift, scale = post
            if table is not None:
                x = _rotate_half(x, tabs[2 * table][...], tabs[2 * table + 1][...], shift)
            if scale != 1.0:
                x = x * scale
        if heads:
            hd = n // heads
            for h_i in range(heads):
                o_ref[h_i] = x[:, h_i * hd:(h_i + 1) * hd]
        else:
            o_ref[...] = x
        off += n


def _norm_mod_matmul(x2, g, sc, sh, w, splits, tm, rows_per_batch, name, tables=()):
    m = x2.shape[0]
    outs = [((heads, m, n // heads), F32) if heads else ((m, n), F32) for n, heads, _ in splits]
    body = functools.partial(_norm_mod_matmul_body, splits=splits, n_tab=len(tables))
    periodic = [t for pair in tables for t in pair]
    return _rows_call(body, [x2], [g.reshape(1, -1), w.astype(BF16)], [sc, sh], outs, tm, rows_per_batch, name,
                      periodic=periodic)


def _out_proj_body(*refs, n_in):
    a_refs = refs[:n_in]
    x_ref = refs[n_in]
    w_refs = refs[n_in + 1: 2 * n_in + 1]
    g_ref = refs[2 * n_in + 1]
    o_ref = refs[2 * n_in + 2]
    acc = None
    for a_ref, w_ref in zip(a_refs, w_refs):
        if a_ref.ndim == 3:
            for h_i in range(a_ref.shape[0]):
                t = jnp.dot(a_ref[h_i].astype(BF16), w_ref[h_i], preferred_element_type=F32)
                acc = t if acc is None else acc + t
        else:
            t = jnp.dot(a_ref[...].astype(BF16), w_ref[...], preferred_element_type=F32)
            acc = t if acc is None else acc + t
    o_ref[...] = x_ref[...] + g_ref[...] * acc


def _out_proj(parts, ws, x2, gate, tm, rows_per_batch, name):
    m, d = x2.shape
    body = functools.partial(_out_proj_body, n_in=len(parts))
    ws = [w.astype(BF16) for w in ws]
    return _rows_call(body, list(parts) + [x2], ws, [gate], [((m, d), F32)], tm, rows_per_batch, name)[0]


def _final_norm_body(x_ref, g_ref, o_ref):
    x = x_ref[...]
    o_ref[...] = x * lax.rsqrt(jnp.mean(x * x, axis=-1, keepdims=True) + EPS) * g_ref[...]


def _hy_prep_body(u_ref, p_ref, n_ref, w_ref, b_ref, x0_ref, z_ref, *, width):
    i = pl.program_id(1)
    last = pl.num_programs(1) - 1
    u = u_ref[...]
    tt = u.shape[0]
    prev_row = jnp.where(i == 0, 0.0, p_ref[SUBLANES - 1:SUBLANES, :])
    next_row = jnp.where(i == last, 0.0, n_ref[0:1, :])
    rows = lax.broadcasted_iota(jnp.int32, (tt, 1), 0)
    up = jnp.where(rows == 0, prev_row, pltpu.roll(u, 1, axis=0))
    dn = jnp.where(rows == tt - 1, next_row, pltpu.roll(u, tt - 1, axis=0))
    y = w_ref[0:1, :] * up + w_ref[1:2, :] * u + w_ref[2:3, :] * dn + b_ref[...]
    x0_ref[...] = y[:, :width]
    z_ref[...] = y[:, 2 * width:] * y[:, width:2 * width]


def _hyena_prep(p, short_w, short_b, bsz, seq, tt):
    w3 = p.shape[1]
    width = w3 // 3
    p3 = p.reshape(bsz, seq, w3)
    nt = seq // tt
    sub = tt // SUBLANES
    nsub = seq // SUBLANES
    body = functools.partial(_hy_prep_body, width=width)
    x0, z = pl.pallas_call(
        body,
        grid=(bsz, nt),
        in_specs=[
            pl.BlockSpec((None, tt, w3), lambda b, i: (b, i, 0)),
            pl.BlockSpec((None, SUBLANES, w3), lambda b, i: (b, jnp.maximum(i * sub - 1, 0), 0)),
            pl.BlockSpec((None, SUBLANES, w3), lambda b, i: (b, jnp.minimum((i + 1) * sub, nsub - 1), 0)),
            pl.BlockSpec((HY_SHORT, w3), lambda b, i: (0, 0)),
            pl.BlockSpec((1, w3), lambda b, i: (0, 0)),
        ],
        out_specs=[pl.BlockSpec((None, tt, width), lambda b, i: (b, i, 0))] * 2,
        out_shape=[jax.ShapeDtypeStruct((bsz, seq, width), F32)] * 2,
        compiler_params=_cparams(("parallel", "parallel")),
        name="hyena_prep",
    )(p3, p3, p3, short_w, short_b.reshape(1, -1))
    return x0, z


def _hy_filter_body(t_ref, w_ref, lag_ref, bands_ref, deltas_ref, w1t_ref, w1c_ref, w1s_ref, b1_ref,
                    w2_ref, b2_ref, w3_ref, h_ref, s_ref):
    i = pl.program_id(0)
    arg = w_ref[...] * bands_ref[...]
    pre = (t_ref[...] * w1t_ref[...]
           + jnp.dot(jnp.cos(arg), w1c_ref[...], precision=HI, preferred_element_type=F32)
           + jnp.dot(-jnp.sin(arg), w1s_ref[...], precision=HI, preferred_element_type=F32)
           + b1_ref[...])
    h1 = jnp.sin(pre)
    h2 = jnp.sin(jnp.dot(h1, w2_ref[...], precision=HI, preferred_element_type=F32) + b2_ref[...])
    h3 = jnp.dot(h2, w3_ref[...], precision=HI, preferred_element_type=F32)
    h = h3 * (jnp.exp(-lag_ref[...] * deltas_ref[...]) + HY_SHIFT)
    h_ref[...] = h

    @pl.when(i == 0)
    def _():
        s_ref[...] = jnp.zeros_like(s_ref)

    s_ref[...] += jnp.sum(jnp.abs(h), axis=0, keepdims=True)


def _hyena_filter(seq, w1, b1, w2, b2, w3):
    width = w3.shape[1]
    pos = jnp.arange(seq, dtype=F32)
    t = (pos / seq)[:, None]
    w = (2.0 * math.pi * pos / seq)[:, None]
    lag = (jnp.abs(pos - seq // 2) / (seq / 2))[:, None]
    bands = jnp.linspace(1e-4, HY_BANDS - 1, HY_BANDS, dtype=F32)[None]
    deltas = jnp.abs(jnp.linspace(HY_FAST_DECAY, HY_SLOW_DECAY, width, dtype=F32))[None]
    tl = min(seq, 1024)
    col = pl.BlockSpec((tl, 1), lambda i: (i, 0))

    def whole(a):
        return pl.BlockSpec(a.shape, lambda i, n=a.ndim: (0,) * n)

    consts = [bands, deltas, w1[0:1], w1[1:1 + HY_BANDS], w1[1 + HY_BANDS:], b1.reshape(1, -1),
              w2, b2.reshape(1, -1), w3]
    h, s = pl.pallas_call(
        _hy_filter_body,
        grid=(seq // tl,),
        in_specs=[col, col, col] + [whole(a) for a in consts],
        out_specs=[pl.BlockSpec((tl, width), lambda i: (i, 0)), pl.BlockSpec((1, width), lambda i: (0, 0))],
        out_shape=[jax.ShapeDtypeStruct((seq, width), F32), jax.ShapeDtypeStruct((1, width), F32)],
        compiler_params=_cparams(("arbitrary",)),
        name="hyena_filter",
    )(t, w, lag, *consts)
    return h, s


FFT_G = SUBLANES
FFT_CW = 512


def _kron_eye(mat):
    return np.kron(mat, np.eye(FFT_G))


def _dft_tables_k(na, ka, a_lo, a_cnt):
    n = na * FFT_B
    a = np.arange(na)
    b = np.arange(FFT_B)
    ang1 = 2.0 * np.pi * np.outer(a, a) / na
    ang2 = 2.0 * np.pi * np.outer(b, b) / FFT_B
    angt = 2.0 * np.pi * np.outer(b, a) / n
    c1, s1 = np.cos(ang1), np.sin(ang1)
    c2, s2 = np.cos(ang2), np.sin(ang2)
    nbb = FFT_B // FFT_G
    tw1 = angt.reshape(nbb, FFT_G, na).transpose(0, 2, 1).reshape(nbb, na * FFT_G, 1)
    rows = slice(a_lo, a_lo + a_cnt)
    tabs = dict(
        m1=_kron_eye(np.concatenate([c1, -s1], axis=0)[:, :ka]),
        f2=np.block([[c2, s2], [-s2, c2]]),
        f2i=np.block([[c2, -s2], [s2, c2]]),
        m3r=_kron_eye(c1[rows] / n), m3i=_kron_eye(-s1[rows] / n),
        tw1c=np.cos(tw1), tw1s=np.sin(tw1),
        twc_c=np.cos(angt).T[:, :, None], tws_c=np.sin(angt).T[:, :, None],
    )
    return {k: jnp.asarray(v, F32) for k, v in tabs.items()}


def _fftk1_body(x_ref, m_ref, tc_ref, ts_ref, sc_ref, o_ref, *, na):
    ka, g, cw = x_ref.shape
    v = (x_ref[...].reshape(ka * g, cw) * sc_ref[...]).astype(BF16)
    r = jnp.dot(m_ref[...], v, preferred_element_type=F32)
    gr, gi = r[:na * g], r[na * g:]
    tc, ts = tc_ref[...], ts_ref[...]
    o_ref[...] = jnp.concatenate([gr * tc + gi * ts, gi * tc - gr * ts], axis=0).reshape(2 * na, g, cw)


def _fftk_stage1(x5, scale, tabs, na):
    bz, ka, nbb, g, ch = x5.shape
    cw = min(ch, FFT_CW)
    m1 = tabs["m1"].astype(BF16)
    return pl.pallas_call(
        functools.partial(_fftk1_body, na=na),
        grid=(bz, ch // cw, nbb),
        in_specs=[
            pl.BlockSpec((None, ka, None, g, cw), lambda z, q, b: (z, 0, b, 0, q)),
            pl.BlockSpec(m1.shape, lambda z, q, b: (0, 0)),
            pl.BlockSpec((None, na * g, 1), lambda z, q, b: (b, 0, 0)),
            pl.BlockSpec((None, na * g, 1), lambda z, q, b: (b, 0, 0)),
            pl.BlockSpec((1, cw), lambda z, q, b: (0, q)),
        ],
        out_specs=pl.BlockSpec((None, 2 * na, None, g, cw), lambda z, q, b: (z, 0, b, 0, q)),
        out_shape=jax.ShapeDtypeStruct((bz, 2 * na, nbb, g, ch), F32),
        compiler_params=_cparams(("parallel", "parallel", "parallel")),
        name="fft_stage1",
    )(x5, m1, tabs["tw1c"], tabs["tw1s"], scale)


def _fftk2_spec_body(ar_ref, ai_ref, f_ref, o_ref):
    f = f_ref[...].astype(BF16)
    cw = ar_ref.shape[-1]
    for j in range(FFT_G):
        v = jnp.concatenate([ar_ref[j].reshape(FFT_B, cw), ai_ref[j].reshape(FFT_B, cw)], axis=0).astype(BF16)
        o_ref[j] = jnp.dot(f, v, preferred_element_type=F32)


def _fftk2_conv_body(ar_ref, ai_ref, h_ref, f_ref, fi_ref, tc_ref, ts_ref, o_ref):
    f = f_ref[...].astype(BF16)
    fi = fi_ref[...].astype(BF16)
    cw = ar_ref.shape[-1]
    nbb = FFT_B // FFT_G
    for j in range(FFT_G):
        v = jnp.concatenate([ar_ref[j].reshape(FFT_B, cw), ai_ref[j].reshape(FFT_B, cw)], axis=0).astype(BF16)
        x = jnp.dot(f, v, preferred_element_type=F32)
        xr, xi = x[:FFT_B], x[FFT_B:]
        hr, hi = h_ref[j, :FFT_B, :], h_ref[j, FFT_B:, :]
        p = jnp.concatenate([xr * hr - xi * hi, xr * hi + xi * hr], axis=0).astype(BF16)
        q = jnp.dot(fi, p, preferred_element_type=F32)
        qr, qi = q[:FFT_B], q[FFT_B:]
        tc, ts = tc_ref[j], ts_ref[j]
        o_ref[j, 0] = (qr * tc - qi * ts).reshape(nbb, FFT_G, cw)
        o_ref[j, 1] = (qi * tc + qr * ts).reshape(nbb, FFT_G, cw)


def _fftk_stage2_spectrum(a5, tabs, na):
    _, _, nbb, g, ch = a5.shape
    cw = min(ch, FFT_CW)
    ng = na // FFT_G
    return pl.pallas_call(
        _fftk2_spec_body,
        grid=(ch // cw, ng),
        in_specs=[
            pl.BlockSpec((None, FFT_G, nbb, g, cw), lambda q, c: (0, c, 0, 0, q)),
            pl.BlockSpec((None, FFT_G, nbb, g, cw), lambda q, c: (0, ng + c, 0, 0, q)),
            pl.BlockSpec((2 * FFT_B, 2 * FFT_B), lambda q, c: (0, 0)),
        ],
        out_specs=pl.BlockSpec((FFT_G, 2 * FFT_B, cw), lambda q, c: (c, 0, q)),
        out_shape=jax.ShapeDtypeStruct((na, 2 * FFT_B, ch), F32),
        compiler_params=_cparams(("parallel", "parallel")),
        name="fft_stage2_spectrum",
    )(a5, a5, tabs["f2"])


def _fftk_stage2_conv(a5, hspec, tabs, na):
    bz, _, nbb, g, ch = a5.shape
    cw = min(ch, FFT_CW)
    ng = na // FFT_G
    return pl.pallas_call(
        _fftk2_conv_body,
        grid=(ch // cw, ng, bz),
        in_specs=[
            pl.BlockSpec((None, FFT_G, nbb, g, cw), lambda q, c, z: (z, c, 0, 0, q)),
            pl.BlockSpec((None, FFT_G, nbb, g, cw), lambda q, c, z: (z, ng + c, 0, 0, q)),
            pl.BlockSpec((FFT_G, 2 * FFT_B, cw), lambda q, c, z: (c, 0, q)),
            pl.BlockSpec((2 * FFT_B, 2 * FFT_B), lambda q, c, z: (0, 0)),
            pl.BlockSpec((2 * FFT_B, 2 * FFT_B), lambda q, c, z: (0, 0)),
            pl.BlockSpec((FFT_G, FFT_B, 1), lambda q, c, z: (c, 0, 0)),
            pl.BlockSpec((FFT_G, FFT_B, 1), lambda q, c, z: (c, 0, 0)),
        ],
        out_specs=pl.BlockSpec((None, None, FFT_G, 2, nbb, g, cw), lambda q, c, z: (z, c, 0, 0, 0, 0, q)),
        out_shape=jax.ShapeDtypeStruct((bz, ng, FFT_G, 2, nbb, g, ch), F32),
        compiler_params=_cparams(("parallel", "parallel", "parallel")),
        name="fft_stage2_conv",
    )(a5, a5, hspec, tabs["f2"], tabs["f2i"], tabs["twc_c"], tabs["tws_c"])


def _fftk3_body(*refs, gate):
    if gate:
        br_ref, bi_ref, mr_ref, mi_ref, x0_ref, z_ref, bias_ref, o_ref = refs
    else:
        br_ref, bi_ref, mr_ref, mi_ref, o_ref = refs
    ng, gc, g, cw = br_ref.shape
    vr = br_ref[...].reshape(ng * gc * g, cw).astype(BF16)
    vi = bi_ref[...].reshape(ng * gc * g, cw).astype(BF16)
    y = jnp.dot(mr_ref[...], vr, preferred_element_type=F32) + jnp.dot(mi_ref[...], vi, preferred_element_type=F32)
    a_cnt = o_ref.shape[0]
    if gate:
        x0 = x0_ref[...].reshape(a_cnt * g, cw)
        z = z_ref[...].reshape(a_cnt * g, cw)
        y = x0 * (y + bias_ref[...] * z)
    o_ref[...] = y.reshape(a_cnt, g, cw)


def _fftk_stage3(b7, tabs, na, a_cnt, gate_args=None):
    bz, ng, gc, _, nbb, g, ch = b7.shape
    cw = min(ch, FFT_CW)
    mr, mi = tabs["m3r"].astype(BF16), tabs["m3i"].astype(BF16)
    row_spec = pl.BlockSpec((None, a_cnt, None, g, cw), lambda z, q, b: (z, 0, b, 0, q))
    in_specs = [
        pl.BlockSpec((None, ng, gc, None, None, g, cw), lambda z, q, b: (z, 0, 0, 0, b, 0, q)),
        pl.BlockSpec((None, ng, gc, None, None, g, cw), lambda z, q, b: (z, 0, 0, 1, b, 0, q)),
        pl.BlockSpec(mr.shape, lambda z, q, b: (0, 0)),
        pl.BlockSpec(mi.shape, lambda z, q, b: (0, 0)),
    ]
    args = [b7, b7, mr, mi]
    if gate_args is not None:
        in_specs += [row_spec, row_spec, pl.BlockSpec((1, cw), lambda z, q, b: (0, q))]
        args += list(gate_args)
    return pl.pallas_call(
        functools.partial(_fftk3_body, gate=gate_args is not None),
        grid=(bz, ch // cw, nbb),
        in_specs=in_specs,
        out_specs=row_spec,
        out_shape=jax.ShapeDtypeStruct((bz, a_cnt, nbb, g, ch), F32),
        compiler_params=_cparams(("parallel", "parallel", "parallel")),
        name="fft_stage3",
    )(*args)


def _hy_gate_body(x0_ref, y_ref, z_ref, b_ref, o_ref):
    o_ref[...] = x0_ref[...] * (y_ref[...] + b_ref[...] * z_ref[...])


def _hyena(p, bsz, seq, short_w, short_b, w1, b1, w2, b2, w3, hy_bias, tt):
    width = w3.shape[1]
    x0, z = _hyena_prep(p, short_w, short_b, bsz, seq, tt)
    hu, hs = _hyena_filter(seq, w1, b1, w2, b2, w3)
    na = max(2 * seq // FFT_B, 16)
    ka = max(seq // FFT_B, 16)
    pad = ka * FFT_B - seq
    a_lo, a_cnt = (0, na) if pad else (seq // 2 // FFT_B, ka)
    tabs = _dft_tables_k(na, ka, a_lo, a_cnt)
    nbb = FFT_B // FFT_G

    def rows5(a, lead):
        if pad:
            a = jnp.pad(a, ((0, 0), (0, pad), (0, 0)))
        return a.reshape(lead, ka, nbb, FFT_G, width)

    ones = jnp.ones((1, width), F32)
    bias = hy_bias.reshape(1, -1)
    hspec = _fftk_stage2_spectrum(_fftk_stage1(rows5(hu[None], 1), 1.0 / hs, tabs, na), tabs, na)
    a5 = _fftk_stage1(rows5(z, bsz), ones, tabs, na)
    b7 = _fftk_stage2_conv(a5, hspec, tabs, na)
    m = bsz * seq
    if not pad:
        out = _fftk_stage3(b7, tabs, na, a_cnt, (rows5(x0, bsz), rows5(z, bsz), bias))
        return out.reshape(m, width)
    y = _fftk_stage3(b7, tabs, na, a_cnt).reshape(bsz, na * FFT_B, width)[:, seq // 2: seq // 2 + seq]
    return _rows_call(_hy_gate_body, [x0.reshape(m, width), y.reshape(m, width), z.reshape(m, width)],
                      [bias], [], [((m, width), F32)], tt, seq, "hyena_gate")[0]


def _s5_tables(lam_re, lam_im, log_dt, b_re, b_im, c_re, c_im, nsteps):
    t_len, hdim = S5_CHUNK, S5_GROUP
    lam = lax.complex(jnp.minimum(lam_re.astype(F32), -1e-4), lam_im.astype(F32))
    dt = jnp.exp(log_dt.astype(F32))[..., None]
    lam_dt = lam * dt
    lam_bar = jnp.exp(lam_dt)
    b_bar = ((lam_bar - 1.0) / lam)[..., None] * lax.complex(b_re.astype(F32), b_im.astype(F32))
    cm = lax.complex(c_re.astype(F32), c_im.astype(F32))
    ks = jnp.arange(t_len + 1, dtype=F32)
    pw = jnp.exp(ks[:, None, None, None] * lam_dt[None])
    g = lam.shape[1]
    tabs = {}
    pin = [pw[:t_len, 0][::-1], pw[:t_len, 1]]
    pout = [pw[1:, 0], pw[1:, 1][::-1]]
    for d in range(2):
        win = pin[d][:, :, :, None] * b_bar[d][None]
        win = jnp.transpose(win, (1, 0, 3, 2)).reshape(g, t_len * hdim, -1)
        tabs[f"win{d}"] = jnp.concatenate([jnp.real(win), jnp.imag(win)], axis=-1)
        wout = pout[d][:, :, None, :] * cm[d][None]
        wout = jnp.transpose(wout, (1, 3, 0, 2)).reshape(g, -1, t_len * hdim)
        tabs[f"wout{d}"] = jnp.concatenate([jnp.real(wout), -jnp.imag(wout)], axis=1)
        mu = jnp.exp((t_len * 2.0 ** jnp.arange(nsteps, dtype=F32))[:, None, None] * lam_dt[d][None])
        mr, mi = jnp.real(mu), jnp.imag(mu)
        tabs[f"m1{d}"] = jnp.concatenate([mr, mr], axis=-1)[:, :, None, :]
        tabs[f"m2{d}"] = jnp.concatenate([-mi, mi], axis=-1)[:, :, None, :]
        mu1 = jnp.exp(t_len * lam_dt[d])
        tabs[f"mu1{d}"] = jnp.concatenate([jnp.real(mu1), jnp.real(mu1)], axis=-1)[:, None, :]
        tabs[f"mu2{d}"] = jnp.concatenate([-jnp.imag(mu1), jnp.imag(mu1)], axis=-1)[:, None, :]
    kern = [jnp.real(jnp.einsum("ghp,tgp,gpk->tghk", cm[d], pw[:t_len, d], b_bar[d], precision=HI)) for d in range(2)]
    s_idx = jnp.arange(t_len)[:, None]
    t_idx = jnp.arange(t_len)[None, :]
    fwd = jnp.where((t_idx >= s_idx)[:, :, None, None, None], kern[0][jnp.maximum(t_idx - s_idx, 0)], 0.0)
    bwd = jnp.where((s_idx >= t_idx)[:, :, None, None, None], kern[1][jnp.maximum(s_idx - t_idx, 0)], 0.0)
    d0 = fwd + bwd
    tabs["d0"] = jnp.transpose(d0, (2, 0, 4, 1, 3)).reshape(g, t_len * hdim, t_len * hdim)
    return _s5_pair_tables(tabs)


def _s5_pair_tables(tabs):
    half = S5_STATE

    def block_diag(a):
        g, r, c = a.shape
        a = a.reshape(g // 2, 2, r, c)
        z = jnp.zeros_like(a[:, 0])
        return jnp.concatenate([jnp.concatenate([a[:, 0], z], -1), jnp.concatenate([z, a[:, 1]], -1)], axis=-2)

    def lanes(a):
        return jnp.concatenate([a[..., 0::2, :, :], a[..., 1::2, :, :]], axis=-1)

    out = {"d0": block_diag(tabs["d0"])}
    for d in range(2):
        win, wout = tabs[f"win{d}"], tabs[f"wout{d}"]
        out[f"winr{d}"], out[f"wini{d}"] = block_diag(win[..., :half]), block_diag(win[..., half:])
        out[f"woutr{d}"], out[f"wouti{d}"] = block_diag(wout[:, :half, :]), block_diag(wout[:, half:, :])
        out[f"mr{d}"], out[f"mi{d}"] = lanes(tabs[f"m1{d}"][..., :half]), lanes(tabs[f"m2{d}"][..., half:])
        out[f"mur{d}"], out[f"mui{d}"] = lanes(tabs[f"mu1{d}"][..., :half]), lanes(tabs[f"mu2{d}"][..., half:])
    return out


S5_GPB = LANES // S5_GROUP


def _s5_perm_table():
    p = np.zeros((S5_CHUNK, S5_GPB, LANES, LANES), np.float32)
    for t in range(S5_CHUNK):
        for g in range(S5_GPB):
            for h in range(S5_GROUP):
                p[t, g, S5_GROUP * g + h, S5_GROUP * t + h] = 1.0
    return jnp.asarray(p)


S5_TABLES = ("d0", "winr0", "wini0", "winr1", "wini1", "woutr0", "wouti0", "woutr1", "wouti1")
S5_STEP_TABLES = ("mr0", "mi0", "mr1", "mi1")
S5_LAST_TABLES = ("mur0", "mui0", "mur1", "mui1")


def _s5_body(u_ref, perm_ref, permt_ref, *refs, nc, nsteps):
    nt, ns = len(S5_TABLES), len(S5_STEP_TABLES)
    tab = dict(zip(S5_TABLES + S5_STEP_TABLES + S5_LAST_TABLES, refs))
    s0_ref, s1_ref, y_ref, f0_ref, f1_ref = refs[nt + ns + len(S5_LAST_TABLES):]
    t_len = S5_CHUNK
    j = lax.broadcasted_iota(jnp.int32, (nc, 1), 0)

    def mm(a, name, p):
        return jnp.dot(a.astype(BF16), tab[name][p].astype(BF16), preferred_element_type=F32)

    def scan(e_re, e_im, init_ref, p, d):
        first = (j == 0) if d == 0 else (j == nc - 1)
        one = 1 if d == 0 else nc - 1
        s_re = jnp.where(first, init_ref[p, 0:1, :], pltpu.roll(e_re, one, axis=0))
        s_im = jnp.where(first, init_ref[p, 1:2, :], pltpu.roll(e_im, one, axis=0))
        for k in range(nsteps):
            step = 2 ** k
            keep = (j >= step) if d == 0 else (j < nc - step)
            shift = step if d == 0 else nc - step
            sh_re = jnp.where(keep, pltpu.roll(s_re, shift, axis=0), 0.0)
            sh_im = jnp.where(keep, pltpu.roll(s_im, shift, axis=0), 0.0)
            mr, mi = tab[f"mr{d}"][k, p], tab[f"mi{d}"][k, p]
            s_re, s_im = s_re + mr * sh_re - mi * sh_im, s_im + mr * sh_im + mi * sh_re
        return s_re, s_im

    xs = [u_ref[pl.ds(t, nc, stride=t_len), :].astype(BF16) for t in range(t_len)]
    ys = []
    for p in range(S5_GPB // 2):
        us = []
        for g in (2 * p, 2 * p + 1):
            u = None
            for t in range(t_len):
                part = jnp.dot(xs[t], perm_ref[t, g].astype(BF16), preferred_element_type=F32)
                u = part if u is None else u + part
            us.append(u.astype(BF16))
        u = jnp.concatenate(us, axis=1)
        y = mm(u, "d0", p)
        for d, (init_ref, fin_ref) in enumerate(((s0_ref, f0_ref), (s1_ref, f1_ref))):
            e_re, e_im = mm(u, f"winr{d}", p), mm(u, f"wini{d}", p)
            s_re, s_im = scan(e_re, e_im, init_ref, p, d)
            y = y + mm(s_re, f"woutr{d}", p) + mm(s_im, f"wouti{d}", p)
            mur, mui = tab[f"mur{d}"][p], tab[f"mui{d}"][p]
            last = slice(nc - 1, nc) if d == 0 else slice(0, 1)
            fin_ref[p, 0:1, :] = (mur * s_re - mui * s_im + e_re)[last, :]
            fin_ref[p, 1:2, :] = (mur * s_im + mui * s_re + e_im)[last, :]
        for y_g in (y[:, :LANES], y[:, LANES:]):
            y_hi = y_g.astype(BF16)
            ys.append((y_hi, (y_g - y_hi.astype(F32)).astype(BF16)))
    for t in range(t_len):
        out = None
        for g in range(S5_GPB):
            pt = permt_ref[t, g].astype(BF16)
            part = jnp.dot(ys[g][0], pt, preferred_element_type=F32) + jnp.dot(ys[g][1], pt, preferred_element_type=F32)
            out = part if out is None else out + part
        y_ref[pl.ds(t, nc, stride=t_len), :] = out


def _s5_core(u2, bsz, seq, tabs, init0, init1):
    m, width = u2.shape
    pairs = width // S5_GROUP // 2
    ppb = S5_GPB // 2
    nq = width // LANES
    t_len = S5_CHUNK
    nc = seq // t_len
    nsteps = max(1, math.ceil(math.log2(nc)))
    assert nsteps <= tabs["mr0"].shape[0] and t_len * S5_GROUP == LANES
    perm = _s5_perm_table()
    permt = jnp.swapaxes(perm, 2, 3)

    def whole(a):
        return pl.BlockSpec(a.shape, lambda q, b, nd=a.ndim: (0,) * nd)

    def per_q(a):
        return pl.BlockSpec((ppb,) + a.shape[1:], lambda q, b, nd=a.ndim: (q,) + (0,) * (nd - 1))

    def per_q_steps(a):
        return pl.BlockSpec((nsteps, ppb) + a.shape[2:], lambda q, b: (0, q, 0, 0))

    p2 = 2 * S5_STATE
    state_spec = pl.BlockSpec((None, ppb, 2, p2), lambda q, b: (b, q, 0, 0))
    seq_spec = pl.BlockSpec((seq, LANES), lambda q, b: (b, q))
    mats = [tabs[k] for k in S5_TABLES]
    steps = [tabs[k][:nsteps] for k in S5_STEP_TABLES]
    lasts = [tabs[k] for k in S5_LAST_TABLES]
    in_specs = [seq_spec, whole(perm), whole(permt)] + [per_q(a) for a in mats] + [per_q_steps(a) for a in steps] \
        + [per_q(a) for a in lasts] + [state_spec] * 2
    state_shape = jax.ShapeDtypeStruct((bsz, pairs, 2, p2), F32)
    y, f0, f1 = pl.pallas_call(
        functools.partial(_s5_body, nc=nc, nsteps=nsteps),
        grid=(nq, bsz),
        in_specs=in_specs,
        out_specs=[seq_spec, state_spec, state_spec],
        out_shape=[jax.ShapeDtypeStruct((m, width), F32), state_shape, state_shape],
        compiler_params=_cparams(("parallel", "parallel")),
        name="s5_scan",
    )(u2, perm, permt, *mats, *steps, *lasts, init0, init1)
    return y, f0, f1


def _s5_glu_body(u_ref, y_ref, d_ref, w_ref, b_ref, o_ref):
    y = d_ref[...] * u_ref[...] + y_ref[...]
    g = jax.nn.gelu(y)
    o_ref[...] = g * jax.nn.sigmoid(jnp.dot(g.astype(BF16), w_ref[...], preferred_element_type=F32) + b_ref[...])


def _s5_glu(u2, y2, d_skip, glu_w, glu_b, tm, name):
    m, width = u2.shape
    return _rows_call(_s5_glu_body, [u2, y2], [d_skip.reshape(1, -1), glu_w.astype(BF16), glu_b.reshape(1, -1)], [],
                      [((m, width), F32)], tm, m, name)[0]


def _s5_mixer(u_lat, u_ctx, bsz, seq, seq_c, lam_re, lam_im, log_dt, b_re, b_im, c_re, c_im, d_skip, glu_w, glu_b,
              tm, tm_c):
    nsteps = max(1, math.ceil(math.log2(seq // S5_CHUNK)))
    tabs = _s5_tables(lam_re, lam_im, log_dt, b_re, b_im, c_re, c_im, nsteps)
    g = u_lat.shape[1] // S5_GROUP
    zero = jnp.zeros((bsz, g // 2, 2, 2 * S5_STATE), F32)
    y_ctx, f0, f1 = _s5_core(u_ctx, bsz, seq_c, tabs, zero, zero)
    y_lat, _, _ = _s5_core(u_lat, bsz, seq, tabs, f0, f1)
    out_lat = _s5_glu(u_lat, y_lat, d_skip, glu_w, glu_b, tm, "s5_glu")
    out_ctx = _s5_glu(u_ctx, y_ctx, d_skip, glu_w, glu_b, tm_c, "s5_glu_ctx")
    return out_lat, out_ctx


def _rope_tables_1d(seq, hd, heads):
    half = hd // 2
    inv = ROPE_BASE ** (-jnp.arange(half, dtype=F32) / half)
    ang = jnp.arange(seq, dtype=F32)[:, None] * inv[None]
    cos, sin = jnp.cos(ang), jnp.sin(ang)
    return jnp.tile(jnp.concatenate([cos, cos], -1), (1, heads)), jnp.tile(jnp.concatenate([-sin, sin], -1), (1, heads))


def _rope_tables_2d(seq, hd, heads):
    q = hd // 4
    inv = ROPE_BASE ** (-jnp.arange(q, dtype=F32) / q)
    n_rows = seq // GRID_W
    row = jnp.repeat(jnp.arange(n_rows, dtype=F32), GRID_W)
    col = jnp.tile(jnp.arange(GRID_W, dtype=F32), n_rows)
    ar, ac = row[:, None] * inv[None], col[:, None] * inv[None]
    cos = jnp.concatenate([jnp.cos(ar), jnp.cos(ar), jnp.cos(ac), jnp.cos(ac)], -1)
    sin = jnp.concatenate([-jnp.sin(ar), jnp.sin(ar), -jnp.sin(ac), jnp.sin(ac)], -1)
    return jnp.tile(cos, (1, heads)), jnp.tile(sin, (1, heads))


def _kv_update(k, kdec, v):
    kd = (k * kdec).astype(BF16)
    return lax.dot_general(kd, v.astype(BF16), (((0,), (0,)), ((), ())), preferred_element_type=F32)


def _ret_bwd_body(k_ref, v_ref, kdec_ref, gc_ref, init_ref, sprev_ref, fin_ref, s_ref):
    @pl.when(pl.program_id(1) == 0)
    def _():
        s_ref[...] = init_ref[...]

    for h in range(k_ref.shape[0]):
        s = s_ref[h]
        sprev_ref[h] = s
        s = gc_ref[h] * s + _kv_update(k_ref[h], kdec_ref[h], v_ref[h])
        s_ref[h] = s
        fin_ref[h] = s


def _ret_main_body(q_ref, k_ref, v_ref, g_ref, mask_ref, qdf_ref, kdf_ref, qdb_ref, gc_ref, init_ref, sb_ref,
                   o_ref, fin_ref, s_ref):
    @pl.when(pl.program_id(1) == 0)
    def _():
        s_ref[...] = init_ref[...]

    for h in range(q_ref.shape[0]):
        q, k, v = q_ref[h], k_ref[h], v_ref[h]
        vb = v.astype(BF16)
        sc = lax.dot_general(q.astype(BF16), k.astype(BF16), (((1,), (1,)), ((), ())), preferred_element_type=F32)
        o = jnp.dot((sc * mask_ref[h]).astype(BF16), vb, preferred_element_type=F32)
        s = s_ref[h]
        o = o + jnp.dot((q * qdf_ref[h]).astype(BF16), s.astype(BF16), preferred_element_type=F32)
        o = o + jnp.dot((q * qdb_ref[h]).astype(BF16), sb_ref[h].astype(BF16), preferred_element_type=F32)
        o = o * lax.rsqrt(jnp.mean(o * o, axis=-1, keepdims=True) + EPS)
        o_ref[h] = o * jax.nn.silu(g_ref[h])
        s = gc_ref[h] * s + _kv_update(k, kdf_ref[h], v)
        s_ref[h] = s
        fin_ref[h] = s


def _ret_tables(decay_logit):
    cl = RET_CHUNK
    log_g = jax.nn.log_sigmoid(decay_logit.astype(F32))
    idx = jnp.arange(cl, dtype=F32)
    diff = idx[:, None] - idx[None, :]
    mf = jnp.where(diff[None] >= 0, jnp.exp(jnp.maximum(diff, 0.0)[None] * log_g[0][:, None, None]), 0.0)
    mb = jnp.where(diff[None] <= 0, jnp.exp(jnp.maximum(-diff, 0.0)[None] * log_g[1][:, None, None]), 0.0)

    def col(e, d):
        return jnp.exp(e[None, :] * log_g[d][:, None])[:, :, None]

    return dict(mask=mf + mb, qdf=col(idx + 1.0, 0), kdf=col(cl - 1.0 - idx, 0), qdb=col(cl - idx, 1), kdb=col(idx, 1),
                gcf=jnp.exp(cl * log_g[0])[:, None, None], gcb=jnp.exp(cl * log_g[1])[:, None, None])


def _retention(q, k, v, g, bsz, seq, tabs, init_f, init_b):
    heads, _, dk = q.shape
    dv = v.shape[2]
    cl = RET_CHUNK
    n = seq // cl

    def whole(a):
        return pl.BlockSpec(a.shape, lambda b, i, nd=a.ndim: (0,) * nd)

    state_spec = pl.BlockSpec((None, heads, dk, dv), lambda b, i: (b, 0, 0, 0))
    state_shape = jax.ShapeDtypeStruct((bsz, heads, dk, dv), F32)
    rev = lambda b, i: (0, b * n + (n - 1 - i), 0)
    sprev_b, fin_b = pl.pallas_call(
        _ret_bwd_body,
        grid=(bsz, n),
        in_specs=[pl.BlockSpec((heads, cl, dk), rev), pl.BlockSpec((heads, cl, dv), rev),
                  whole(tabs["kdb"]), whole(tabs["gcb"]), state_spec],
        out_specs=[pl.BlockSpec((None, None, heads, dk, dv), lambda b, i: (b, n - 1 - i, 0, 0, 0)), state_spec],
        out_shape=[jax.ShapeDtypeStruct((bsz, n, heads, dk, dv), F32), state_shape],
        scratch_shapes=[pltpu.VMEM((heads, dk, dv), F32)],
        compiler_params=_cparams(("parallel", "arbitrary")),
        name="retention_backward_states",
    )(k, v, tabs["kdb"], tabs["gcb"], init_b)
    fwd = lambda b, i: (0, b * n + i, 0)
    o, fin_f = pl.pallas_call(
        _ret_main_body,
        grid=(bsz, n),
        in_specs=[pl.BlockSpec((heads, cl, dk), fwd), pl.BlockSpec((heads, cl, dk), fwd),
                  pl.BlockSpec((heads, cl, dv), fwd), pl.BlockSpec((heads, cl, dv), fwd),
                  whole(tabs["mask"]), whole(tabs["qdf"]), whole(tabs["kdf"]), whole(tabs["qdb"]), whole(tabs["gcf"]),
                  state_spec, pl.BlockSpec((None, None, heads, dk, dv), lambda b, i: (b, i, 0, 0, 0))],
        out_specs=[pl.BlockSpec((heads, cl, dv), fwd), state_spec],
        out_shape=[jax.ShapeDtypeStruct((heads, bsz * seq, dv), F32), state_shape],
        scratch_shapes=[pltpu.VMEM((heads, dk, dv), F32)],
        compiler_params=_cparams(("parallel", "arbitrary")),
        name="retention",
    )(q, k, v, g, tabs["mask"], tabs["qdf"], tabs["kdf"], tabs["qdb"], tabs["gcf"], init_f, sprev_b)
    return o, fin_f, fin_b


def _swa_body(*refs, local, seq, scale):
    if local:
        q_ref, kp_ref, kc_ref, kn_ref, vp_ref, vc_ref, vn_ref, kx_ref, vx_ref, sink_ref, o_ref = refs
    else:
        q_ref, kx_ref, vx_ref, sink_ref, o_ref = refs
    i = pl.program_id(1)
    bk = q_ref.shape[1]
    nkv = kx_ref.shape[0]
    grp = q_ref.shape[0] // nkv
    nt = (((1,), (1,)), ((), ()))
    hd = q_ref.shape[2]
    rows = grp * bk
    seq_c = kx_ref.shape[1]
    if local:
        row = lax.broadcasted_iota(jnp.int32, (rows, 3 * bk + seq_c), 0) % bk
        col = lax.broadcasted_iota(jnp.int32, (rows, 3 * bk + seq_c), 1)
        key_pos = (i - 1) * bk + col
        valid = (col >= 3 * bk) | ((jnp.abs(col - (row + bk)) <= SWA_WINDOW) & (key_pos >= 0) & (key_pos < seq))
    for kv in range(nkv):
        if local:
            k_all = jnp.concatenate([kp_ref[kv], kc_ref[kv], kn_ref[kv], kx_ref[kv]], axis=0).astype(BF16)
            v_all = jnp.concatenate([vp_ref[kv], vc_ref[kv], vn_ref[kv], vx_ref[kv]], axis=0).astype(BF16)
        else:
            k_all = kx_ref[kv].astype(BF16)
            v_all = vx_ref[kv].astype(BF16)
        q = (q_ref[kv * grp:(kv + 1) * grp].reshape(rows, hd) * scale).astype(BF16)
        sink = jnp.concatenate([jnp.broadcast_to(sink_ref[kv * grp + gi], (bk, 1)) for gi in range(grp)], axis=0)
        s = lax.dot_general(q, k_all, nt, preferred_element_type=F32)
        if local:
            s = jnp.where(valid, s, NEG_INF)
        m = jnp.maximum(jnp.max(s, axis=-1, keepdims=True), sink)
        p = jnp.exp(s - m)
        den = jnp.sum(p, axis=-1, keepdims=True) + jnp.exp(sink - m)
        o = jnp.dot(p.astype(BF16), v_all, preferred_element_type=F32)
        o_ref[kv * grp:(kv + 1) * grp] = (o / den).reshape(grp, bk, hd)


def _swa(q, k, v, kx, vx, sink, bsz, seq, seq_c, local):
    hq, m, hd = q.shape
    hkv = kx.shape[0]
    bk = SWA_BLOCK
    nb = seq // bk
    scale = hd ** -0.5
    cur = lambda b, i: (0, b * nb + i, 0)
    prv = lambda b, i: (0, b * nb + jnp.maximum(i - 1, 0), 0)
    nxt = lambda b, i: (0, b * nb + jnp.minimum(i + 1, nb - 1), 0)
    ctx_spec = pl.BlockSpec((hkv, seq_c, hd), lambda b, i: (0, b, 0))
    sink_spec = pl.BlockSpec((hq, 1, 1), lambda b, i: (0, 0, 0))
    kvb = lambda f: pl.BlockSpec((hkv, bk, hd), f)
    in_specs = [pl.BlockSpec((hq, bk, hd), cur)]
    args = [q]
    if local:
        in_specs += [kvb(prv), kvb(cur), kvb(nxt), kvb(prv), kvb(cur), kvb(nxt)]
        args += [k, k, k, v, v, v]
    in_specs += [ctx_spec, ctx_spec, sink_spec]
    args += [kx, vx, sink.astype(F32).reshape(hq, 1, 1)]
    return pl.pallas_call(
        functools.partial(_swa_body, local=local, seq=seq, scale=scale),
        grid=(bsz, nb),
        in_specs=in_specs,
        out_specs=pl.BlockSpec((hq, bk, hd), cur),
        out_shape=jax.ShapeDtypeStruct((hq, m, hd), F32),
        compiler_params=_cparams(("parallel", "parallel")),
        name="swa" if local else "swa_ctx",
    )(*args)


def _mixer_cd_in(x2, g, sc, sh, w_in, d, seq, tm, rows_per_batch, name, rope):
    hd = d // 16
    qk, vw = RET_HEADS * hd, 2 * RET_HEADS * hd
    qw, kw = SWA_Q_HEADS * hd, SWA_KV_HEADS * hd
    if rope:
        t2 = _rope_tables_2d(seq, hd, 1)
        tables = [_rope_tables_1d(seq, hd, RET_HEADS), [jnp.tile(t, (1, SWA_Q_HEADS)) for t in t2],
                  [jnp.tile(t, (1, SWA_KV_HEADS)) for t in t2]]
        post = [(0, hd // 2, hd ** -0.5), (0, hd // 2, 1.0), (1, hd // 4, 1.0), (2, hd // 4, 1.0)]
    else:
        tables = []
        post = [(None, 0, hd ** -0.5), None, None, None]
    splits = [(qk, RET_HEADS, post[0]), (qk, RET_HEADS, post[1]), (vw, RET_HEADS, None), (vw, RET_HEADS, None),
              (qw, SWA_Q_HEADS, post[2]), (kw, SWA_KV_HEADS, post[3]), (kw, SWA_KV_HEADS, None)]
    names = ("rq", "rk", "rv", "rg", "sq", "sk", "sv")
    return dict(zip(names, _norm_mod_matmul(x2, g, sc, sh, w_in, splits, tm, rows_per_batch, name, tables)))


def _mixer_cd_core(lat, cx, bsz, seq, seq_c, decay_logit, sink, with_ctx):
    dk, dv = lat["rq"].shape[2], lat["rv"].shape[2]
    tabs = _ret_tables(decay_logit)
    zero = jnp.zeros((bsz, RET_HEADS, dk, dv), F32)
    ro_c, fin_f, fin_b = _retention(cx["rq"], cx["rk"], cx["rv"], cx["rg"], bsz, seq_c, tabs, zero, zero)
    ro_l, _, _ = _retention(lat["rq"], lat["rk"], lat["rv"], lat["rg"], bsz, seq, tabs, fin_f, fin_b)
    so_l = _swa(lat["sq"], lat["sk"], lat["sv"], cx["sk"], cx["sv"], sink, bsz, seq, seq_c, True)
    so_c = None
    if with_ctx:
        so_c = _swa(cx["sq"], None, None, cx["sk"], cx["sv"], sink, bsz, seq_c, seq_c, False)
    return ro_l, so_l, (ro_c if with_ctx else None), so_c


def _store_tile_rows(ref, val, base=0):
    r, d = val.shape
    sub = d // LANES
    for c in range(sub):
        ref[pl.ds(base + c, r, stride=sub), :] = val[:, c * LANES:(c + 1) * LANES]


def _load_tile_rows(ref, r, sub, base=0):
    return jnp.concatenate([ref[pl.ds(base + c, r, stride=sub), :] for c in range(sub)], axis=1)


def _router_body(x_ref, g_ref, rw_ref, rb_ref, tril_ref, sc_ref, sh_ref, hx_ref, ti_ref, gt_ref, pos_ref, cnt_ref):
    @pl.when(pl.program_id(0) == 0)
    def _():
        cnt_ref[...] = jnp.zeros_like(cnt_ref)

    hx = _norm_mod(x_ref[...], g_ref[...], sc_ref[...], sh_ref[...])
    _store_tile_rows(hx_ref, hx)
    logits = jnp.dot(hx, rw_ref[...], precision=HI, preferred_element_type=F32) + rb_ref[...]
    lane = lax.broadcasted_iota(jnp.int32, logits.shape, 1).astype(F32)
    rem = logits
    vals, hots = [], []
    for k in range(TOP_K):
        m = jnp.max(rem, axis=-1, keepdims=True)
        idx = jnp.min(jnp.where(rem == m, lane, float(N_EXPERTS)), axis=-1, keepdims=True)
        hot = lane == idx
        rem = jnp.where(hot, NEG_INF, rem)
        vals.append(m)
        hots.append(hot.astype(F32))
        ti_ref[:, k:k + 1] = idx.astype(jnp.int32)
    exps = [jnp.exp(v - vals[0]) for v in vals]
    den = exps[0] + exps[1] + exps[2] + exps[3]
    for k in range(TOP_K):
        gt_ref[:, k:k + 1] = exps[k] / den
    sel = hots[0] + hots[1] + hots[2] + hots[3]
    before = jnp.dot(tril_ref[...], sel.astype(BF16), preferred_element_type=F32) + cnt_ref[...]
    for k in range(TOP_K):
        pos_ref[:, k:k + 1] = jnp.sum(hots[k] * before, axis=-1, keepdims=True).astype(jnp.int32)
    cnt_ref[...] += jnp.sum(sel, axis=0, keepdims=True)


def _moe_route(x2, g, sc, sh, router_w, router_b, tr, rows_per_batch):
    n, d = x2.shape
    tiles_per_batch = rows_per_batch // tr
    tril = jnp.asarray(np.tril(np.ones((tr, tr), np.float32), -1)).astype(BF16)
    whole = lambda a: pl.BlockSpec(a.shape, lambda i, nd=a.ndim: (0,) * nd)
    bat = pl.BlockSpec((None, 1, d), lambda i: (i // tiles_per_batch, 0, 0))
    g2, rb2 = g.reshape(1, -1), router_b.reshape(1, -1)
    small = lambda dt: jax.ShapeDtypeStruct((n, TOP_K), dt)
    small_spec = pl.BlockSpec((tr, TOP_K), lambda i: (i, 0))
    return pl.pallas_call(
        _router_body,
        grid=(n // tr,),
        in_specs=[pl.BlockSpec((tr, d), lambda i: (i, 0)), whole(g2), whole(router_w), whole(rb2), whole(tril), bat, bat],
        out_specs=[pl.BlockSpec((tr * (d // LANES), LANES), lambda i: (i, 0)), small_spec, small_spec, small_spec,
                   pl.BlockSpec((1, N_EXPERTS), lambda i: (0, 0))],
        out_shape=[jax.ShapeDtypeStruct((n * (d // LANES), LANES), F32), small(jnp.int32), small(F32), small(jnp.int32),
                   jax.ShapeDtypeStruct((1, N_EXPERTS), F32)],
        compiler_params=_cparams(("arbitrary",)),
        name="moe_router",
    )(x2, g2, router_w, rb2, tril, sc, sh)


def _tile_row_copy(src, s_off, dst, d_off, sem, sub):
    return pltpu.make_async_copy(src.at[pl.ds(pl.multiple_of(s_off, sub), sub)],
                                 dst.at[pl.ds(pl.multiple_of(d_off, sub), sub)], sem)


def _dispatch_body(dest_ref, hx_ref, xs_in_ref, xs_ref, sem, *, td, sub):
    del xs_in_ref

    def issue(n, carry):
        for k in range(TOP_K):
            _tile_row_copy(hx_ref, n * sub, xs_ref, dest_ref[n * TOP_K + k], sem, sub).start(priority=k % 2)
        return carry

    def drain(n, carry):
        for k in range(TOP_K):
            _tile_row_copy(hx_ref, 0, xs_ref, 0, sem, sub).wait()
        return carry

    lax.fori_loop(0, td, issue, 0, unroll=DMA_UNROLL)
    lax.fori_loop(0, td, drain, 0, unroll=DMA_UNROLL)


def _moe_dispatch(hx, dest_off, n_slots, td, sub, xs_init=None):
    lanes = hx.shape[1]
    n = hx.shape[0] // sub
    zeros = jnp.zeros((n_slots * sub, lanes), F32) if xs_init is None else xs_init
    return pl.pallas_call(
        functools.partial(_dispatch_body, td=td, sub=sub),
        grid=(n // td,),
        in_specs=[pl.BlockSpec((td * TOP_K,), lambda i: (i,), memory_space=pltpu.SMEM),
                  pl.BlockSpec((td * sub, lanes), lambda i: (i, 0)), pl.BlockSpec(memory_space=pl.ANY)],
        out_specs=pl.BlockSpec(memory_space=pl.ANY),
        out_shape=jax.ShapeDtypeStruct((n_slots * sub, lanes), F32),
        scratch_shapes=[pltpu.SemaphoreType.DMA(())],
        input_output_aliases={2: 0},
        compiler_params=pltpu.CompilerParams(dimension_semantics=("arbitrary",), has_side_effects=True,
                                             vmem_limit_bytes=VMEM_LIMIT),
        name="moe_dispatch",
    )(dest_off, hx, zeros)


def _ffn_body(be_ref, nu_ref, x_ref, wgu_ref, bgu_ref, wdn_ref, bdn_ref, o_ref, wgu_bf, wdn_bf, *, tm):
    j = pl.program_id(0)
    e = be_ref[j]
    prev = be_ref[jnp.maximum(j - 1, 0)]

    @pl.when((j == 0) | (e != prev))
    def _():
        wgu_bf[...] = wgu_ref[...].astype(BF16)
        wdn_bf[...] = wdn_ref[...].astype(BF16)

    @pl.when(j < nu_ref[0])
    def _():
        f = wdn_ref.shape[0]
        x = _load_tile_rows(x_ref, tm, wgu_ref.shape[0] // LANES).astype(BF16)
        gu = jnp.dot(x, wgu_bf[...], preferred_element_type=F32) + bgu_ref[...]
        gate = jnp.minimum(gu[:, :f], SWIGLU_LIMIT)
        up = jnp.clip(gu[:, f:], -SWIGLU_LIMIT, SWIGLU_LIMIT)
        act = gate * jax.nn.sigmoid(SWIGLU_ALPHA * gate) * (up + 1.0)
        _store_tile_rows(o_ref, jnp.dot(act.astype(BF16), wdn_bf[...], preferred_element_type=F32) + bdn_ref[...])

    @pl.when(j >= nu_ref[0])
    def _():
        o_ref[...] = jnp.zeros_like(o_ref)


def _moe_ffn(xs, block_exp, n_used, layer, w_gu, b_gu, w_dn, b_dn, tm):
    depth, n_exp, d, f2 = w_gu.shape
    f = w_dn.shape[2]
    sub = d // LANES
    n_slots = xs.shape[0] // sub
    blk = lambda j, be, nu: (jnp.minimum(j, nu[0] - 1), 0)
    exp4 = lambda j, be, nu: (layer, be[j], 0, 0)
    grid_spec = pltpu.PrefetchScalarGridSpec(
        num_scalar_prefetch=2,
        grid=(n_slots // tm,),
        in_specs=[pl.BlockSpec((tm * sub, LANES), blk), pl.BlockSpec((None, None, d, f2), exp4),
                  pl.BlockSpec((None, None, 1, f2), exp4), pl.BlockSpec((None, None, f, d), exp4),
                  pl.BlockSpec((None, None, 1, d), exp4)],
        out_specs=pl.BlockSpec((tm * sub, LANES), lambda j, be, nu: (j, 0)),
        scratch_shapes=[pltpu.VMEM((d, f2), BF16), pltpu.VMEM((f, d), BF16)],
    )
    return pl.pallas_call(
        functools.partial(_ffn_body, tm=tm),
        grid_spec=grid_spec,
        out_shape=jax.ShapeDtypeStruct((n_slots * sub, LANES), F32),
        compiler_params=_cparams(("arbitrary",)),
        name="moe_ffn",
    )(block_exp, n_used, xs, w_gu, b_gu.reshape(depth, n_exp, 1, f2), w_dn, b_dn.reshape(depth, n_exp, 1, d))


def _combine_body(dest_ref, gt_ref, x_ref, ys_ref, g_ref, o_ref, buf, sem, *, tc, sub):
    def issue(n, carry):
        for k in range(TOP_K):
            _tile_row_copy(ys_ref, dest_ref[n * TOP_K + k], buf, (k * tc + n) * sub, sem, sub).start(priority=k % 2)
        return carry

    def drain(n, carry):
        for k in range(TOP_K):
            _tile_row_copy(ys_ref, 0, buf, 0, sem, sub).wait()
        return carry

    lax.fori_loop(0, tc, issue, 0, unroll=DMA_UNROLL)
    lax.fori_loop(0, tc, drain, 0, unroll=DMA_UNROLL)
    gates = [jnp.broadcast_to(gt_ref[:, k:k + 1], (tc, LANES)) for k in range(TOP_K)]
    for c in range(sub):
        cols = slice(c * LANES, (c + 1) * LANES)
        acc = gates[0] * buf[pl.ds(c, tc, stride=sub), :]
        for k in range(1, TOP_K):
            acc = acc + gates[k] * buf[pl.ds(k * tc * sub + c, tc, stride=sub), :]
        o_ref[:, cols] = x_ref[:, cols] + g_ref[:, cols] * acc


def _moe_combine(ys, dest_off, gates, x2, gate2, tc, rows_per_batch):
    n, d = x2.shape
    sub = d // LANES
    tiles_per_batch = rows_per_batch // tc
    return pl.pallas_call(
        functools.partial(_combine_body, tc=tc, sub=sub),
        grid=(n // tc,),
        in_specs=[pl.BlockSpec((tc * TOP_K,), lambda i: (i,), memory_space=pltpu.SMEM),
                  pl.BlockSpec((tc, TOP_K), lambda i: (i, 0)), pl.BlockSpec((tc, d), lambda i: (i, 0)),
                  pl.BlockSpec(memory_space=pl.ANY),
                  pl.BlockSpec((None, 1, d), lambda i: (i // tiles_per_batch, 0, 0))],
        out_specs=pl.BlockSpec((tc, d), lambda i: (i, 0)),
        out_shape=jax.ShapeDtypeStruct((n, d), F32),
        scratch_shapes=[pltpu.VMEM((TOP_K * tc * sub, LANES), F32), pltpu.SemaphoreType.DMA(())],
        compiler_params=_cparams(("arbitrary",)),
        name="moe_combine",
    )(dest_off, gates, x2, ys, gate2)


def _moe_slot_blocks(n_tok):
    return -(-n_tok * TOP_K // MOE_TM) + N_EXPERTS


def _moe_layer(streams, g, router_w, router_b, layer, w_gu, b_gu, w_dn, b_dn, n_blocks, slots=None):
    tm = MOE_TM
    sub = streams[0][0].shape[1] // LANES
    routed = [_moe_route(x2, g, sc, sh, router_w, router_b, tr, rpb) for x2, sc, sh, _, rpb, tr, _ in streams]
    counts = [r[4][0].astype(jnp.int32) for r in routed]
    total = functools.reduce(lambda a, b: a + b, counts)
    padded = (total + tm - 1) // tm * tm
    pad_end = jnp.cumsum(padded)
    assert n_blocks >= _moe_slot_blocks(sum(s[0].shape[0] for s in streams))
    n_used = (pad_end[-1] // tm).astype(jnp.int32)
    blk_ids = jnp.arange(n_blocks, dtype=jnp.int32)
    last_row = jnp.minimum(blk_ids, n_used - 1) * tm
    block_exp = jnp.sum((pad_end[None, :] <= last_row[:, None]).astype(jnp.int32), axis=1)
    block_exp = jnp.minimum(block_exp, N_EXPERTS - 1)
    start = pad_end - padded
    xs, dests = slots, []
    for (x2, _, _, _, _, tr, _), (hx, top_i, _, pos, _), cnt in zip(streams, routed, counts):
        dest_off = ((start[top_i] + pos) * sub).astype(jnp.int32).reshape(-1)
        xs = _moe_dispatch(hx, dest_off, n_blocks * tm, tr, sub, xs)
        dests.append(dest_off)
        start = start + cnt
    ys = _moe_ffn(xs, block_exp, n_used.reshape(1), layer, w_gu, b_gu, w_dn, b_dn, tm)
    outs = [_moe_combine(ys, dest_off, r[2], x2, gate2, tc, rpb)
            for (x2, _, _, gate2, rpb, _, tc), r, dest_off in zip(streams, routed, dests)]
    return outs, ys


def _mod_body(c_ref, w_ref, b_ref, o_ref):
    o_ref[...] = jnp.dot(jax.nn.silu(c_ref[...]), w_ref[...], precision=HI, preferred_element_type=F32) + b_ref[...]


def _modulation(cc, mod_w, mod_b):
    depth, d, d6 = mod_w.shape
    r = cc.shape[0]
    return pl.pallas_call(
        _mod_body,
        grid=(depth, d6 // d),
        in_specs=[pl.BlockSpec((r, d), lambda l, j: (0, 0)), pl.BlockSpec((None, d, d), lambda l, j: (l, 0, j)),
                  pl.BlockSpec((None, 1, d), lambda l, j: (l, 0, j))],
        out_specs=pl.BlockSpec((None, r, d), lambda l, j: (l, 0, j)),
        out_shape=jax.ShapeDtypeStruct((depth, r, d6), F32),
        compiler_params=_cparams(("parallel", "parallel")),
        name="modulation",
    )(cc, mod_w, mod_b.reshape(depth, 1, d6))


def kernel(x, c, ctx, c_ctx, mod_w, mod_b, norm1_g, norm2_g, ab_w_in, ab_w_out, hy_short_w, hy_short_b, hy_f_w1, hy_f_b1, hy_f_w2, hy_f_b2, hy_f_w3, hy_bias, s5_lambda_re, s5_lambda_im, s5_log_dt, s5_b_re, s5_b_im, s5_c_re, s5_c_im, s5_d, s5_glu_w, s5_glu_b, cd_w_in, cd_w_out, ret_decay_logit, swa_sink, router_w, router_b, exp_w_gu, exp_b_gu, exp_w_down, exp_b_down, final_g):
    bsz, seq, d = x.shape
    seq_c = ctx.shape[1]
    depth = mod_w.shape[0]
    m_l, m_c = bsz * seq, bsz * seq_c
    tm, tm_c, tt = 512, 256, 256
    xl = x.reshape(m_l, d)
    xc = ctx.reshape(m_c, d)
    cc = jnp.concatenate([c, c_ctx[None], jnp.zeros((SUBLANES - bsz - 1, d), F32)], axis=0)
    mods = _modulation(cc, mod_w, mod_b)
    hy_w = hy_f_w3.shape[2]
    n_blocks = _moe_slot_blocks(m_l + m_c)
    slots = None
    for layer in range(depth):
        with_ctx = layer < depth - 1
        i = layer // 2
        sh1, sc1, g1, sh2, sc2, g2 = [t[:, None, :] for t in jnp.split(mods[layer, :bsz], 6, axis=-1)]
        csh1, csc1, cg1, csh2, csc2, cg2 = [t[:, None, :] for t in jnp.split(mods[layer, bsz:bsz + 1], 6, axis=-1)]
        if layer % 2 == 0:
            w_in, w_out = ab_w_in[i], ab_w_out[i]
            splits = [(3 * hy_w, 0, None), (w_in.shape[1] - 3 * hy_w, 0, None)]
            pa, pb = _norm_mod_matmul(xl, norm1_g[layer], sc1, sh1, w_in, splits, tm, seq, "ab_in")
            pac, pbc = _norm_mod_matmul(xc, norm1_g[layer], csc1, csh1, w_in, splits, tm_c, m_c, "ab_in_ctx")
            hy = (hy_short_w[i], hy_short_b[i], hy_f_w1[i], hy_f_b1[i], hy_f_w2[i], hy_f_b2[i], hy_f_w3[i], hy_bias[i])
            ya = _hyena(pa, bsz, seq, *hy, tt)
            yb, ybc = _s5_mixer(pb, pbc, bsz, seq, seq_c, s5_lambda_re[i], s5_lambda_im[i], s5_log_dt[i], s5_b_re[i],
                                s5_b_im[i], s5_c_re[i], s5_c_im[i], s5_d[i], s5_glu_w[i], s5_glu_b[i], tm, tm_c)
            ws = [w_out[:hy_w], w_out[hy_w:]]
            xl = _out_proj([ya, yb], ws, xl, g1, tm, seq, "ab_out")
            if with_ctx:
                yac = _hyena(pac, bsz, seq_c, *hy, tt)
                xc = _out_proj([yac, ybc], ws, xc, cg1, tm_c, m_c, "ab_out_ctx")
        else:
            w_in, w_out = cd_w_in[i], cd_w_out[i]
            vw = 2 * RET_HEADS * (d // 16)
            qw = SWA_Q_HEADS * (d // 16)
            lat = _mixer_cd_in(xl, norm1_g[layer], sc1, sh1, w_in, d, seq, tm, seq, "cd_in", True)
            cx = _mixer_cd_in(xc, norm1_g[layer], csc1, csh1, w_in, d, seq_c, tm_c, m_c, "cd_in_ctx", False)
            ro_l, so_l, ro_c, so_c = _mixer_cd_core(lat, cx, bsz, seq, seq_c, ret_decay_logit[i], swa_sink[i],
                                                    with_ctx)
            ws = [w_out[:vw].reshape(RET_HEADS, vw // RET_HEADS, d), w_out[vw:].reshape(SWA_Q_HEADS, qw // SWA_Q_HEADS, d)]
            xl = _out_proj([ro_l, so_l], ws, xl, g1, tm, seq, "cd_out")
            if with_ctx:
                xc = _out_proj([ro_c, so_c], ws, xc, cg1, tm_c, m_c, "cd_out_ctx")
        moe_w = (router_w[layer], router_b[layer], layer, exp_w_gu, exp_b_gu, exp_w_down, exp_b_down)
        streams = [(xl, sc2, sh2, g2, seq, tm, tm_c)]
        if with_ctx:
            streams.append((xc, csc2, csh2, cg2, m_c, tm_c, tm_c))
        outs, slots = _moe_layer(streams, norm2_g[layer], *moe_w, n_blocks, slots)
        xl = outs[0]
        if with_ctx:
            xc = outs[1]
    out = _rows_call(_final_norm_body, [xl], [final_g.reshape(1, -1)], [], [((m_l, d), F32)], tm, m_l, "final_norm")[0]
    return out.reshape(bsz, seq, d)
```

```python
import functools
import math

import numpy as np
import jax
import jax.numpy as jnp
from jax import lax
from jax.experimental import pallas as pl
from jax.experimental.pallas import tpu as pltpu

F32 = jnp.float32
BF16 = jnp.bfloat16
HI = lax.Precision.HIGHEST

EPS = 1e-6
NEG_INF = -1e30
ROPE_BASE = 10000.0
GRID_W = 64

HY_SHORT = 3
HY_BANDS = 16
HY_SHIFT = 0.05
HY_FAST_DECAY = math.log(1e-2) / 0.3
HY_SLOW_DECAY = math.log(1e-2) / 1.5
S5_GROUP = 16
S5_STATE = 64
S5_CHUNK = 8
RET_HEADS = 4
RET_CHUNK = 128
SWA_Q_HEADS = 8
SWA_KV_HEADS = 2
SWA_WINDOW = 128
SWA_BLOCK = 128
N_EXPERTS = 32
TOP_K = 4
SWIGLU_LIMIT = 7.0
SWIGLU_ALPHA = 1.702

LANES = 128
SUBLANES = 8
VMEM_LIMIT = 52 * 2**20
FFT_B = 128
MOE_TM = 512
DMA_UNROLL = 4


def _cparams(sem):
    return pltpu.CompilerParams(dimension_semantics=sem, vmem_limit_bytes=VMEM_LIMIT)


def _rows_call(body, rows, consts, batched, outs, tm, rows_per_batch, name, periodic=()):
    m = rows[0].shape[-2]
    assert m % tm == 0 and rows_per_batch % tm == 0
    tiles_per_batch = rows_per_batch // tm

    def row_spec(shape):
        if len(shape) == 2:
            return pl.BlockSpec((tm, shape[1]), lambda i: (i, 0))
        return pl.BlockSpec((shape[0], tm, shape[2]), lambda i: (0, i, 0))

    in_specs = [row_spec(a.shape) for a in rows]
    for a in periodic:
        in_specs.append(pl.BlockSpec((tm, a.shape[1]), lambda i: (i % tiles_per_batch, 0)))
    for a in consts:
        in_specs.append(pl.BlockSpec(a.shape, lambda i, n=a.ndim: (0,) * n))
    for a in batched:
        in_specs.append(pl.BlockSpec((None, 1, a.shape[2]), lambda i: (i // tiles_per_batch, 0, 0)))
    out_specs = [row_spec(s) for s, _ in outs]
    out_shape = [jax.ShapeDtypeStruct(s, d) for s, d in outs]
    res = pl.pallas_call(
        body,
        grid=(m // tm,),
        in_specs=in_specs,
        out_specs=out_specs,
        out_shape=out_shape,
        compiler_params=_cparams(("parallel",)),
        name=name,
    )(*rows, *periodic, *consts, *batched)
    return res


def _norm_mod(x, g, sc, sh):
    y = x * lax.rsqrt(jnp.mean(x * x, axis=-1, keepdims=True) + EPS) * g
    return y * (1.0 + sc) + sh


def _rotate_half(x, cos, sin, shift):
    w = x.shape[1]
    lane = lax.broadcasted_iota(jnp.int32, x.shape, 1)
    partner = jnp.where(lane % (2 * shift) < shift, pltpu.roll(x, w - shift, axis=1), pltpu.roll(x, shift, axis=1))
    return x * cos + partner * sin


def _norm_mod_matmul_body(x_ref, *refs, splits, n_tab):
    tabs = refs[:2 * n_tab]
    g_ref, w_ref, sc_ref, sh_ref = refs[2 * n_tab:2 * n_tab + 4]
    o_refs = refs[2 * n_tab + 4:]
    h = _norm_mod(x_ref[...], g_ref[...], sc_ref[...], sh_ref[...])
    r = jnp.dot(h.astype(BF16), w_ref[...], preferred_element_type=F32)
    off = 0
    for o_ref, (n, heads, post) in zip(o_refs, splits):
        x = r[:, off:off + n]
        if post is not None:
            table, shift, scale = post
            if table is not None:
                x = _rotate_half(x, tabs[2 * table][...], tabs[2 * table + 1][...], shift)
            if scale != 1.0:
                x = x * scale
        if heads:
            hd = n // heads
            for h_i in range(heads):
                o_ref[h_i] = x[:, h_i * hd:(h_i + 1) * hd]
        else:
            o_ref[...] = x
        off += n


def _norm_mod_matmul(x2, g, sc, sh, w, splits, tm, rows_per_batch, name, tables=()):
    m = x2.shape[0]
    outs = [((heads, m, n // heads), F32) if heads else ((m, n), F32) for n, heads, _ in splits]
    body = functools.partial(_norm_mod_matmul_body, splits=splits, n_tab=len(tables))
    periodic = [t for pair in tables for t in pair]
    return _rows_call(body, [x2], [g.reshape(1, -1), w.astype(BF16)], [sc, sh], outs, tm, rows_per_batch, name,
                      periodic=periodic)


def _out_proj_body(*refs, n_in):
    a_refs = refs[:n_in]
    x_ref = refs[n_in]
    w_refs = refs[n_in + 1: 2 * n_in + 1]
    g_ref = refs[2 * n_in + 1]
    o_ref = refs[2 * n_in + 2]
    acc = None
    for a_ref, w_ref in zip(a_refs, w_refs):
        if a_ref.ndim == 3:
            for h_i in range(a_ref.shape[0]):
                t = jnp.dot(a_ref[h_i].astype(BF16), w_ref[h_i], preferred_element_type=F32)
                acc = t if acc is None else acc + t
        else:
            t = jnp.dot(a_ref[...].astype(BF16), w_ref[...], preferred_element_type=F32)
            acc = t if acc is None else acc + t
    o_ref[...] = x_ref[...] + g_ref[...] * acc


def _out_proj(parts, ws, x2, gate, tm, rows_per_batch, name):
    m, d = x2.shape
    body = functools.partial(_out_proj_body, n_in=len(parts))
    ws = [w.astype(BF16) for w in ws]
    return _rows_call(body, list(parts) + [x2], ws, [gate], [((m, d), F32)], tm, rows_per_batch, name)[0]


def _final_norm_body(x_ref, g_ref, o_ref):
    x = x_ref[...]
    o_ref[...] = x * lax.rsqrt(jnp.mean(x * x, axis=-1, keepdims=True) + EPS) * g_ref[...]


def _hy_prep_body(u_ref, p_ref, n_ref, w_ref, b_ref, x0_ref, z_ref, *, width):
    i = pl.program_id(1)
    last = pl.num_programs(1) - 1
    u = u_ref[...]
    tt = u.shape[0]
    prev_row = jnp.where(i == 0, 0.0, p_ref[SUBLANES - 1:SUBLANES, :])
    next_row = jnp.where(i == last, 0.0, n_ref[0:1, :])
    rows = lax.broadcasted_iota(jnp.int32, (tt, 1), 0)
    up = jnp.where(rows == 0, prev_row, pltpu.roll(u, 1, axis=0))
    dn = jnp.where(rows == tt - 1, next_row, pltpu.roll(u, tt - 1, axis=0))
    y = w_ref[0:1, :] * up + w_ref[1:2, :] * u + w_ref[2:3, :] * dn + b_ref[...]
    x0_ref[...] = y[:, :width]
    z_ref[...] = y[:, 2 * width:] * y[:, width:2 * width]


def _hyena_prep(p, short_w, short_b, bsz, seq, tt):
    w3 = p.shape[1]
    width = w3 // 3
    p3 = p.reshape(bsz, seq, w3)
    nt = seq // tt
    sub = tt // SUBLANES
    nsub = seq // SUBLANES
    body = functools.partial(_hy_prep_body, width=width)
    x0, z = pl.pallas_call(
        body,
        grid=(bsz, nt),
        in_specs=[
            pl.BlockSpec((None, tt, w3), lambda b, i: (b, i, 0)),
            pl.BlockSpec((None, SUBLANES, w3), lambda b, i: (b, jnp.maximum(i * sub - 1, 0), 0)),
            pl.BlockSpec((None, SUBLANES, w3), lambda b, i: (b, jnp.minimum((i + 1) * sub, nsub - 1), 0)),
            pl.BlockSpec((HY_SHORT, w3), lambda b, i: (0, 0)),
            pl.BlockSpec((1, w3), lambda b, i: (0, 0)),
        ],
        out_specs=[pl.BlockSpec((None, tt, width), lambda b, i: (b, i, 0))] * 2,
        out_shape=[jax.ShapeDtypeStruct((bsz, seq, width), F32)] * 2,
        compiler_params=_cparams(("parallel", "parallel")),
        name="hyena_prep",
    )(p3, p3, p3, short_w, short_b.reshape(1, -1))
    return x0, z


def _hy_filter_body(t_ref, w_ref, lag_ref, bands_ref, deltas_ref, w1t_ref, w1c_ref, w1s_ref, b1_ref,
                    w2_ref, b2_ref, w3_ref, h_ref, s_ref):
    i = pl.program_id(0)
    arg = w_ref[...] * bands_ref[...]
    pre = (t_ref[...] * w1t_ref[...]
           + jnp.dot(jnp.cos(arg), w1c_ref[...], precision=HI, preferred_element_type=F32)
           + jnp.dot(-jnp.sin(arg), w1s_ref[...], precision=HI, preferred_element_type=F32)
           + b1_ref[...])
    h1 = jnp.sin(pre)
    h2 = jnp.sin(jnp.dot(h1, w2_ref[...], precision=HI, preferred_element_type=F32) + b2_ref[...])
    h3 = jnp.dot(h2, w3_ref[...], precision=HI, preferred_element_type=F32)
    h = h3 * (jnp.exp(-lag_ref[...] * deltas_ref[...]) + HY_SHIFT)
    h_ref[...] = h

    @pl.when(i == 0)
    def _():
        s_ref[...] = jnp.zeros_like(s_ref)

    s_ref[...] += jnp.sum(jnp.abs(h), axis=0, keepdims=True)


def _hyena_filter(seq, w1, b1, w2, b2, w3):
    width = w3.shape[1]
    pos = jnp.arange(seq, dtype=F32)
    t = (pos / seq)[:, None]
    w = (2.0 * math.pi * pos / seq)[:, None]
    lag = (jnp.abs(pos - seq // 2) / (seq / 2))[:, None]
    bands = jnp.linspace(1e-4, HY_BANDS - 1, HY_BANDS, dtype=F32)[None]
    deltas = jnp.abs(jnp.linspace(HY_FAST_DECAY, HY_SLOW_DECAY, width, dtype=F32))[None]
    tl = min(seq, 1024)
    col = pl.BlockSpec((tl, 1), lambda i: (i, 0))

    def whole(a):
        return pl.BlockSpec(a.shape, lambda i, n=a.ndim: (0,) * n)

    consts = [bands, deltas, w1[0:1], w1[1:1 + HY_BANDS], w1[1 + HY_BANDS:], b1.reshape(1, -1),
              w2, b2.reshape(1, -1), w3]
    h, s = pl.pallas_call(
        _hy_filter_body,
        grid=(seq // tl,),
        in_specs=[col, col, col] + [whole(a) for a in consts],
        out_specs=[pl.BlockSpec((tl, width), lambda i: (i, 0)), pl.BlockSpec((1, width), lambda i: (0, 0))],
        out_shape=[jax.ShapeDtypeStruct((seq, width), F32), jax.ShapeDtypeStruct((1, width), F32)],
        compiler_params=_cparams(("arbitrary",)),
        name="hyena_filter",
    )(t, w, lag, *consts)
    return h, s


FFT_G = SUBLANES
FFT_CW = 512


def _kron_eye(mat):
    return np.kron(mat, np.eye(FFT_G))


def _dft_tables_k(na, ka, a_lo, a_cnt):
    n = na * FFT_B
    a = np.arange(na)
    b = np.arange(FFT_B)
    ang1 = 2.0 * np.pi * np.outer(a, a) / na
    ang2 = 2.0 * np.pi * np.outer(b, b) / FFT_B
    angt = 2.0 * np.pi * np.outer(b, a) / n
    c1, s1 = np.cos(ang1), np.sin(ang1)
    c2, s2 = np.cos(ang2), np.sin(ang2)
    nbb = FFT_B // FFT_G
    tw1 = angt.reshape(nbb, FFT_G, na).transpose(0, 2, 1).reshape(nbb, na * FFT_G, 1)
    rows = slice(a_lo, a_lo + a_cnt)
    tabs = dict(
        m1=_kron_eye(np.concatenate([c1, -s1], axis=0)[:, :ka]),
        f2=np.block([[c2, s2], [-s2, c2]]),
        f2i=np.block([[c2, -s2], [s2, c2]]),
        m3r=_kron_eye(c1[rows] / n), m3i=_kron_eye(-s1[rows] / n),
        tw1c=np.cos(tw1), tw1s=np.sin(tw1),
        twc_c=np.cos(angt).T[:, :, None], tws_c=np.sin(angt).T[:, :, None],
    )
    return {k: jnp.asarray(v, F32) for k, v in tabs.items()}


def _fftk1_body(x_ref, m_ref, tc_ref, ts_ref, sc_ref, o_ref, *, na):
    ka, g, cw = x_ref.shape
    v = (x_ref[...].reshape(ka * g, cw) * sc_ref[...]).astype(BF16)
    r = jnp.dot(m_ref[...], v, preferred_element_type=F32)
    gr, gi = r[:na * g], r[na * g:]
    tc, ts = tc_ref[...], ts_ref[...]
    o_ref[...] = jnp.concatenate([gr * tc + gi * ts, gi * tc - gr * ts], axis=0).reshape(2 * na, g, cw)


def _fftk_stage1(x5, scale, tabs, na):
    bz, ka, nbb, g, ch = x5.shape
    cw = min(ch, FFT_CW)
    m1 = tabs["m1"].astype(BF16)
    return pl.pallas_call(
        functools.partial(_fftk1_body, na=na),
        grid=(bz, ch // cw, nbb),
        in_specs=[
            pl.BlockSpec((None, ka, None, g, cw), lambda z, q, b: (z, 0, b, 0, q)),
            pl.BlockSpec(m1.shape, lambda z, q, b: (0, 0)),
            pl.BlockSpec((None, na * g, 1), lambda z, q, b: (b, 0, 0)),
            pl.BlockSpec((None, na * g, 1), lambda z, q, b: (b, 0, 0)),
            pl.BlockSpec((1, cw), lambda z, q, b: (0, q)),
        ],
        out_specs=pl.BlockSpec((None, 2 * na, None, g, cw), lambda z, q, b: (z, 0, b, 0, q)),
        out_shape=jax.ShapeDtypeStruct((bz, 2 * na, nbb, g, ch), F32),
        compiler_params=_cparams(("parallel", "parallel", "parallel")),
        name="fft_stage1",
    )(x5, m1, tabs["tw1c"], tabs["tw1s"], scale)


def _fftk2_spec_body(ar_ref, ai_ref, f_ref, o_ref):
    f = f_ref[...].astype(BF16)
    cw = ar_ref.shape[-1]
    for j in range(FFT_G):
        v = jnp.concatenate([ar_ref[j].reshape(FFT_B, cw), ai_ref[j].reshape(FFT_B, cw)], axis=0).astype(BF16)
        o_ref[j] = jnp.dot(f, v, preferred_element_type=F32)


def _fftk2_conv_body(ar_ref, ai_ref, h_ref, f_ref, fi_ref, tc_ref, ts_ref, o_ref):
    f = f_ref[...].astype(BF16)
    fi = fi_ref[...].astype(BF16)
    cw = ar_ref.shape[-1]
    nbb = FFT_B // FFT_G
    for j in range(FFT_G):
        v = jnp.concatenate([ar_ref[j].reshape(FFT_B, cw), ai_ref[j].reshape(FFT_B, cw)], axis=0).astype(BF16)
        x = jnp.dot(f, v, preferred_element_type=F32)
        xr, xi = x[:FFT_B], x[FFT_B:]
        hr, hi = h_ref[j, :FFT_B, :], h_ref[j, FFT_B:, :]
        p = jnp.concatenate([xr * hr - xi * hi, xr * hi + xi * hr], axis=0).astype(BF16)
        q = jnp.dot(fi, p, preferred_element_type=F32)
        qr, qi = q[:FFT_B], q[FFT_B:]
        tc, ts = tc_ref[j], ts_ref[j]
        o_ref[j, 0] = (qr * tc - qi * ts).reshape(nbb, FFT_G, cw)
        o_ref[j, 1] = (qi * tc + qr * ts).reshape(nbb, FFT_G, cw)


def _fftk_stage2_spectrum(a5, tabs, na):
    _, _, nbb, g, ch = a5.shape
    cw = min(ch, FFT_CW)
    ng = na // FFT_G
    return pl.pallas_call(
        _fftk2_spec_body,
        grid=(ch // cw, ng),
        in_specs=[
            pl.BlockSpec((None, FFT_G, nbb, g, cw), lambda q, c: (0, c, 0, 0, q)),
            pl.BlockSpec((None, FFT_G, nbb, g, cw), lambda q, c: (0, ng + c, 0, 0, q)),
            pl.BlockSpec((2 * FFT_B, 2 * FFT_B), lambda q, c: (0, 0)),
        ],
        out_specs=pl.BlockSpec((FFT_G, 2 * FFT_B, cw), lambda q, c: (c, 0, q)),
        out_shape=jax.ShapeDtypeStruct((na, 2 * FFT_B, ch), F32),
        compiler_params=_cparams(("parallel", "parallel")),
        name="fft_stage2_spectrum",
    )(a5, a5, tabs["f2"])


def _fftk_stage2_conv(a5, hspec, tabs, na):
    bz, _, nbb, g, ch = a5.shape
    cw = min(ch, FFT_CW)
    ng = na // FFT_G
    return pl.pallas_call(
        _fftk2_conv_body,
        grid=(ch // cw, ng, bz),
        in_specs=[
            pl.BlockSpec((None, FFT_G, nbb, g, cw), lambda q, c, z: (z, c, 0, 0, q)),
            pl.BlockSpec((None, FFT_G, nbb, g, cw), lambda q, c, z: (z, ng + c, 0, 0, q)),
            pl.BlockSpec((FFT_G, 2 * FFT_B, cw), lambda q, c, z: (c, 0, q)),
            pl.BlockSpec((2 * FFT_B, 2 * FFT_B), lambda q, c, z: (0, 0)),
            pl.BlockSpec((2 * FFT_B, 2 * FFT_B), lambda q, c, z: (0, 0)),
            pl.BlockSpec((FFT_G, FFT_B, 1), lambda q, c, z: (c, 0, 0)),
            pl.BlockSpec((FFT_G, FFT_B, 1), lambda q, c, z: (c, 0, 0)),
        ],
        out_specs=pl.BlockSpec((None, None, FFT_G, 2, nbb, g, cw), lambda q, c, z: (z, c, 0, 0, 0, 0, q)),
        out_shape=jax.ShapeDtypeStruct((bz, ng, FFT_G, 2, nbb, g, ch), F32),
        compiler_params=_cparams(("parallel", "parallel", "parallel")),
        name="fft_stage2_conv",
    )(a5, a5, hspec, tabs["f2"], tabs["f2i"], tabs["twc_c"], tabs["tws_c"])


def _fftk3_body(*refs, gate):
    if gate:
        br_ref, bi_ref, mr_ref, mi_ref, x0_ref, z_ref, bias_ref, o_ref = refs
    else:
        br_ref, bi_ref, mr_ref, mi_ref, o_ref = refs
    ng, gc, g, cw = br_ref.shape
    vr = br_ref[...].reshape(ng * gc * g, cw).astype(BF16)
    vi = bi_ref[...].reshape(ng * gc * g, cw).astype(BF16)
    y = jnp.dot(mr_ref[...], vr, preferred_element_type=F32) + jnp.dot(mi_ref[...], vi, preferred_element_type=F32)
    a_cnt = o_ref.shape[0]
    if gate:
        x0 = x0_ref[...].reshape(a_cnt * g, cw)
        z = z_ref[...].reshape(a_cnt * g, cw)
        y = x0 * (y + bias_ref[...] * z)
    o_ref[...] = y.reshape(a_cnt, g, cw)


def _fftk_stage3(b7, tabs, na, a_cnt, gate_args=None):
    bz, ng, gc, _, nbb, g, ch = b7.shape
    cw = min(ch, FFT_CW)
    mr, mi = tabs["m3r"].astype(BF16), tabs["m3i"].astype(BF16)
    row_spec = pl.BlockSpec((None, a_cnt, None, g, cw), lambda z, q, b: (z, 0, b, 0, q))
    in_specs = [
        pl.BlockSpec((None, ng, gc, None, None, g, cw), lambda z, q, b: (z, 0, 0, 0, b, 0, q)),
        pl.BlockSpec((None, ng, gc, None, None, g, cw), lambda z, q, b: (z, 0, 0, 1, b, 0, q)),
        pl.BlockSpec(mr.shape, lambda z, q, b: (0, 0)),
        pl.BlockSpec(mi.shape, lambda z, q, b: (0, 0)),
    ]
    args = [b7, b7, mr, mi]
    if gate_args is not None:
        in_specs += [row_spec, row_spec, pl.BlockSpec((1, cw), lambda z, q, b: (0, q))]
        args += list(gate_args)
    return pl.pallas_call(
        functools.partial(_fftk3_body, gate=gate_args is not None),
        grid=(bz, ch // cw, nbb),
        in_specs=in_specs,
        out_specs=row_spec,
        out_shape=jax.ShapeDtypeStruct((bz, a_cnt, nbb, g, ch), F32),
        compiler_params=_cparams(("parallel", "parallel", "parallel")),
        name="fft_stage3",
    )(*args)


def _hy_gate_body(x0_ref, y_ref, z_ref, b_ref, o_ref):
    o_ref[...] = x0_ref[...] * (y_ref[...] + b_ref[...] * z_ref[...])


def _hyena(p, bsz, seq, short_w, short_b, w1, b1, w2, b2, w3, hy_bias, tt):
    width = w3.shape[1]
    x0, z = _hyena_prep(p, short_w, short_b, bsz, seq, tt)
    hu, hs = _hyena_filter(seq, w1, b1, w2, b2, w3)
    na = max(2 * seq // FFT_B, 16)
    ka = max(seq // FFT_B, 16)
    pad = ka * FFT_B - seq
    a_lo, a_cnt = (0, na) if pad else (seq // 2 // FFT_B, ka)
    tabs = _dft_tables_k(na, ka, a_lo, a_cnt)
    nbb = FFT_B // FFT_G

    def rows5(a, lead):
        if pad:
            a = jnp.pad(a, ((0, 0), (0, pad), (0, 0)))
        return a.reshape(lead, ka, nbb, FFT_G, width)

    ones = jnp.ones((1, width), F32)
    bias = hy_bias.reshape(1, -1)
    hspec = _fftk_stage2_spectrum(_fftk_stage1(rows5(hu[None], 1), 1.0 / hs, tabs, na), tabs, na)
    a5 = _fftk_stage1(rows5(z, bsz), ones, tabs, na)
    b7 = _fftk_stage2_conv(a5, hspec, tabs, na)
    m = bsz * seq
    if not pad:
        out = _fftk_stage3(b7, tabs, na, a_cnt, (rows5(x0, bsz), rows5(z, bsz), bias))
        return out.reshape(m, width)
    y = _fftk_stage3(b7, tabs, na, a_cnt).reshape(bsz, na * FFT_B, width)[:, seq // 2: seq // 2 + seq]
    return _rows_call(_hy_gate_body, [x0.reshape(m, width), y.reshape(m, width), z.reshape(m, width)],
                      [bias], [], [((m, width), F32)], tt, seq, "hyena_gate")[0]


def _s5_tables(lam_re, lam_im, log_dt, b_re, b_im, c_re, c_im, nsteps):
    t_len, hdim = S5_CHUNK, S5_GROUP
    lam = lax.complex(jnp.minimum(lam_re.astype(F32), -1e-4), lam_im.astype(F32))
    dt = jnp.exp(log_dt.astype(F32))[..., None]
    lam_dt = lam * dt
    lam_bar = jnp.exp(lam_dt)
    b_bar = ((lam_bar - 1.0) / lam)[..., None] * lax.complex(b_re.astype(F32), b_im.astype(F32))
    cm = lax.complex(c_re.astype(F32), c_im.astype(F32))
    ks = jnp.arange(t_len + 1, dtype=F32)
    pw = jnp.exp(ks[:, None, None, None] * lam_dt[None])
    g = lam.shape[1]
    tabs = {}
    pin = [pw[:t_len, 0][::-1], pw[:t_len, 1]]
    pout = [pw[1:, 0], pw[1:, 1][::-1]]
    for d in range(2):
        win = pin[d][:, :, :, None] * b_bar[d][None]
        win = jnp.transpose(win, (1, 0, 3, 2)).reshape(g, t_len * hdim, -1)
        tabs[f"win{d}"] = jnp.concatenate([jnp.real(win), jnp.imag(win)], axis=-1)
        wout = pout[d][:, :, None, :] * cm[d][None]
        wout = jnp.transpose(wout, (1, 3, 0, 2)).reshape(g, -1, t_len * hdim)
        tabs[f"wout{d}"] = jnp.concatenate([jnp.real(wout), -jnp.imag(wout)], axis=1)
        mu = jnp.exp((t_len * 2.0 ** jnp.arange(nsteps, dtype=F32))[:, None, None] * lam_dt[d][None])
        mr, mi = jnp.real(mu), jnp.imag(mu)
        tabs[f"m1{d}"] = jnp.concatenate([mr, mr], axis=-1)[:, :, None, :]
        tabs[f"m2{d}"] = jnp.concatenate([-mi, mi], axis=-1)[:, :, None, :]
        mu1 = jnp.exp(t_len * lam_dt[d])
        tabs[f"mu1{d}"] = jnp.concatenate([jnp.real(mu1), jnp.real(mu1)], axis=-1)[:, None, :]
        tabs[f"mu2{d}"] = jnp.concatenate([-jnp.imag(mu1), jnp.imag(mu1)], axis=-1)[:, None, :]
    kern = [jnp.real(jnp.einsum("ghp,tgp,gpk->tghk", cm[d], pw[:t_len, d], b_bar[d], precision=HI)) for d in range(2)]
    s_idx = jnp.arange(t_len)[:, None]
    t_idx = jnp.arange(t_len)[None, :]
    fwd = jnp.where((t_idx >= s_idx)[:, :, None, None, None], kern[0][jnp.maximum(t_idx - s_idx, 0)], 0.0)
    bwd = jnp.where((s_idx >= t_idx)[:, :, None, None, None], kern[1][jnp.maximum(s_idx - t_idx, 0)], 0.0)
    d0 = fwd + bwd
    tabs["d0"] = jnp.transpose(d0, (2, 0, 4, 1, 3)).reshape(g, t_len * hdim, t_len * hdim)
    return _s5_pair_tables(tabs)


def _s5_pair_tables(tabs):
    half = S5_STATE

    def block_diag(a):
        g, r, c = a.shape
        a = a.reshape(g // 2, 2, r, c)
        z = jnp.zeros_like(a[:, 0])
        return jnp.concatenate([jnp.concatenate([a[:, 0], z], -1), jnp.concatenate([z, a[:, 1]], -1)], axis=-2)

    def lanes(a):
        return jnp.concatenate([a[..., 0::2, :, :], a[..., 1::2, :, :]], axis=-1)

    out = {"d0": block_diag(tabs["d0"])}
    for d in range(2):
        win, wout = tabs[f"win{d}"], tabs[f"wout{d}"]
        out[f"winr{d}"], out[f"wini{d}"] = block_diag(win[..., :half]), block_diag(win[..., half:])
        out[f"woutr{d}"], out[f"wouti{d}"] = block_diag(wout[:, :half, :]), block_diag(wout[:, half:, :])
        out[f"mr{d}"], out[f"mi{d}"] = lanes(tabs[f"m1{d}"][..., :half]), lanes(tabs[f"m2{d}"][..., half:])
        out[f"mur{d}"], out[f"mui{d}"] = lanes(tabs[f"mu1{d}"][..., :half]), lanes(tabs[f"mu2{d}"][..., half:])
    return out


S5_GPB = LANES // S5_GROUP


def _s5_perm_table():
    p = np.zeros((S5_CHUNK, S5_GPB, LANES, LANES), np.float32)
    for t in range(S5_CHUNK):
        for g in range(S5_GPB):
            for h in range(S5_GROUP):
                p[t, g, S5_GROUP * g + h, S5_GROUP * t + h] = 1.0
    return jnp.asarray(p)


S5_TABLES = ("d0", "winr0", "wini0", "winr1", "wini1", "woutr0", "wouti0", "woutr1", "wouti1")
S5_STEP_TABLES = ("mr0", "mi0", "mr1", "mi1")
S5_LAST_TABLES = ("mur0", "mui0", "mur1", "mui1")


def _s5_body(u_ref, perm_ref, permt_ref, *refs, nc, nsteps):
    nt, ns = len(S5_TABLES), len(S5_STEP_TABLES)
    tab = dict(zip(S5_TABLES + S5_STEP_TABLES + S5_LAST_TABLES, refs))
    s0_ref, s1_ref, y_ref, f0_ref, f1_ref = refs[nt + ns + len(S5_LAST_TABLES):]
    t_len = S5_CHUNK
    j = lax.broadcasted_iota(jnp.int32, (nc, 1), 0)

    def mm(a, name, p):
        return jnp.dot(a.astype(BF16), tab[name][p].astype(BF16), preferred_element_type=F32)

    def scan(e_re, e_im, init_ref, p, d):
        first = (j == 0) if d == 0 else (j == nc - 1)
        one = 1 if d == 0 else nc - 1
        s_re = jnp.where(first, init_ref[p, 0:1, :], pltpu.roll(e_re, one, axis=0))
        s_im = jnp.where(first, init_ref[p, 1:2, :], pltpu.roll(e_im, one, axis=0))
        for k in range(nsteps):
            step = 2 ** k
            keep = (j >= step) if d == 0 else (j < nc - step)
            shift = step if d == 0 else nc - step
            sh_re = jnp.where(keep, pltpu.roll(s_re, shift, axis=0), 0.0)
            sh_im = jnp.where(keep, pltpu.roll(s_im, shift, axis=0), 0.0)
            mr, mi = tab[f"mr{d}"][k, p], tab[f"mi{d}"][k, p]
            s_re, s_im = s_re + mr * sh_re - mi * sh_im, s_im + mr * sh_im + mi * sh_re
        return s_re, s_im

    xs = [u_ref[pl.ds(t, nc, stride=t_len), :].astype(BF16) for t in range(t_len)]
    ys = []
    for p in range(S5_GPB // 2):
        us = []
        for g in (2 * p, 2 * p + 1):
            u = None
            for t in range(t_len):
                part = jnp.dot(xs[t], perm_ref[t, g].astype(BF16), preferred_element_type=F32)
                u = part if u is None else u + part
            us.append(u.astype(BF16))
        u = jnp.concatenate(us, axis=1)
        y = mm(u, "d0", p)
        for d, (init_ref, fin_ref) in enumerate(((s0_ref, f0_ref), (s1_ref, f1_ref))):
            e_re, e_im = mm(u, f"winr{d}", p), mm(u, f"wini{d}", p)
            s_re, s_im = scan(e_re, e_im, init_ref, p, d)
            y = y + mm(s_re, f"woutr{d}", p) + mm(s_im, f"wouti{d}", p)
            mur, mui = tab[f"mur{d}"][p], tab[f"mui{d}"][p]
            last = slice(nc - 1, nc) if d == 0 else slice(0, 1)
            fin_ref[p, 0:1, :] = (mur * s_re - mui * s_im + e_re)[last, :]
            fin_ref[p, 1:2, :] = (mur * s_im + mui * s_re + e_im)[last, :]
        for y_g in (y[:, :LANES], y[:, LANES:]):
            y_hi = y_g.astype(BF16)
            ys.append((y_hi, (y_g - y_hi.astype(F32)).astype(BF16)))
    for t in range(t_len):
        out = None
        for g in range(S5_GPB):
            pt = permt_ref[t, g].astype(BF16)
            part = jnp.dot(ys[g][0], pt, preferred_element_type=F32) + jnp.dot(ys[g][1], pt, preferred_element_type=F32)
            out = part if out is None else out + part
        y_ref[pl.ds(t, nc, stride=t_len), :] = out


def _s5_core(u2, bsz, seq, tabs, init0, init1):
    m, width = u2.shape
    pairs = width // S5_GROUP // 2
    ppb = S5_GPB // 2
    nq = width // LANES
    t_len = S5_CHUNK
    nc = seq // t_len
    nsteps = max(1, math.ceil(math.log2(nc)))
    assert nsteps <= tabs["mr0"].shape[0] and t_len * S5_GROUP == LANES
    perm = _s5_perm_table()
    permt = jnp.swapaxes(perm, 2, 3)

    def whole(a):
        return pl.BlockSpec(a.shape, lambda q, b, nd=a.ndim: (0,) * nd)

    def per_q(a):
        return pl.BlockSpec((ppb,) + a.shape[1:], lambda q, b, nd=a.ndim: (q,) + (0,) * (nd - 1))

    def per_q_steps(a):
        return pl.BlockSpec((nsteps, ppb) + a.shape[2:], lambda q, b: (0, q, 0, 0))

    p2 = 2 * S5_STATE
    state_spec = pl.BlockSpec((None, ppb, 2, p2), lambda q, b: (b, q, 0, 0))
    seq_spec = pl.BlockSpec((seq, LANES), lambda q, b: (b, q))
    mats = [tabs[k] for k in S5_TABLES]
    steps = [tabs[k][:nsteps] for k in S5_STEP_TABLES]
    lasts = [tabs[k] for k in S5_LAST_TABLES]
    in_specs = [seq_spec, whole(perm), whole(permt)] + [per_q(a) for a in mats] + [per_q_steps(a) for a in steps] \
        + [per_q(a) for a in lasts] + [state_spec] * 2
    state_shape = jax.ShapeDtypeStruct((bsz, pairs, 2, p2), F32)
    y, f0, f1 = pl.pallas_call(
        functools.partial(_s5_body, nc=nc, nsteps=nsteps),
        grid=(nq, bsz),
        in_specs=in_specs,
        out_specs=[seq_spec, state_spec, state_spec],
        out_shape=[jax.ShapeDtypeStruct((m, width), F32), state_shape, state_shape],
        compiler_params=_cparams(("parallel", "parallel")),
        name="s5_scan",
    )(u2, perm, permt, *mats, *steps, *lasts, init0, init1)
    return y, f0, f1


def _s5_glu_body(u_ref, y_ref, d_ref, w_ref, b_ref, o_ref):
    y = d_ref[...] * u_ref[...] + y_ref[...]
    g = jax.nn.gelu(y)
    o_ref[...] = g * jax.nn.sigmoid(jnp.dot(g.astype(BF16), w_ref[...], preferred_element_type=F32) + b_ref[...])


def _s5_glu(u2, y2, d_skip, glu_w, glu_b, tm, name):
    m, width = u2.shape
    return _rows_call(_s5_glu_body, [u2, y2], [d_skip.reshape(1, -1), glu_w.astype(BF16), glu_b.reshape(1, -1)], [],
                      [((m, width), F32)], tm, m, name)[0]


def _s5_mixer(u_lat, u_ctx, bsz, seq, seq_c, lam_re, lam_im, log_dt, b_re, b_im, c_re, c_im, d_skip, glu_w, glu_b,
              tm, tm_c):
    nsteps = max(1, math.ceil(math.log2(seq // S5_CHUNK)))
    tabs = _s5_tables(lam_re, lam_im, log_dt, b_re, b_im, c_re, c_im, nsteps)
    g = u_lat.shape[1] // S5_GROUP
    zero = jnp.zeros((bsz, g // 2, 2, 2 * S5_STATE), F32)
    y_ctx, f0, f1 = _s5_core(u_ctx, bsz, seq_c, tabs, zero, zero)
    y_lat, _, _ = _s5_core(u_lat, bsz, seq, tabs, f0, f1)
    out_lat = _s5_glu(u_lat, y_lat, d_skip, glu_w, glu_b, tm, "s5_glu")
    out_ctx = _s5_glu(u_ctx, y_ctx, d_skip, glu_w, glu_b, tm_c, "s5_glu_ctx")
    return out_lat, out_ctx


def _rope_tables_1d(seq, hd, heads):
    half = hd // 2
    inv = ROPE_BASE ** (-jnp.arange(half, dtype=F32) / half)
    ang = jnp.arange(seq, dtype=F32)[:, None] * inv[None]
    cos, sin = jnp.cos(ang), jnp.sin(ang)
    return jnp.tile(jnp.concatenate([cos, cos], -1), (1, heads)), jnp.tile(jnp.concatenate([-sin, sin], -1), (1, heads))


def _rope_tables_2d(seq, hd, heads):
    q = hd // 4
    inv = ROPE_BASE ** (-jnp.arange(q, dtype=F32) / q)
    n_rows = seq // GRID_W
    row = jnp.repeat(jnp.arange(n_rows, dtype=F32), GRID_W)
    col = jnp.tile(jnp.arange(GRID_W, dtype=F32), n_rows)
    ar, ac = row[:, None] * inv[None], col[:, None] * inv[None]
    cos = jnp.concatenate([jnp.cos(ar), jnp.cos(ar), jnp.cos(ac), jnp.cos(ac)], -1)
    sin = jnp.concatenate([-jnp.sin(ar), jnp.sin(ar), -jnp.sin(ac), jnp.sin(ac)], -1)
    return jnp.tile(cos, (1, heads)), jnp.tile(sin, (1, heads))


def _kv_update(k, kdec, v):
    kd = (k * kdec).astype(BF16)
    return lax.dot_general(kd, v.astype(BF16), (((0,), (0,)), ((), ())), preferred_element_type=F32)


def _ret_bwd_body(k_ref, v_ref, kdec_ref, gc_ref, init_ref, sprev_ref, fin_ref, s_ref):
    @pl.when(pl.program_id(1) == 0)
    def _():
        s_ref[...] = init_ref[...]

    for h in range(k_ref.shape[0]):
        s = s_ref[h]
        sprev_ref[h] = s
        s = gc_ref[h] * s + _kv_update(k_ref[h], kdec_ref[h], v_ref[h])
        s_ref[h] = s
        fin_ref[h] = s


def _ret_main_body(q_ref, k_ref, v_ref, g_ref, mask_ref, qdf_ref, kdf_ref, qdb_ref, gc_ref, init_ref, sb_ref,
                   o_ref, fin_ref, s_ref):
    @pl.when(pl.program_id(1) == 0)
    def _():
        s_ref[...] = init_ref[...]

    for h in range(q_ref.shape[0]):
        q, k, v = q_ref[h], k_ref[h], v_ref[h]
        sc = lax.dot_general(q.astype(BF16), k.astype(BF16), (((1,), (1,)), ((), ())), preferred_element_type=F32)
        s = s_ref[h]
        lhs = jnp.concatenate([sc * mask_ref[h], q * qdf_ref[h], q * qdb_ref[h]], axis=1).astype(BF16)
        rhs = jnp.concatenate([v, s, sb_ref[h]], axis=0).astype(BF16)
        o = jnp.dot(lhs, rhs, preferred_element_type=F32)
        o = o * lax.rsqrt(jnp.mean(o * o, axis=-1, keepdims=True) + EPS)
        o_ref[h] = o * jax.nn.silu(g_ref[h])
        s = gc_ref[h] * s + _kv_update(k, kdf_ref[h], v)
        s_ref[h] = s
        fin_ref[h] = s


def _ret_tables(decay_logit):
    cl = RET_CHUNK
    log_g = jax.nn.log_sigmoid(decay_logit.astype(F32))
    idx = jnp.arange(cl, dtype=F32)
    diff = idx[:, None] - idx[None, :]
    mf = jnp.where(diff[None] >= 0, jnp.exp(jnp.maximum(diff, 0.0)[None] * log_g[0][:, None, None]), 0.0)
    mb = jnp.where(diff[None] <= 0, jnp.exp(jnp.maximum(-diff, 0.0)[None] * log_g[1][:, None, None]), 0.0)

    def col(e, d):
        return jnp.exp(e[None, :] * log_g[d][:, None])[:, :, None]

    return dict(mask=mf + mb, qdf=col(idx + 1.0, 0), kdf=col(cl - 1.0 - idx, 0), qdb=col(cl - idx, 1), kdb=col(idx, 1),
                gcf=jnp.exp(cl * log_g[0])[:, None, None], gcb=jnp.exp(cl * log_g[1])[:, None, None])


def _retention(q, k, v, g, bsz, seq, tabs, init_f, init_b):
    heads, _, dk = q.shape
    dv = v.shape[2]
    cl = RET_CHUNK
    n = seq // cl

    def whole(a):
        return pl.BlockSpec(a.shape, lambda b, i, nd=a.ndim: (0,) * nd)

    state_spec = pl.BlockSpec((None, heads, dk, dv), lambda b, i: (b, 0, 0, 0))
    state_shape = jax.ShapeDtypeStruct((bsz, heads, dk, dv), F32)
    rev = lambda b, i: (0, b * n + (n - 1 - i), 0)
    sprev_b, fin_b = pl.pallas_call(
        _ret_bwd_body,
        grid=(bsz, n),
        in_specs=[pl.BlockSpec((heads, cl, dk), rev), pl.BlockSpec((heads, cl, dv), rev),
                  whole(tabs["kdb"]), whole(tabs["gcb"]), state_spec],
        out_specs=[pl.BlockSpec((None, None, heads, dk, dv), lambda b, i: (b, n - 1 - i, 0, 0, 0)), state_spec],
        out_shape=[jax.ShapeDtypeStruct((bsz, n, heads, dk, dv), F32), state_shape],
        scratch_shapes=[pltpu.VMEM((heads, dk, dv), F32)],
        compiler_params=_cparams(("parallel", "arbitrary")),
        name="retention_backward_states",
    )(k, v, tabs["kdb"], tabs["gcb"], init_b)
    fwd = lambda b, i: (0, b * n + i, 0)
    o, fin_f = pl.pallas_call(
        _ret_main_body,
        grid=(bsz, n),
        in_specs=[pl.BlockSpec((heads, cl, dk), fwd), pl.BlockSpec((heads, cl, dk), fwd),
                  pl.BlockSpec((heads, cl, dv), fwd), pl.BlockSpec((heads, cl, dv), fwd),
                  whole(tabs["mask"]), whole(tabs["qdf"]), whole(tabs["kdf"]), whole(tabs["qdb"]), whole(tabs["gcf"]),
                  state_spec, pl.BlockSpec((None, None, heads, dk, dv), lambda b, i: (b, i, 0, 0, 0))],
        out_specs=[pl.BlockSpec((heads, cl, dv), fwd), state_spec],
        out_shape=[jax.ShapeDtypeStruct((heads, bsz * seq, dv), F32), state_shape],
        scratch_shapes=[pltpu.VMEM((heads, dk, dv), F32)],
        compiler_params=_cparams(("parallel", "arbitrary")),
        name="retention",
    )(q, k, v, g, tabs["mask"], tabs["qdf"], tabs["kdf"], tabs["qdb"], tabs["gcf"], init_f, sprev_b)
    return o, fin_f, fin_b


def _swa_body(*refs, local, seq, scale):
    if local:
        q_ref, kp_ref, kc_ref, kn_ref, vp_ref, vc_ref, vn_ref, kx_ref, vx_ref, sink_ref, o_ref = refs
    else:
        q_ref, kx_ref, vx_ref, sink_ref, o_ref = refs
    i = pl.program_id(1)
    bk = q_ref.shape[1]
    nkv = kx_ref.shape[0]
    grp = q_ref.shape[0] // nkv
    nt = (((1,), (1,)), ((), ()))
    hd = q_ref.shape[2]
    rows = grp * bk
    seq_c = kx_ref.shape[1]
    if local:
        row = lax.broadcasted_iota(jnp.int32, (rows, 3 * bk + seq_c), 0) % bk
        col = lax.broadcasted_iota(jnp.int32, (rows, 3 * bk + seq_c), 1)
        key_pos = (i - 1) * bk + col
        valid = (col >= 3 * bk) | ((jnp.abs(col - (row + bk)) <= SWA_WINDOW) & (key_pos >= 0) & (key_pos < seq))
    for kv in range(nkv):
        if local:
            k_all = jnp.concatenate([kp_ref[kv], kc_ref[kv], kn_ref[kv], kx_ref[kv]], axis=0).astype(BF16)
            v_all = jnp.concatenate([vp_ref[kv], vc_ref[kv], vn_ref[kv], vx_ref[kv]], axis=0).astype(BF16)
        else:
            k_all = kx_ref[kv].astype(BF16)
            v_all = vx_ref[kv].astype(BF16)
        q = (q_ref[kv * grp:(kv + 1) * grp].reshape(rows, hd) * scale).astype(BF16)
        sink = jnp.concatenate([jnp.broadcast_to(sink_ref[kv * grp + gi], (bk, 1)) for gi in range(grp)], axis=0)
        s = lax.dot_general(q, k_all, nt, preferred_element_type=F32)
        if local:
            s = jnp.where(valid, s, NEG_INF)
        m = jnp.maximum(jnp.max(s, axis=-1, keepdims=True), sink)
        p = jnp.exp(s - m)
        den = jnp.sum(p, axis=-1, keepdims=True) + jnp.exp(sink - m)
        o = jnp.dot(p.astype(BF16), v_all, preferred_element_type=F32)
        o_ref[kv * grp:(kv + 1) * grp] = (o / den).reshape(grp, bk, hd)


def _swa(q, k, v, kx, vx, sink, bsz, seq, seq_c, local):
    hq, m, hd = q.shape
    hkv = kx.shape[0]
    bk = SWA_BLOCK
    nb = seq // bk
    scale = hd ** -0.5
    cur = lambda b, i: (0, b * nb + i, 0)
    prv = lambda b, i: (0, b * nb + jnp.maximum(i - 1, 0), 0)
    nxt = lambda b, i: (0, b * nb + jnp.minimum(i + 1, nb - 1), 0)
    ctx_spec = pl.BlockSpec((hkv, seq_c, hd), lambda b, i: (0, b, 0))
    sink_spec = pl.BlockSpec((hq, 1, 1), lambda b, i: (0, 0, 0))
    kvb = lambda f: pl.BlockSpec((hkv, bk, hd), f)
    in_specs = [pl.BlockSpec((hq, bk, hd), cur)]
    args = [q]
    if local:
        in_specs += [kvb(prv), kvb(cur), kvb(nxt), kvb(prv), kvb(cur), kvb(nxt)]
        args += [k, k, k, v, v, v]
    in_specs += [ctx_spec, ctx_spec, sink_spec]
    args += [kx, vx, sink.astype(F32).reshape(hq, 1, 1)]
    return pl.pallas_call(
        functools.partial(_swa_body, local=local, seq=seq, scale=scale),
        grid=(bsz, nb),
        in_specs=in_specs,
        out_specs=pl.BlockSpec((hq, bk, hd), cur),
        out_shape=jax.ShapeDtypeStruct((hq, m, hd), F32),
        compiler_params=_cparams(("parallel", "parallel")),
        name="swa" if local else "swa_ctx",
    )(*args)


def _mixer_cd_in(x2, g, sc, sh, w_in, d, seq, tm, rows_per_batch, name, rope):
    hd = d // 16
    qk, vw = RET_HEADS * hd, 2 * RET_HEADS * hd
    qw, kw = SWA_Q_HEADS * hd, SWA_KV_HEADS * hd
    if rope:
        t2 = _rope_tables_2d(seq, hd, 1)
        tables = [_rope_tables_1d(seq, hd, RET_HEADS), [jnp.tile(t, (1, SWA_Q_HEADS)) for t in t2],
                  [jnp.tile(t, (1, SWA_KV_HEADS)) for t in t2]]
        post = [(0, hd // 2, hd ** -0.5), (0, hd // 2, 1.0), (1, hd // 4, 1.0), (2, hd // 4, 1.0)]
    else:
        tables = []
        post = [(None, 0, hd ** -0.5), None, None, None]
    splits = [(qk, RET_HEADS, post[0]), (qk, RET_HEADS, post[1]), (vw, RET_HEADS, None), (vw, RET_HEADS, None),
              (qw, SWA_Q_HEADS, post[2]), (kw, SWA_KV_HEADS, post[3]), (kw, SWA_KV_HEADS, None)]
    names = ("rq", "rk", "rv", "rg", "sq", "sk", "sv")
    return dict(zip(names, _norm_mod_matmul(x2, g, sc, sh, w_in, splits, tm, rows_per_batch, name, tables)))


def _mixer_cd_core(lat, cx, bsz, seq, seq_c, decay_logit, sink, with_ctx):
    dk, dv = lat["rq"].shape[2], lat["rv"].shape[2]
    tabs = _ret_tables(decay_logit)
    zero = jnp.zeros((bsz, RET_HEADS, dk, dv), F32)
    ro_c, fin_f, fin_b = _retention(cx["rq"], cx["rk"], cx["rv"], cx["rg"], bsz, seq_c, tabs, zero, zero)
    ro_l, _, _ = _retention(lat["rq"], lat["rk"], lat["rv"], lat["rg"], bsz, seq, tabs, fin_f, fin_b)
    so_l = _swa(lat["sq"], lat["sk"], lat["sv"], cx["sk"], cx["sv"], sink, bsz, seq, seq_c, True)
    so_c = None
    if with_ctx:
        so_c = _swa(cx["sq"], None, None, cx["sk"], cx["sv"], sink, bsz, seq_c, seq_c, False)
    return ro_l, so_l, (ro_c if with_ctx else None), so_c


def _store_tile_rows(ref, val, base=0):
    r, d = val.shape
    sub = d // LANES
    for c in range(sub):
        ref[pl.ds(base + c, r, stride=sub), :] = val[:, c * LANES:(c + 1) * LANES]


def _load_tile_rows(ref, r, sub, base=0):
    return jnp.concatenate([ref[pl.ds(base + c, r, stride=sub), :] for c in range(sub)], axis=1)


def _router_body(x_ref, g_ref, rw_ref, rb_ref, tril_ref, sc_ref, sh_ref, hx_ref, ti_ref, gt_ref, pos_ref, cnt_ref):
    @pl.when(pl.program_id(0) == 0)
    def _():
        cnt_ref[...] = jnp.zeros_like(cnt_ref)

    hx = _norm_mod(x_ref[...], g_ref[...], sc_ref[...], sh_ref[...])
    _store_tile_rows(hx_ref, hx)
    logits = jnp.dot(hx, rw_ref[...], precision=HI, preferred_element_type=F32) + rb_ref[...]
    lane = lax.broadcasted_iota(jnp.int32, logits.shape, 1).astype(F32)
    rem = logits
    vals, hots = [], []
    for k in range(TOP_K):
        m = jnp.max(rem, axis=-1, keepdims=True)
        idx = jnp.min(jnp.where(rem == m, lane, float(N_EXPERTS)), axis=-1, keepdims=True)
        hot = lane == idx
        rem = jnp.where(hot, NEG_INF, rem)
        vals.append(m)
        hots.append(hot.astype(F32))
        ti_ref[:, k:k + 1] = idx.astype(jnp.int32)
    exps = [jnp.exp(v - vals[0]) for v in vals]
    den = exps[0] + exps[1] + exps[2] + exps[3]
    for k in range(TOP_K):
        gt_ref[:, k:k + 1] = exps[k] / den
    sel = hots[0] + hots[1] + hots[2] + hots[3]
    before = jnp.dot(tril_ref[...], sel.astype(BF16), preferred_element_type=F32) + cnt_ref[...]
    for k in range(TOP_K):
        pos_ref[:, k:k + 1] = jnp.sum(hots[k] * before, axis=-1, keepdims=True).astype(jnp.int32)
    cnt_ref[...] += jnp.sum(sel, axis=0, keepdims=True)


def _moe_route(x2, g, sc, sh, router_w, router_b, tr, rows_per_batch):
    n, d = x2.shape
    tiles_per_batch = rows_per_batch // tr
    tril = jnp.asarray(np.tril(np.ones((tr, tr), np.float32), -1)).astype(BF16)
    whole = lambda a: pl.BlockSpec(a.shape, lambda i, nd=a.ndim: (0,) * nd)
    bat = pl.BlockSpec((None, 1, d), lambda i: (i // tiles_per_batch, 0, 0))
    g2, rb2 = g.reshape(1, -1), router_b.reshape(1, -1)
    small = lambda dt: jax.ShapeDtypeStruct((n, TOP_K), dt)
    small_spec = pl.BlockSpec((tr, TOP_K), lambda i: (i, 0))
    return pl.pallas_call(
        _router_body,
        grid=(n // tr,),
        in_specs=[pl.BlockSpec((tr, d), lambda i: (i, 0)), whole(g2), whole(router_w), whole(rb2), whole(tril), bat, bat],
        out_specs=[pl.BlockSpec((tr * (d // LANES), LANES), lambda i: (i, 0)), small_spec, small_spec, small_spec,
                   pl.BlockSpec((1, N_EXPERTS), lambda i: (0, 0))],
        out_shape=[jax.ShapeDtypeStruct((n * (d // LANES), LANES), F32), small(jnp.int32), small(F32), small(jnp.int32),
                   jax.ShapeDtypeStruct((1, N_EXPERTS), F32)],
        compiler_params=_cparams(("arbitrary",)),
        name="moe_router",
    )(x2, g2, router_w, rb2, tril, sc, sh)


def _tile_row_copy(src, s_off, dst, d_off, sem, sub):
    return pltpu.make_async_copy(src.at[pl.ds(pl.multiple_of(s_off, sub), sub)],
                                 dst.at[pl.ds(pl.multiple_of(d_off, sub), sub)], sem)


def _dispatch_body(dest_ref, hx_ref, xs_in_ref, xs_ref, sem, *, td, sub):
    del xs_in_ref

    def issue(n, carry):
        for k in range(TOP_K):
            _tile_row_copy(hx_ref, n * sub, xs_ref, dest_ref[n * TOP_K + k], sem, sub).start(priority=k % 2)
        return carry

    def drain(n, carry):
        for k in range(TOP_K):
            _tile_row_copy(hx_ref, 0, xs_ref, 0, sem, sub).wait()
        return carry

    lax.fori_loop(0, td, issue, 0, unroll=DMA_UNROLL)
    lax.fori_loop(0, td, drain, 0, unroll=DMA_UNROLL)


def _moe_dispatch(hx, dest_off, n_slots, td, sub, xs_init=None):
    lanes = hx.shape[1]
    n = hx.shape[0] // sub
    zeros = jnp.zeros((n_slots * sub, lanes), F32) if xs_init is None else xs_init
    return pl.pallas_call(
        functools.partial(_dispatch_body, td=td, sub=sub),
        grid=(n // td,),
        in_specs=[pl.BlockSpec((td * TOP_K,), lambda i: (i,), memory_space=pltpu.SMEM),
                  pl.BlockSpec((td * sub, lanes), lambda i: (i, 0)), pl.BlockSpec(memory_space=pl.ANY)],
        out_specs=pl.BlockSpec(memory_space=pl.ANY),
        out_shape=jax.ShapeDtypeStruct((n_slots * sub, lanes), F32),
        scratch_shapes=[pltpu.SemaphoreType.DMA(())],
        input_output_aliases={2: 0},
        compiler_params=pltpu.CompilerParams(dimension_semantics=("arbitrary",), has_side_effects=True,
                                             vmem_limit_bytes=VMEM_LIMIT),
        name="moe_dispatch",
    )(dest_off, hx, zeros)


def _ffn_body(be_ref, nu_ref, x_ref, wgu_ref, bgu_ref, wdn_ref, bdn_ref, o_ref, wgu_bf, wdn_bf, *, tm):
    j = pl.program_id(0)
    e = be_ref[j]
    prev = be_ref[jnp.maximum(j - 1, 0)]

    @pl.when((j == 0) | (e != prev))
    def _():
        wgu_bf[...] = wgu_ref[...].astype(BF16)
        wdn_bf[...] = wdn_ref[...].astype(BF16)

    @pl.when(j < nu_ref[0])
    def _():
        f = wdn_ref.shape[0]
        x = _load_tile_rows(x_ref, tm, wgu_ref.shape[0] // LANES).astype(BF16)
        gu = jnp.dot(x, wgu_bf[...], preferred_element_type=F32) + bgu_ref[...]
        gate = jnp.minimum(gu[:, :f], SWIGLU_LIMIT)
        up = jnp.clip(gu[:, f:], -SWIGLU_LIMIT, SWIGLU_LIMIT)
        act = gate * jax.nn.sigmoid(SWIGLU_ALPHA * gate) * (up + 1.0)
        _store_tile_rows(o_ref, jnp.dot(act.astype(BF16), wdn_bf[...], preferred_element_type=F32) + bdn_ref[...])

    @pl.when(j >= nu_ref[0])
    def _():
        o_ref[...] = jnp.zeros_like(o_ref)


def _moe_ffn(xs, block_exp, n_used, layer, w_gu, b_gu, w_dn, b_dn, tm):
    depth, n_exp, d, f2 = w_gu.shape
    f = w_dn.shape[2]
    sub = d // LANES
    n_slots = xs.shape[0] // sub
    blk = lambda j, be, nu: (jnp.minimum(j, nu[0] - 1), 0)
    exp4 = lambda j, be, nu: (layer, be[j], 0, 0)
    grid_spec = pltpu.PrefetchScalarGridSpec(
        num_scalar_prefetch=2,
        grid=(n_slots // tm,),
        in_specs=[pl.BlockSpec((tm * sub, LANES), blk), pl.BlockSpec((None, None, d, f2), exp4),
                  pl.BlockSpec((None, None, 1, f2), exp4), pl.BlockSpec((None, None, f, d), exp4),
                  pl.BlockSpec((None, None, 1, d), exp4)],
        out_specs=pl.BlockSpec((tm * sub, LANES), lambda j, be, nu: (j, 0)),
        scratch_shapes=[pltpu.VMEM((d, f2), BF16), pltpu.VMEM((f, d), BF16)],
    )
    return pl.pallas_call(
        functools.partial(_ffn_body, tm=tm),
        grid_spec=grid_spec,
        out_shape=jax.ShapeDtypeStruct((n_slots * sub, LANES), F32),
        compiler_params=_cparams(("arbitrary",)),
        name="moe_ffn",
    )(block_exp, n_used, xs, w_gu, b_gu.reshape(depth, n_exp, 1, f2), w_dn, b_dn.reshape(depth, n_exp, 1, d))


def _combine_body(dest_ref, gt_ref, x_ref, ys_ref, g_ref, o_ref, buf, sem, *, tc, sub):
    def issue(n, carry):
        for k in range(TOP_K):
            _tile_row_copy(ys_ref, dest_ref[n * TOP_K + k], buf, (k * tc + n) * sub, sem, sub).start(priority=k % 2)
        return carry

    def drain(n, carry):
        for k in range(TOP_K):
            _tile_row_copy(ys_ref, 0, buf, 0, sem, sub).wait()
        return carry

    lax.fori_loop(0, tc, issue, 0, unroll=DMA_UNROLL)
    lax.fori_loop(0, tc, drain, 0, unroll=DMA_UNROLL)
    gates = [jnp.broadcast_to(gt_ref[:, k:k + 1], (tc, LANES)) for k in range(TOP_K)]
    for c in range(sub):
        cols = slice(c * LANES, (c + 1) * LANES)
        acc = gates[0] * buf[pl.ds(c, tc, stride=sub), :]
        for k in range(1, TOP_K):
            acc = acc + gates[k] * buf[pl.ds(k * tc * sub + c, tc, stride=sub), :]
        o_ref[:, cols] = x_ref[:, cols] + g_ref[:, cols] * acc


def _moe_combine(ys, dest_off, gates, x2, gate2, tc, rows_per_batch):
    n, d = x2.shape
    sub = d // LANES
    tiles_per_batch = rows_per_batch // tc
    return pl.pallas_call(
        functools.partial(_combine_body, tc=tc, sub=sub),
        grid=(n // tc,),
        in_specs=[pl.BlockSpec((tc * TOP_K,), lambda i: (i,), memory_space=pltpu.SMEM),
                  pl.BlockSpec((tc, TOP_K), lambda i: (i, 0)), pl.BlockSpec((tc, d), lambda i: (i, 0)),
                  pl.BlockSpec(memory_space=pl.ANY),
                  pl.BlockSpec((None, 1, d), lambda i: (i // tiles_per_batch, 0, 0))],
        out_specs=pl.BlockSpec((tc, d), lambda i: (i, 0)),
        out_shape=jax.ShapeDtypeStruct((n, d), F32),
        scratch_shapes=[pltpu.VMEM((TOP_K * tc * sub, LANES), F32), pltpu.SemaphoreType.DMA(())],
        compiler_params=_cparams(("arbitrary",)),
        name="moe_combine",
    )(dest_off, gates, x2, ys, gate2)


def _moe_slot_blocks(n_tok):
    return -(-n_tok * TOP_K // MOE_TM) + N_EXPERTS


def _moe_layer(streams, g, router_w, router_b, layer, w_gu, b_gu, w_dn, b_dn, n_blocks, slots=None):
    tm = MOE_TM
    sub = streams[0][0].shape[1] // LANES
    routed = [_moe_route(x2, g, sc, sh, router_w, router_b, tr, rpb) for x2, sc, sh, _, rpb, tr, _ in streams]
    counts = [r[4][0].astype(jnp.int32) for r in routed]
    total = functools.reduce(lambda a, b: a + b, counts)
    padded = (total + tm - 1) // tm * tm
    pad_end = jnp.cumsum(padded)
    assert n_blocks >= _moe_slot_blocks(sum(s[0].shape[0] for s in streams))
    n_used = (pad_end[-1] // tm).astype(jnp.int32)
    blk_ids = jnp.arange(n_blocks, dtype=jnp.int32)
    last_row = jnp.minimum(blk_ids, n_used - 1) * tm
    block_exp = jnp.sum((pad_end[None, :] <= last_row[:, None]).astype(jnp.int32), axis=1)
    block_exp = jnp.minimum(block_exp, N_EXPERTS - 1)
    start = pad_end - padded
    xs, dests = slots, []
    for (x2, _, _, _, _, tr, _), (hx, top_i, _, pos, _), cnt in zip(streams, routed, counts):
        dest_off = ((start[top_i] + pos) * sub).astype(jnp.int32).reshape(-1)
        xs = _moe_dispatch(hx, dest_off, n_blocks * tm, tr, sub, xs)
        dests.append(dest_off)
        start = start + cnt
    ys = _moe_ffn(xs, block_exp, n_used.reshape(1), layer, w_gu, b_gu, w_dn, b_dn, tm)
    outs = [_moe_combine(ys, dest_off, r[2], x2, gate2, tc, rpb)
            for (x2, _, _, gate2, rpb, _, tc), r, dest_off in zip(streams, routed, dests)]
    return outs, ys


def _mod_body(c_ref, w_ref, b_ref, o_ref):
    o_ref[...] = jnp.dot(jax.nn.silu(c_ref[...]), w_ref[...], precision=HI, preferred_element_type=F32) + b_ref[...]


def _modulation(cc, mod_w, mod_b):
    depth, d, d6 = mod_w.shape
    r = cc.shape[0]
    return pl.pallas_call(
        _mod_body,
        grid=(depth, d6 // d),
        in_specs=[pl.BlockSpec((r, d), lambda l, j: (0, 0)), pl.BlockSpec((None, d, d), lambda l, j: (l, 0, j)),
                  pl.BlockSpec((None, 1, d), lambda l, j: (l, 0, j))],
        out_specs=pl.BlockSpec((None, r, d), lambda l, j: (l, 0, j)),
        out_shape=jax.ShapeDtypeStruct((depth, r, d6), F32),
        compiler_params=_cparams(("parallel", "parallel")),
        name="modulation",
    )(cc, mod_w, mod_b.reshape(depth, 1, d6))


def kernel(x, c, ctx, c_ctx, mod_w, mod_b, norm1_g, norm2_g, ab_w_in, ab_w_out, hy_short_w, hy_short_b, hy_f_w1, hy_f_b1, hy_f_w2, hy_f_b2, hy_f_w3, hy_bias, s5_lambda_re, s5_lambda_im, s5_log_dt, s5_b_re, s5_b_im, s5_c_re, s5_c_im, s5_d, s5_glu_w, s5_glu_b, cd_w_in, cd_w_out, ret_decay_logit, swa_sink, router_w, router_b, exp_w_gu, exp_b_gu, exp_w_down, exp_b_down, final_g):
    bsz, seq, d = x.shape
    seq_c = ctx.shape[1]
    depth = mod_w.shape[0]
    m_l, m_c = bsz * seq, bsz * seq_c
    tm, tm_c, tt = 512, 256, 256
    xl = x.reshape(m_l, d)
    xc = ctx.reshape(m_c, d)
    cc = jnp.concatenate([c, c_ctx[None], jnp.zeros((SUBLANES - bsz - 1, d), F32)], axis=0)
    mods = _modulation(cc, mod_w, mod_b)
    hy_w = hy_f_w3.shape[2]
    n_blocks = _moe_slot_blocks(m_l + m_c)
    slots = None
    for layer in range(depth):
        with_ctx = layer < depth - 1
        i = layer // 2
        sh1, sc1, g1, sh2, sc2, g2 = [t[:, None, :] for t in jnp.split(mods[layer, :bsz], 6, axis=-1)]
        csh1, csc1, cg1, csh2, csc2, cg2 = [t[:, None, :] for t in jnp.split(mods[layer, bsz:bsz + 1], 6, axis=-1)]
        if layer % 2 == 0:
            w_in, w_out = ab_w_in[i], ab_w_out[i]
            splits = [(3 * hy_w, 0, None), (w_in.shape[1] - 3 * hy_w, 0, None)]
            pa, pb = _norm_mod_matmul(xl, norm1_g[layer], sc1, sh1, w_in, splits, tm, seq, "ab_in")
            pac, pbc = _norm_mod_matmul(xc, norm1_g[layer], csc1, csh1, w_in, splits, tm_c, m_c, "ab_in_ctx")
            hy = (hy_short_w[i], hy_short_b[i], hy_f_w1[i], hy_f_b1[i], hy_f_w2[i], hy_f_b2[i], hy_f_w3[i], hy_bias[i])
            ya = _hyena(pa, bsz, seq, *hy, tt)
            yb, ybc = _s5_mixer(pb, pbc, bsz, seq, seq_c, s5_lambda_re[i], s5_lambda_im[i], s5_log_dt[i], s5_b_re[i],
                                s5_b_im[i], s5_c_re[i], s5_c_im[i], s5_d[i], s5_glu_w[i], s5_glu_b[i], tm, tm_c)
            ws = [w_out[:hy_w], w_out[hy_w:]]
            xl = _out_proj([ya, yb], ws, xl, g1, tm, seq, "ab_out")
            if with_ctx:
                yac = _hyena(pac, bsz, seq_c, *hy, tt)
                xc = _out_proj([yac, ybc], ws, xc, cg1, tm_c, m_c, "ab_out_ctx")
        else:
            w_in, w_out = cd_w_in[i], cd_w_out[i]
            vw = 2 * RET_HEADS * (d // 16)
            qw = SWA_Q_HEADS * (d // 16)
            lat = _mixer_cd_in(xl, norm1_g[layer], sc1, sh1, w_in, d, seq, tm, seq, "cd_in", True)
            cx = _mixer_cd_in(xc, norm1_g[layer], csc1, csh1, w_in, d, seq_c, tm_c, m_c, "cd_in_ctx", False)
            ro_l, so_l, ro_c, so_c = _mixer_cd_core(lat, cx, bsz, seq, seq_c, ret_decay_logit[i], swa_sink[i],
                                                    with_ctx)
            ws = [w_out[:vw].reshape(RET_HEADS, vw // RET_HEADS, d), w_out[vw:].reshape(SWA_Q_HEADS, qw // SWA_Q_HEADS, d)]
            xl = _out_proj([ro_l, so_l], ws, xl, g1, tm, seq, "cd_out")
            if with_ctx:
                xc = _out_proj([ro_c, so_c], ws, xc, cg1, tm_c, m_c, "cd_out_ctx")
        moe_w = (router_w[layer], router_b[layer], layer, exp_w_gu, exp_b_gu, exp_w_down, exp_b_down)
        streams = [(xl, sc2, sh2, g2, seq, tm, tm_c)]
        if with_ctx:
            streams.append((xc, csc2, csh2, cg2, m_c, tm_c, tm_c))
        outs, slots = _moe_layer(streams, norm2_g[layer], *moe_w, n_blocks, slots)
        xl = outs[0]
        if with_ctx:
            xc = outs[1]
    out = _rows_call(_final_norm_body, [xl], [final_g.reshape(1, -1)], [], [((m_l, d), F32)], tm, m_l, "final_norm")[0]
    return out.reshape(bsz, seq, d)
```
